```python
import jax
import jax.numpy as jnp
from jax import lax
import numpy as np

D_MODEL = 1024
BATCH = 8
SEQ = 4096
DEPTH = 1

CHUNK = 64
Q_BLOCK = 128
LN_EPS = 1e-5
FOX_HEADS = 8
FOX_HEAD_DIM = 64
FOX_WIDTH = FOX_HEADS * FOX_HEAD_DIM
GLA_HEADS = 4
GLA_KEY_DIM = 64
GLA_VALUE_DIM = 128
GLA_KEY_WIDTH = GLA_HEADS * GLA_KEY_DIM
GLA_WIDTH = GLA_HEADS * GLA_VALUE_DIM
GLA_GATE_RANK = 16
GLA_GATE_TEMP = 16.0
MIX_WIDTH = FOX_WIDTH + GLA_WIDTH
N_EXPERTS = 256
N_GROUPS = 8
TOPK_GROUPS = 4
TOP_K = 8
EXPERT_DIM = 256
SHARED_DIM = 256
ROUTED_SCALE = 2.5
EXPERT_ROWS = 64
DEEPNORM_ALPHA = (2.0 * DEPTH) ** 0.25
DEEPNORM_BETA = (8.0 * DEPTH) ** -0.25
IN_SPLITS = (FOX_WIDTH, FOX_WIDTH, FOX_WIDTH, FOX_HEADS, GLA_KEY_WIDTH, GLA_KEY_WIDTH, GLA_WIDTH, GLA_GATE_RANK, GLA_WIDTH)
IN_SCALES = (1.0, 1.0, DEEPNORM_BETA, 1.0, 1.0, 1.0, DEEPNORM_BETA, 1.0, 1.0)
IN_WIDTH = sum(IN_SPLITS)

kernel_name = 'hybrid_fox_gla_moe_deepnorm_layer'


def layer_norm(x, g, b):
    xf = x.astype(jnp.float32)
    mu = jnp.mean(xf, axis=-1, keepdims=True)
    xc = xf - mu
    var = jnp.mean(xc * xc, axis=-1, keepdims=True)
    return (xc * lax.rsqrt(var + LN_EPS) * g + b).astype(x.dtype)


def rms_norm(x, g):
    xf = x.astype(jnp.float32)
    return xf * lax.rsqrt(jnp.mean(xf * xf, axis=-1, keepdims=True) + LN_EPS) * g


def forgetting_attention(q, k, v, log_f):
    B, S, H, Dh = q.shape
    scale = Dh ** -0.5
    c = jnp.cumsum(log_f, axis=1).transpose(0, 2, 1)
    outs = []
    for i in range(S // Q_BLOCK):
        q0, q1 = i * Q_BLOCK, (i + 1) * Q_BLOCK
        logits = jnp.einsum('bqhd,bkhd->bhqk', q[:, q0:q1], k[:, :q1]).astype(jnp.float32) * scale
        logits = logits + (c[:, :, q0:q1, None] - c[:, :, None, :q1])
        causal = jnp.arange(q0, q1)[:, None] >= jnp.arange(q1)[None, :]
        p = jax.nn.softmax(jnp.where(causal, logits, -jnp.inf), axis=-1)
        outs.append(jnp.einsum('bhqk,bkhd->bqhd', p.astype(v.dtype), v[:, :q1]))
    return jnp.concatenate(outs, axis=1)


def gla_chunked(q, k, v, log_a):
    B, S, H, Dk = q.shape
    Dv = v.shape[-1]
    N = S // CHUNK

    def to_chunks(t):
        return t.astype(jnp.float32).reshape(B, N, CHUNK, H, -1).transpose(1, 0, 3, 2, 4)

    qc = to_chunks(q) * (Dk ** -0.5)
    kc = to_chunks(k)
    vc = to_chunks(v)
    lac = to_chunks(log_a)
    causal = jnp.tril(jnp.ones((CHUNK, CHUNK), dtype=bool))

    def step(state, inp):
        qi, ki, vi, lai = inp
        b = jnp.cumsum(lai, axis=2)
        o_inter = jnp.einsum('bhtd,bhde->bhte', qi * jnp.exp(b), state)
        diff = b[:, :, :, None, :] - b[:, :, None, :, :]
        decay = jnp.exp(jnp.where(causal[:, :, None], diff, -jnp.inf))
        scores = jnp.einsum('bhtd,bhsd,bhtsd->bhts', qi, ki, decay)
        o_intra = jnp.einsum('bhts,bhse->bhte', scores, vi)
        b_last = b[:, :, -1:, :]
        new_state = (jnp.exp(b_last[:, :, 0, :, None]) * state
                     + jnp.einsum('bhsd,bhse->bhde', ki * jnp.exp(b_last - b), vi))
        return new_state, o_inter + o_intra

    state0 = jnp.zeros((B, H, Dk, Dv), jnp.float32)
    _, o = lax.scan(step, state0, (qc, kc, vc, lac))
    return o.transpose(1, 0, 3, 2, 4).reshape(B, S, H, Dv)


def hybrid_mixer(h, w_in, b_fgate, w_gate_up, b_gate, g_gla_norm, w_out):
    B, S, _ = h.shape
    proj = jnp.einsum('bsd,de->bse', h, w_in)
    fq, fk, fv, ff, gq, gk, gv, ga, gr = jnp.split(proj, np.cumsum(IN_SPLITS)[:-1].tolist(), axis=-1)
    log_f = jax.nn.log_sigmoid((ff + b_fgate).astype(jnp.float32))
    y_fox = forgetting_attention(fq.reshape(B, S, FOX_HEADS, FOX_HEAD_DIM),
                                 fk.reshape(B, S, FOX_HEADS, FOX_HEAD_DIM),
                                 fv.reshape(B, S, FOX_HEADS, FOX_HEAD_DIM), log_f)
    y_fox = y_fox.reshape(B, S, FOX_WIDTH).astype(h.dtype)
    gate_logits = jnp.einsum('bsr,rk->bsk', ga, w_gate_up) + b_gate
    log_a = jax.nn.log_sigmoid(gate_logits.astype(jnp.float32)) / GLA_GATE_TEMP
    o = gla_chunked(gq.reshape(B, S, GLA_HEADS, GLA_KEY_DIM),
                    gk.reshape(B, S, GLA_HEADS, GLA_KEY_DIM),
                    gv.reshape(B, S, GLA_HEADS, GLA_VALUE_DIM),
                    log_a.reshape(B, S, GLA_HEADS, GLA_KEY_DIM))
    o = rms_norm(o, g_gla_norm) * jax.nn.silu(gr.astype(jnp.float32)).reshape(B, S, GLA_HEADS, GLA_VALUE_DIM)
    y_gla = o.reshape(B, S, GLA_WIDTH).astype(h.dtype)
    y = jnp.concatenate([y_fox, y_gla], axis=-1)
    return jnp.einsum('bse,ed->bsd', y, w_out)


def swiglu(x, wg, wu, wd):
    return (jax.nn.silu(x @ wg) * (x @ wu)) @ wd


def moe_ffn(h, w_router, router_bias, w_exp_gate, w_exp_up, w_exp_down, w_sh_gate, w_sh_up, w_sh_down):
    B, S, D = h.shape
    xf = h.reshape(-1, D)
    T = B * S
    scores = jax.nn.sigmoid(jnp.einsum('td,de->te', xf.astype(jnp.float32), w_router.astype(jnp.float32)))
    biased = scores + router_bias.astype(jnp.float32)
    grp_score = lax.top_k(biased.reshape(T, N_GROUPS, N_EXPERTS // N_GROUPS), 2)[0].sum(-1)
    _, grp_idx = lax.top_k(grp_score, TOPK_GROUPS)
    grp_mask = jnp.any(grp_idx[:, :, None] == jnp.arange(N_GROUPS)[None, None, :], axis=1)
    expert_mask = jnp.repeat(grp_mask, N_EXPERTS // N_GROUPS, axis=1)
    _, top_idx = lax.top_k(jnp.where(expert_mask, biased, -jnp.inf), TOP_K)
    top_w = jnp.take_along_axis(scores, top_idx, axis=1)
    top_w = top_w / jnp.sum(top_w, axis=-1, keepdims=True) * ROUTED_SCALE
    A = T * TOP_K
    eid = top_idx.reshape(-1).astype(jnp.int32)
    tok = jnp.repeat(jnp.arange(T, dtype=jnp.int32), TOP_K)
    gate = top_w.reshape(-1)
    order = jnp.argsort(eid)
    eid_s, tok_s, gate_s = eid[order], tok[order], gate[order]
    counts = jnp.bincount(eid, length=N_EXPERTS).astype(jnp.int32)
    padded = (counts + EXPERT_ROWS - 1) // EXPERT_ROWS * EXPERT_ROWS
    start = jnp.cumsum(counts) - counts
    pend = jnp.cumsum(padded)
    pstart = pend - padded
    dest = pstart[eid_s] + jnp.arange(A, dtype=jnp.int32) - start[eid_s]
    P = -(-A // EXPERT_ROWS) * EXPERT_ROWS + N_EXPERTS * EXPERT_ROWS
    nb = P // EXPERT_ROWS
    row_tok = jnp.zeros((P,), jnp.int32).at[dest].set(tok_s)
    row_gate = jnp.zeros((P,), jnp.float32).at[dest].set(gate_s)
    blk_exp = jnp.minimum(jnp.searchsorted(pend, jnp.arange(nb, dtype=jnp.int32) * EXPERT_ROWS, side='right'),
                          N_EXPERTS - 1)

    def body(acc, inp):
        rt, rg, e = inp
        y = swiglu(xf[rt], w_exp_gate[e], w_exp_up[e], w_exp_down[e]).astype(jnp.float32) * rg[:, None]
        return acc.at[rt].add(y), None

    routed, _ = lax.scan(body, jnp.zeros((T, D), jnp.float32),
                         (row_tok.reshape(nb, EXPERT_ROWS), row_gate.reshape(nb, EXPERT_ROWS), blk_exp))
    shared = swiglu(xf, w_sh_gate, w_sh_up, w_sh_down).astype(jnp.float32)
    return (routed + shared).astype(h.dtype).reshape(B, S, D)


def setup_inputs(seed: int = 0) -> dict:
    key = jax.random.key(seed)
    ks = jax.random.split(key, 24)
    L, D = DEPTH, D_MODEL

    def nrm(k, shape, scale):
        return jax.random.normal(k, shape, jnp.float32) * scale

    col_scale = jnp.asarray(np.concatenate([np.full((n,), s, np.float32) for n, s in zip(IN_SPLITS, IN_SCALES)]))
    return {
        'x': jax.random.normal(ks[0], (BATCH, SEQ, D), jnp.float32),
        'ln_in_g': 1.0 + nrm(ks[1], (D,), 0.02),
        'ln_in_b': nrm(ks[2], (D,), 0.02),
        'w_in': nrm(ks[3], (L, D, IN_WIDTH), D ** -0.5) * col_scale,
        'b_fgate': jax.random.uniform(ks[4], (L, FOX_HEADS), jnp.float32, 1.0, 4.0),
        'w_gate_up': nrm(ks[5], (L, GLA_GATE_RANK, GLA_KEY_WIDTH), GLA_GATE_RANK ** -0.5),
        'b_gate': nrm(ks[6], (L, GLA_KEY_WIDTH), 0.1),
        'g_gla_norm': 1.0 + nrm(ks[7], (L, GLA_VALUE_DIM), 0.02),
        'w_out': nrm(ks[8], (L, MIX_WIDTH, D), MIX_WIDTH ** -0.5 * DEEPNORM_BETA),
        'ln1_g': 1.0 + nrm(ks[9], (L, D), 0.02),
        'ln1_b': nrm(ks[10], (L, D), 0.02),
        'w_router': nrm(ks[11], (L, D, N_EXPERTS), D ** -0.5),
        'router_bias': nrm(ks[12], (L, N_EXPERTS), 0.01),
        'w_exp_gate': nrm(ks[13], (L, N_EXPERTS, D, EXPERT_DIM), D ** -0.5),
        'w_exp_up': nrm(ks[14], (L, N_EXPERTS, D, EXPERT_DIM), D ** -0.5 * DEEPNORM_BETA),
        'w_exp_down': nrm(ks[15], (L, N_EXPERTS, EXPERT_DIM, D), EXPERT_DIM ** -0.5 * DEEPNORM_BETA),
        'w_sh_gate': nrm(ks[16], (L, D, SHARED_DIM), D ** -0.5),
        'w_sh_up': nrm(ks[17], (L, D, SHARED_DIM), D ** -0.5 * DEEPNORM_BETA),
        'w_sh_down': nrm(ks[18], (L, SHARED_DIM, D), SHARED_DIM ** -0.5 * DEEPNORM_BETA),
        'ln2_g': 1.0 + nrm(ks[19], (L, D), 0.02),
        'ln2_b': nrm(ks[20], (L, D), 0.02),
    }


def reference(x, ln_in_g, ln_in_b, w_in, b_fgate, w_gate_up, b_gate, g_gla_norm, w_out, ln1_g, ln1_b,
              w_router, router_bias, w_exp_gate, w_exp_up, w_exp_down, w_sh_gate, w_sh_up, w_sh_down,
              ln2_g, ln2_b):
    h = layer_norm(x, ln_in_g, ln_in_b)
    for l in range(DEPTH):
        mix = hybrid_mixer(h, w_in[l], b_fgate[l], w_gate_up[l], b_gate[l], g_gla_norm[l], w_out[l])
        h = layer_norm(DEEPNORM_ALPHA * h + mix, ln1_g[l], ln1_b[l])
        ffn = moe_ffn(h, w_router[l], router_bias[l], w_exp_gate[l], w_exp_up[l], w_exp_down[l],
                      w_sh_gate[l], w_sh_up[l], w_sh_down[l])
        h = layer_norm(DEEPNORM_ALPHA * h + ffn, ln2_g[l], ln2_b[l])
    return h
```

```python
import functools

import jax
import jax.numpy as jnp
import numpy as np
from jax import lax
from jax.experimental import pallas as pl
from jax.experimental.pallas import tpu as pltpu

F32 = jnp.float32
BF16 = jnp.bfloat16

D_MODEL = 1024
CHUNK = 64
LN_EPS = 1e-5
FOX_HEADS = 8
FOX_HEAD_DIM = 64
FOX_WIDTH = FOX_HEADS * FOX_HEAD_DIM
GLA_HEADS = 4
GLA_KEY_DIM = 64
GLA_VALUE_DIM = 128
GLA_KEY_WIDTH = GLA_HEADS * GLA_KEY_DIM
GLA_WIDTH = GLA_HEADS * GLA_VALUE_DIM
GLA_GATE_RANK = 16
GLA_GATE_TEMP = 16.0
N_EXPERTS = 256
N_GROUPS = 8
GROUP_SIZE = N_EXPERTS // N_GROUPS
TOPK_GROUPS = 4
TOP_K = 8
EXPERT_DIM = 256
SHARED_DIM = 256
ROUTED_SCALE = 2.5
DEPTH = 1
DEEPNORM_ALPHA = (2.0 * DEPTH) ** 0.25
IN_SPLITS = (FOX_WIDTH, FOX_WIDTH, FOX_WIDTH, FOX_HEADS, GLA_KEY_WIDTH, GLA_KEY_WIDTH, GLA_WIDTH,
             GLA_GATE_RANK, GLA_WIDTH)

LANES = 128
SUB_CHUNK = 16
VMEM_LIMIT = 48 * 1024 * 1024

TM_PROJ = 512
TQ_FOX = 512
TS_GLA = 512
TM_OPROJ = 512
TM_ROUTE = 512
BM_EXPERT = 256
TM_FINAL = 512


def _cparams(*sem):
    return pltpu.CompilerParams(dimension_semantics=sem, vmem_limit_bytes=VMEM_LIMIT)


def _layer_norm(x, g, b):
    mu = jnp.mean(x, axis=-1, keepdims=True)
    xc = x - mu
    var = jnp.mean(xc * xc, axis=-1, keepdims=True)
    return xc * lax.rsqrt(var + LN_EPS) * g + b


def _log_sigmoid(z):
    return jnp.minimum(z, 0.0) - jnp.log1p(jnp.exp(-jnp.abs(z)))


def _sigmoid(z):
    return 1.0 / (1.0 + jnp.exp(-z))


def _dot(a, b):
    return jnp.dot(a, b, preferred_element_type=F32)


def _dot_nt(a, b):
    return lax.dot_general(a, b, (((1,), (1,)), ((), ())), preferred_element_type=F32)


def _dot_tn(a, b):
    return lax.dot_general(a, b, (((0,), (0,)), ((), ())), preferred_element_type=F32)


def _proj_kernel(x_ref, g_ref, b_ref, wm_ref, ws_ref,
                 h_ref, fq_ref, fk_ref, fv_ref, gq_ref, gk_ref, gv_ref, gr_ref, sm_ref):
    h = _layer_norm(x_ref[...], g_ref[...], b_ref[...])
    h_ref[...] = h
    hb = h.astype(BF16)
    off = 0
    for ref in (fq_ref, fk_ref, fv_ref, gq_ref, gk_ref, gv_ref, gr_ref):
        n = ref.shape[1]
        ref[...] = _dot(hb, wm_ref[:, off:off + n]).astype(ref.dtype)
        off += n
    sm_ref[...] = _dot(hb, ws_ref[...])


def _proj(x2, g, b, w_main, w_small):
    T, D = x2.shape
    tm = TM_PROJ
    widths = (FOX_WIDTH, FOX_WIDTH, FOX_WIDTH, GLA_KEY_WIDTH, GLA_KEY_WIDTH, GLA_WIDTH, GLA_WIDTH)
    row = lambda n: pl.BlockSpec((tm, n), lambda i: (i, 0))
    full = lambda a: pl.BlockSpec(a.shape, lambda i: (0,) * a.ndim)
    out_shape = [jax.ShapeDtypeStruct((T, D), F32)]
    out_shape += [jax.ShapeDtypeStruct((T, n), BF16) for n in widths]
    out_shape += [jax.ShapeDtypeStruct((T, LANES), F32)]
    out_specs = [row(D)] + [row(n) for n in widths] + [row(LANES)]
    return pl.pallas_call(
        _proj_kernel, grid=(T // tm,),
        in_specs=[row(D), full(g), full(b), full(w_main), full(w_small)],
        out_specs=out_specs, out_shape=out_shape,
        compiler_params=_cparams("parallel"), name="proj")(x2, g, b, w_main, w_small)


def _fgate_kernel(sm_ref, bf_ref, c_ref, ct_ref):
    S = sm_ref.shape[0]
    lf = _log_sigmoid(sm_ref[...] + bf_ref[...])
    rows = lax.broadcasted_iota(jnp.int32, lf.shape, 0)
    s = 1
    while s < S:
        lf = lf + jnp.where(rows >= s, pltpu.roll(lf, s, axis=0), 0.0)
        s *= 2
    c_ref[...] = lf
    ct_ref[0] = lf.T[0:FOX_HEADS, :]


def _fgate(small, bf_pad, B, S):
    T = small.shape[0]
    return pl.pallas_call(
        _fgate_kernel, grid=(B,),
        in_specs=[pl.BlockSpec((S, LANES), lambda b: (b, 0)),
                  pl.BlockSpec((1, LANES), lambda b: (0, 0))],
        out_specs=[pl.BlockSpec((S, LANES), lambda b: (b, 0)),
                   pl.BlockSpec((1, FOX_HEADS, S), lambda b: (b, 0, 0))],
        out_shape=[jax.ShapeDtypeStruct((T, LANES), F32),
                   jax.ShapeDtypeStruct((B, FOX_HEADS, S), F32)],
        compiler_params=_cparams("parallel"), name="fgate")(small, bf_pad)


def _fox_kernel(q_ref, k_ref, v_ref, c_ref, cr_ref, o_ref, *, tq):
    i = pl.program_id(1)
    dh = FOX_HEAD_DIM
    row = lax.broadcasted_iota(jnp.int32, (tq, tq), 0)
    col = lax.broadcasted_iota(jnp.int32, (tq, tq), 1)
    causal = row >= col
    for h in range(FOX_HEADS):
        lanes = slice(h * dh, (h + 1) * dh)
        q = q_ref[:, lanes]
        ct = c_ref[:, h:h + 1]

        def step(j, carry, masked):
            m, l, acc = carry
            r0 = pl.multiple_of(j * tq, tq)
            k = k_ref[pl.ds(r0, tq), lanes]
            v = v_ref[pl.ds(r0, tq), lanes]
            cs = cr_ref[0, j, h:h + 1, :]
            s = _dot_nt(q, k) + (ct - cs)
            if masked:
                s = jnp.where(causal, s, -jnp.inf)
            m_new = jnp.maximum(m, jnp.max(s, axis=-1, keepdims=True))
            p = jnp.exp(s - m_new)
            alpha = jnp.exp(m - m_new)
            l = alpha * l + jnp.sum(p, axis=-1, keepdims=True)
            acc = alpha * acc + _dot(p.astype(BF16), v)
            return m_new, l, acc

        init = (jnp.full((tq, 1), -jnp.inf, F32), jnp.zeros((tq, 1), F32), jnp.zeros((tq, dh), F32))
        carry = lax.fori_loop(0, i, functools.partial(step, masked=False), init)
        _, l, acc = step(i, carry, True)
        o_ref[:, lanes] = (acc / l).astype(o_ref.dtype)


def _fox(fq, fk, fv, c, c_row, B, S):
    T = fq.shape[0]
    tq = TQ_FOX
    nq = S // tq
    return pl.pallas_call(
        functools.partial(_fox_kernel, tq=tq), grid=(B, nq),
        in_specs=[pl.BlockSpec((tq, FOX_WIDTH), lambda b, i: (b * nq + i, 0)),
                  pl.BlockSpec((S, FOX_WIDTH), lambda b, i: (b, 0)),
                  pl.BlockSpec((S, FOX_WIDTH), lambda b, i: (b, 0)),
                  pl.BlockSpec((tq, LANES), lambda b, i: (b * nq + i, 0)),
                  pl.BlockSpec((1, nq, FOX_HEADS, tq), lambda b, i: (b, 0, 0, 0))],
        out_specs=pl.BlockSpec((tq, FOX_WIDTH), lambda b, i: (b * nq + i, 0)),
        out_shape=jax.ShapeDtypeStruct((T, FOX_WIDTH), BF16),
        compiler_params=_cparams("parallel", "arbitrary"), name="fox")(fq, fk, fv, c, c_row)


def _gla_kernel(q_ref, k_ref, v_ref, r_ref, sm_ref, wg_ref, bg_ref, gn_ref, tri_ref, y_ref,
                st_ref, b_ref, o_ref, *, ts):
    dk, dv, C, SC = GLA_KEY_DIM, GLA_VALUE_DIM, CHUNK, SUB_CHUNK

    @pl.when(pl.program_id(1) == 0)
    def _():
        st_ref[...] = jnp.zeros_like(st_ref)

    z = _dot(sm_ref[...].astype(BF16), wg_ref[...]) + bg_ref[...]
    la = _log_sigmoid(z) * (1.0 / GLA_GATE_TEMP)
    hi = la.astype(BF16)
    r1 = la - hi.astype(F32)
    mid = r1.astype(BF16)
    lo = (r1 - mid.astype(F32)).astype(BF16)
    tri = tri_ref[...]
    b_ref[...] = _dot(tri, hi) + _dot(tri, mid) + _dot(tri, lo)

    row = lax.broadcasted_iota(jnp.int32, (C, C), 0)
    col = lax.broadcasted_iota(jnp.int32, (C, C), 1)
    row_in_sub = row % SC
    sub_base = (row // SC) * SC
    col16 = lax.broadcasted_iota(jnp.int32, (SC, C), 1)

    def chunk(c, carry):
        r0 = pl.multiple_of(c * C, C)
        for h in range(GLA_HEADS):
            kl = slice(h * dk, (h + 1) * dk)
            vl = slice(h * dv, (h + 1) * dv)
            q = q_ref[pl.ds(r0, C), kl].astype(F32)
            k = k_ref[pl.ds(r0, C), kl].astype(F32)
            v = v_ref[pl.ds(r0, C), vl]
            b = b_ref[pl.ds(r0, C), kl]
            st = st_ref[h]
            b_last = b[C - 1:C, :]

            o = _dot_nt((q * jnp.exp(b)).astype(BF16), st.astype(BF16))

            pieces = [jnp.zeros((SC, C), F32)]
            for i in range(1, C // SC):
                ref_row = b[SC * i - 1:SC * i, :]
                qt = (q[SC * i:SC * (i + 1), :] * jnp.exp(b[SC * i:SC * (i + 1), :] - ref_row)).astype(BF16)
                kt = (k * jnp.exp(jnp.minimum(ref_row - b, 0.0))).astype(BF16)
                pieces.append(jnp.where(col16 < SC * i, _dot_nt(qt, kt), 0.0))
            scores = jnp.concatenate(pieces, axis=0)

            for j in range(SC):
                bj = jnp.concatenate(
                    [jnp.broadcast_to(b[SC * i + j:SC * i + j + 1, :], (SC, dk)) for i in range(C // SC)], axis=0)
                kj = jnp.concatenate(
                    [jnp.broadcast_to(k[SC * i + j:SC * i + j + 1, :], (SC, dk)) for i in range(C // SC)], axis=0)
                decay = jnp.exp(jnp.where(row_in_sub >= j, b - bj, -jnp.inf))
                colv = jnp.sum(q * kj * decay, axis=-1, keepdims=True)
                scores = jnp.where(col == sub_base + j, colv, scores)

            o = o + _dot(scores.astype(BF16), v)
            o_ref[pl.ds(r0, C), vl] = o
            kh = (k * jnp.exp(b_last - b)).astype(BF16)
            st_ref[h] = st * jnp.exp(b_last) + _dot_tn(v, kh)
        return carry

    lax.fori_loop(0, ts // C, chunk, 0)

    for h in range(GLA_HEADS):
        vl = slice(h * dv, (h + 1) * dv)
        o = o_ref[:, vl]
        ms = jnp.mean(o * o, axis=-1, keepdims=True)
        g = r_ref[:, vl].astype(F32)
        y = o * lax.rsqrt(ms + LN_EPS) * gn_ref[...] * (g * _sigmoid(g))
        y_ref[:, vl] = y.astype(y_ref.dtype)


def _gla(gq, gk, gv, gr, small, wg_pad, bg, gnorm, tri, B, S):
    T = gq.shape[0]
    ts = TS_GLA
    ns = S // ts
    row = lambda n: pl.BlockSpec((ts, n), lambda b, i: (b * ns + i, 0))
    full = lambda a: pl.BlockSpec(a.shape, lambda b, i: (0,) * a.ndim)
    return pl.pallas_call(
        functools.partial(_gla_kernel, ts=ts), grid=(B, ns),
        in_specs=[row(GLA_KEY_WIDTH), row(GLA_KEY_WIDTH), row(GLA_WIDTH), row(GLA_WIDTH), row(LANES),
                  full(wg_pad), full(bg), full(gnorm), full(tri)],
        out_specs=row(GLA_WIDTH),
        out_shape=jax.ShapeDtypeStruct((T, GLA_WIDTH), BF16),
        scratch_shapes=[pltpu.VMEM((GLA_HEADS, GLA_VALUE_DIM, GLA_KEY_DIM), F32),
                        pltpu.VMEM((ts, GLA_KEY_WIDTH), F32),
                        pltpu.VMEM((ts, GLA_WIDTH), F32)],
        compiler_params=_cparams("parallel", "arbitrary"), name="gla")(
            gq, gk, gv, gr, small, wg_pad, bg, gnorm, tri)


def _oproj_kernel(yf_ref, yg_ref, h_ref, wo_ref, g_ref, b_ref, wr_ref, h1_ref, h1b_ref, st_ref):
    mix = _dot(yf_ref[...], wo_ref[0:FOX_WIDTH, :]) + _dot(yg_ref[...], wo_ref[FOX_WIDTH:, :])
    h1 = _layer_norm(DEEPNORM_ALPHA * h_ref[...] + mix, g_ref[...], b_ref[...])
    h1_ref[...] = h1
    hb = h1.astype(BF16)
    h1b_ref[...] = hb
    st_ref[...] = _sigmoid(_dot_nt(wr_ref[...], hb))


def _oproj(y_fox, y_gla, h, w_out, g, b, w_router_t):
    T, D = h.shape
    tm = TM_OPROJ
    row = lambda n: pl.BlockSpec((tm, n), lambda i: (i, 0))
    full = lambda a: pl.BlockSpec(a.shape, lambda i: (0,) * a.ndim)
    return pl.pallas_call(
        _oproj_kernel, grid=(T // tm,),
        in_specs=[row(FOX_WIDTH), row(GLA_WIDTH), row(D), full(w_out), full(g), full(b), full(w_router_t)],
        out_specs=[row(D), row(D), pl.BlockSpec((N_EXPERTS, tm), lambda i: (0, i))],
        out_shape=[jax.ShapeDtypeStruct((T, D), F32), jax.ShapeDtypeStruct((T, D), BF16),
                   jax.ShapeDtypeStruct((N_EXPERTS, T), F32)],
        compiler_params=_cparams("parallel"), name="oproj")(y_fox, y_gla, h, w_out, g, b, w_router_t)


def _route_kernel(s_ref, bias_ref, upper_ref, ones_ref, idx_ref, w_ref, rank_ref, cnt_ref, carry_ref):
    E, tm = s_ref.shape

    @pl.when(pl.program_id(0) == 0)
    def _():
        carry_ref[...] = jnp.zeros_like(carry_ref)

    s = s_ref[...]
    biased = s + bias_ref[...]
    neg = -jnp.inf
    erow = lax.broadcasted_iota(jnp.int32, (E, tm), 0).astype(F32)
    grow = lax.broadcasted_iota(jnp.int32, (GROUP_SIZE, tm), 0).astype(F32)

    gs = []
    for g in range(N_GROUPS):
        blk = biased[g * GROUP_SIZE:(g + 1) * GROUP_SIZE, :]
        m1 = jnp.max(blk, axis=0, keepdims=True)
        i1 = jnp.min(jnp.where(blk == m1, grow, float(GROUP_SIZE)), axis=0, keepdims=True)
        m2 = jnp.max(jnp.where(grow == i1, neg, blk), axis=0, keepdims=True)
        gs.append(m1 + m2)
    keep = []
    for g in range(N_GROUPS):
        beaten = jnp.zeros((1, tm), F32)
        for o in range(N_GROUPS):
            if o == g:
                continue
            wins = (gs[o] > gs[g]) | ((gs[o] == gs[g]) & (o < g))
            beaten = beaten + jnp.where(wins, 1.0, 0.0)
        keep.append(jnp.broadcast_to(beaten < float(TOPK_GROUPS), (GROUP_SIZE, tm)))
    cur = jnp.where(jnp.concatenate(keep, axis=0), biased, neg)

    ids, ws, hots = [], [], []
    chosen = jnp.zeros((E, tm), F32)
    for _ in range(TOP_K):
        m = jnp.max(cur, axis=0, keepdims=True)
        ik = jnp.min(jnp.where(cur == m, erow, float(E)), axis=0, keepdims=True)
        hot = erow == ik
        ws.append(jnp.sum(jnp.where(hot, s, 0.0), axis=0, keepdims=True))
        cur = jnp.where(hot, neg, cur)
        chosen = jnp.where(hot, 1.0, chosen)
        ids.append(ik)
        hots.append(hot)
    wsum = ws[0]
    for w in ws[1:]:
        wsum = wsum + w

    chosen_b = chosen.astype(BF16)
    before = carry_ref[...] + _dot(chosen_b, upper_ref[...])
    carry_ref[...] = carry_ref[...] + _dot(chosen_b, ones_ref[...])
    for k in range(TOP_K):
        idx_ref[k:k + 1, :] = ids[k].astype(jnp.int32)
        w_ref[k:k + 1, :] = ws[k] / wsum * ROUTED_SCALE
        rank_ref[k:k + 1, :] = jnp.sum(jnp.where(hots[k], before, 0.0), axis=0, keepdims=True).astype(jnp.int32)
    cnt_ref[...] = carry_ref[:, 0:LANES]


def _route(scores_t, bias_col, upper, ones):
    E, T = scores_t.shape
    tm = TM_ROUTE
    full = lambda a: pl.BlockSpec(a.shape, lambda i: (0,) * a.ndim)
    kt = pl.BlockSpec((TOP_K, tm), lambda i: (0, i))
    return pl.pallas_call(
        _route_kernel, grid=(T // tm,),
        in_specs=[pl.BlockSpec((E, tm), lambda i: (0, i)), full(bias_col), full(upper), full(ones)],
        out_specs=[kt, kt, kt, pl.BlockSpec((E, LANES), lambda i: (0, 0))],
        out_shape=[jax.ShapeDtypeStruct((TOP_K, T), jnp.int32), jax.ShapeDtypeStruct((TOP_K, T), F32),
                   jax.ShapeDtypeStruct((TOP_K, T), jnp.int32), jax.ShapeDtypeStruct((E, LANES), F32)],
        scratch_shapes=[pltpu.VMEM((E, tm), F32)],
        compiler_params=_cparams("arbitrary"), name="route")(scores_t, bias_col, upper, ones)


def _expert_kernel(blk_ref, nb_ref, x_ref, wg_ref, wu_ref, wd_ref, y_ref):
    @pl.when(pl.program_id(0) < nb_ref[0])
    def _():
        x = x_ref[...]
        g = _dot(x, wg_ref[...].astype(BF16))
        u = _dot(x, wu_ref[...].astype(BF16))
        a = (g * _sigmoid(g) * u).astype(BF16)
        y_ref[...] = _dot(a, wd_ref[...].astype(BF16)).astype(y_ref.dtype)


def _experts(blk_exp, nb_used, xs, w_gate, w_up, w_down):
    P, D = xs.shape
    bm = BM_EXPERT
    nb = P // bm
    grid_spec = pltpu.PrefetchScalarGridSpec(
        num_scalar_prefetch=2, grid=(nb,),
        in_specs=[pl.BlockSpec((bm, D), lambda i, be, nu: (i, 0)),
                  pl.BlockSpec((None, D, EXPERT_DIM), lambda i, be, nu: (be[i], 0, 0)),
                  pl.BlockSpec((None, D, EXPERT_DIM), lambda i, be, nu: (be[i], 0, 0)),
                  pl.BlockSpec((None, EXPERT_DIM, D), lambda i, be, nu: (be[i], 0, 0))],
        out_specs=pl.BlockSpec((bm, D), lambda i, be, nu: (i, 0)))
    return pl.pallas_call(
        _expert_kernel, grid_spec=grid_spec,
        out_shape=jax.ShapeDtypeStruct((P, D), BF16),
        compiler_params=_cparams("arbitrary"), name="experts")(blk_exp, nb_used, xs, w_gate, w_up, w_down)


def _final_kernel(h1_ref, yk_ref, wk_ref, wg_ref, wu_ref, wd_ref, g_ref, b_ref, o_ref):
    h1 = h1_ref[...]
    hb = h1.astype(BF16)
    g = _dot(hb, wg_ref[...])
    u = _dot(hb, wu_ref[...])
    ffn = _dot((g * _sigmoid(g) * u).astype(BF16), wd_ref[...])
    wk = wk_ref[...]
    for k in range(TOP_K):
        ffn = ffn + yk_ref[k].astype(F32) * wk[:, k:k + 1]
    o_ref[...] = _layer_norm(DEEPNORM_ALPHA * h1 + ffn, g_ref[...], b_ref[...])


def _final(h1, yk, wk, w_sg, w_su, w_sd, g, b):
    T, D = h1.shape
    tm = TM_FINAL
    row = lambda n: pl.BlockSpec((tm, n), lambda i: (i, 0))
    full = lambda a: pl.BlockSpec(a.shape, lambda i: (0,) * a.ndim)
    return pl.pallas_call(
        _final_kernel, grid=(T // tm,),
        in_specs=[row(D), pl.BlockSpec((TOP_K, tm, D), lambda i: (0, i, 0)), row(TOP_K),
                  full(w_sg), full(w_su), full(w_sd), full(g), full(b)],
        out_specs=row(D), out_shape=jax.ShapeDtypeStruct((T, D), F32),
        compiler_params=_cparams("parallel"), name="final")(h1, yk, wk, w_sg, w_su, w_sd, g, b)


def _block_diag_tri(n, c):
    r = np.arange(n)
    return jnp.asarray(((r[:, None] >= r[None, :]) & (r[:, None] // c == r[None, :] // c)).astype(np.float32), BF16)


def kernel(x, ln_in_g, ln_in_b, w_in, b_fgate, w_gate_up, b_gate, g_gla_norm, w_out, ln1_g, ln1_b, w_router,
           router_bias, w_exp_gate, w_exp_up, w_exp_down, w_sh_gate, w_sh_up, w_sh_down, ln2_g, ln2_b):
    B, S, D = x.shape
    T = B * S
    x2 = x.reshape(T, D)
    l = 0
    row = lambda a: a.reshape(1, -1)

    off = np.cumsum((0,) + IN_SPLITS)
    seg = lambda i: w_in[l][:, off[i]:off[i + 1]]
    w_main = jnp.concatenate([seg(0) * FOX_HEAD_DIM ** -0.5, seg(1), seg(2), seg(4) * GLA_KEY_DIM ** -0.5, seg(5),
                              seg(6), seg(8)], axis=1).astype(BF16)
    n_small = FOX_HEADS + GLA_GATE_RANK
    w_small = jnp.concatenate([seg(3), seg(7), jnp.zeros((D, LANES - n_small), F32)], axis=1).astype(BF16)
    bf_pad = jnp.concatenate([b_fgate[l], jnp.zeros((LANES - FOX_HEADS,), F32)]).reshape(1, LANES)
    wg_pad = jnp.zeros((LANES, GLA_KEY_WIDTH), F32).at[FOX_HEADS:n_small].set(w_gate_up[l]).astype(BF16)

    h, fq, fk, fv, gq, gk, gv, gr, small = _proj(x2, row(ln_in_g), row(ln_in_b), w_main, w_small)

    c, c_t = _fgate(small, bf_pad, B, S)
    nq = S // TQ_FOX
    c_row = c_t.reshape(B, FOX_HEADS, nq, TQ_FOX).transpose(0, 2, 1, 3)
    y_fox = _fox(fq, fk, fv, c, c_row, B, S)

    y_gla = _gla(gq, gk, gv, gr, small, wg_pad, row(b_gate[l]), row(g_gla_norm[l]),
                 _block_diag_tri(TS_GLA, CHUNK), B, S)

    h1, h1b, scores_t = _oproj(y_fox, y_gla, h, w_out[l].astype(BF16), row(ln1_g[l]), row(ln1_b[l]),
                               w_router[l].T.astype(BF16))

    tm = TM_ROUTE
    r = np.arange(tm)
    upper = jnp.asarray((r[:, None] < r[None, :]).astype(np.float32), BF16)
    ones = jnp.ones((tm, tm), BF16)
    idx_t, w_t, rank_t, cnt = _route(scores_t, router_bias[l].reshape(N_EXPERTS, 1), upper, ones)

    bm = BM_EXPERT
    A = T * TOP_K
    nb = A // bm + N_EXPERTS
    P = nb * bm
    counts = cnt[:, 0].astype(jnp.int32)
    padded = (counts + bm - 1) // bm * bm
    pend = jnp.cumsum(padded)
    pstart = pend - padded
    pos = pstart[idx_t] + rank_t
    nb_used = (pend[-1] // bm).astype(jnp.int32)
    blk = jnp.minimum(jnp.searchsorted(pend, jnp.arange(nb, dtype=jnp.int32) * bm, side='right'),
                      N_EXPERTS - 1).astype(jnp.int32)
    blk = jnp.where(jnp.arange(nb) < nb_used, blk, blk[jnp.maximum(nb_used - 1, 0)])
    tok = jnp.broadcast_to(jnp.arange(T, dtype=jnp.int32)[None, :], (TOP_K, T))
    row_tok = jnp.zeros((P,), jnp.int32).at[pos.reshape(-1)].set(tok.reshape(-1))
    xs = jnp.take(h1b, row_tok, axis=0)

    ys = _experts(blk, nb_used.reshape(1), xs, w_exp_gate[l], w_exp_up[l], w_exp_down[l])
    yk = jnp.take(ys, pos.reshape(-1), axis=0).reshape(TOP_K, T, D)

    out = _final(h1, yk, w_t.T, w_sh_gate[l].astype(BF16), w_sh_up[l].astype(BF16), w_sh_down[l].astype(BF16),
                 row(ln2_g[l]), row(ln2_b[l]))
    return out.reshape(B, S, D)
```

```python
import functools

import jax
import jax.numpy as jnp
import numpy as np
from jax import lax
from jax.experimental import pallas as pl
from jax.experimental.pallas import tpu as pltpu
from jax.experimental.pallas import tpu_sc as plsc

F32 = jnp.float32
BF16 = jnp.bfloat16

D_MODEL = 1024
CHUNK = 64
LN_EPS = 1e-5
FOX_HEADS = 8
FOX_HEAD_DIM = 64
FOX_WIDTH = FOX_HEADS * FOX_HEAD_DIM
GLA_HEADS = 4
GLA_KEY_DIM = 64
GLA_VALUE_DIM = 128
GLA_KEY_WIDTH = GLA_HEADS * GLA_KEY_DIM
GLA_WIDTH = GLA_HEADS * GLA_VALUE_DIM
GLA_GATE_RANK = 16
GLA_GATE_TEMP = 16.0
N_EXPERTS = 256
N_GROUPS = 8
GROUP_SIZE = N_EXPERTS // N_GROUPS
TOPK_GROUPS = 4
TOP_K = 8
EXPERT_DIM = 256
SHARED_DIM = 256
ROUTED_SCALE = 2.5
DEPTH = 1
DEEPNORM_ALPHA = (2.0 * DEPTH) ** 0.25
IN_SPLITS = (FOX_WIDTH, FOX_WIDTH, FOX_WIDTH, FOX_HEADS, GLA_KEY_WIDTH, GLA_KEY_WIDTH, GLA_WIDTH,
             GLA_GATE_RANK, GLA_WIDTH)

LANES = 128
SUB_CHUNK = 16
VMEM_LIMIT = 48 * 1024 * 1024

TM_PROJ = 512
TQ_FOX = 512
TS_GLA = 512
TM_OPROJ = 512
TM_ROUTE = 512
BM_EXPERT = 256
TM_FINAL = 512

SC_CORES = 2
SC_SUBCORES = 16
SC_ROW_WINDOW = 128
SC_ROW_SPLIT = 2


def _cparams(*sem):
    return pltpu.CompilerParams(dimension_semantics=sem, vmem_limit_bytes=VMEM_LIMIT)


def _layer_norm(x, g, b):
    mu = jnp.mean(x, axis=-1, keepdims=True)
    xc = x - mu
    var = jnp.mean(xc * xc, axis=-1, keepdims=True)
    return xc * lax.rsqrt(var + LN_EPS) * g + b


def _log_sigmoid(z):
    return jnp.minimum(z, 0.0) - jnp.log1p(jnp.exp(-jnp.abs(z)))


def _sigmoid(z):
    return 1.0 / (1.0 + jnp.exp(-z))


def _dot(a, b):
    return jnp.dot(a, b, preferred_element_type=F32)


def _dot_nt(a, b):
    return lax.dot_general(a, b, (((1,), (1,)), ((), ())), preferred_element_type=F32)


def _dot_tn(a, b):
    return lax.dot_general(a, b, (((0,), (0,)), ((), ())), preferred_element_type=F32)


def _pack_bf16_pairs(x):
    n = x.shape[1] // 2
    u = lax.bitcast_convert_type(x.astype(BF16).astype(F32), jnp.uint32)
    return (u[:, :n] >> 16) | u[:, n:]


def _unpack_bf16_pairs(w):
    lo = lax.bitcast_convert_type(w << 16, F32)
    hi = lax.bitcast_convert_type(w & jnp.uint32(0xFFFF0000), F32)
    return lo, hi


def _proj_kernel(x_ref, g_ref, b_ref, wm_ref, ws_ref,
                 h_ref, fq_ref, fk_ref, fv_ref, gq_ref, gk_ref, gv_ref, gr_ref, sm_ref):
    h = _layer_norm(x_ref[...], g_ref[...], b_ref[...])
    h_ref[...] = h
    hb = h.astype(BF16)
    off = 0
    for ref in (fq_ref, fk_ref, fv_ref, gq_ref, gk_ref, gv_ref, gr_ref):
        n = ref.shape[1]
        ref[...] = _dot(hb, wm_ref[:, off:off + n]).astype(ref.dtype)
        off += n
    sm_ref[...] = _dot(hb, ws_ref[...])


def _proj(x2, g, b, w_main, w_small):
    T, D = x2.shape
    tm = TM_PROJ
    widths = (FOX_WIDTH, FOX_WIDTH, FOX_WIDTH, GLA_KEY_WIDTH, GLA_KEY_WIDTH, GLA_WIDTH, GLA_WIDTH)
    row = lambda n: pl.BlockSpec((tm, n), lambda i: (i, 0))
    full = lambda a: pl.BlockSpec(a.shape, lambda i: (0,) * a.ndim)
    out_shape = [jax.ShapeDtypeStruct((T, D), F32)]
    out_shape += [jax.ShapeDtypeStruct((T, n), BF16) for n in widths]
    out_shape += [jax.ShapeDtypeStruct((T, LANES), F32)]
    out_specs = [row(D)] + [row(n) for n in widths] + [row(LANES)]
    return pl.pallas_call(
        _proj_kernel, grid=(T // tm,),
        in_specs=[row(D), full(g), full(b), full(w_main), full(w_small)],
        out_specs=out_specs, out_shape=out_shape,
        compiler_params=_cparams("parallel"), name="proj")(x2, g, b, w_main, w_small)


def _fgate_kernel(sm_ref, bf_ref, c_ref, ct_ref):
    S = sm_ref.shape[0]
    lf = _log_sigmoid(sm_ref[...] + bf_ref[...])
    rows = lax.broadcasted_iota(jnp.int32, lf.shape, 0)
    s = 1
    while s < S:
        lf = lf + jnp.where(rows >= s, pltpu.roll(lf, s, axis=0), 0.0)
        s *= 2
    c_ref[...] = lf
    ct_ref[0] = lf.T[0:FOX_HEADS, :]


def _fgate(small, bf_pad, B, S):
    T = small.shape[0]
    return pl.pallas_call(
        _fgate_kernel, grid=(B,),
        in_specs=[pl.BlockSpec((S, LANES), lambda b: (b, 0)),
                  pl.BlockSpec((1, LANES), lambda b: (0, 0))],
        out_specs=[pl.BlockSpec((S, LANES), lambda b: (b, 0)),
                   pl.BlockSpec((1, FOX_HEADS, S), lambda b: (b, 0, 0))],
        out_shape=[jax.ShapeDtypeStruct((T, LANES), F32),
                   jax.ShapeDtypeStruct((B, FOX_HEADS, S), F32)],
        compiler_params=_cparams("parallel"), name="fgate")(small, bf_pad)


def _fox_kernel(q_ref, k_ref, v_ref, c_ref, cr_ref, o_ref, *, tq):
    i = pl.program_id(1)
    dh = FOX_HEAD_DIM
    row = lax.broadcasted_iota(jnp.int32, (tq, tq), 0)
    col = lax.broadcasted_iota(jnp.int32, (tq, tq), 1)
    causal = row >= col
    for h in range(FOX_HEADS):
        lanes = slice(h * dh, (h + 1) * dh)
        q = q_ref[:, lanes]
        ct = c_ref[:, h:h + 1]

        def step(j, carry, masked):
            m, l, acc = carry
            r0 = pl.multiple_of(j * tq, tq)
            k = k_ref[pl.ds(r0, tq), lanes]
            v = v_ref[pl.ds(r0, tq), lanes]
            cs = cr_ref[0, j, h:h + 1, :]
            s = _dot_nt(q, k) + (ct - cs)
            if masked:
                s = jnp.where(causal, s, -jnp.inf)
            m_new = jnp.maximum(m, jnp.max(s, axis=-1, keepdims=True))
            p = jnp.exp(s - m_new)
            alpha = jnp.exp(m - m_new)
            l = alpha * l + jnp.sum(p, axis=-1, keepdims=True)
            acc = alpha * acc + _dot(p.astype(BF16), v)
            return m_new, l, acc

        init = (jnp.full((tq, 1), -jnp.inf, F32), jnp.zeros((tq, 1), F32), jnp.zeros((tq, dh), F32))
        carry = lax.fori_loop(0, i, functools.partial(step, masked=False), init)
        _, l, acc = step(i, carry, True)
        o_ref[:, lanes] = (acc / l).astype(o_ref.dtype)


def _fox(fq, fk, fv, c, c_row, B, S):
    T = fq.shape[0]
    tq = TQ_FOX
    nq = S // tq
    return pl.pallas_call(
        functools.partial(_fox_kernel, tq=tq), grid=(B, nq),
        in_specs=[pl.BlockSpec((tq, FOX_WIDTH), lambda b, i: (b * nq + i, 0)),
                  pl.BlockSpec((S, FOX_WIDTH), lambda b, i: (b, 0)),
                  pl.BlockSpec((S, FOX_WIDTH), lambda b, i: (b, 0)),
                  pl.BlockSpec((tq, LANES), lambda b, i: (b * nq + i, 0)),
                  pl.BlockSpec((1, nq, FOX_HEADS, tq), lambda b, i: (b, 0, 0, 0))],
        out_specs=pl.BlockSpec((tq, FOX_WIDTH), lambda b, i: (b * nq + i, 0)),
        out_shape=jax.ShapeDtypeStruct((T, FOX_WIDTH), BF16),
        compiler_params=_cparams("parallel", "arbitrary"), name="fox")(fq, fk, fv, c, c_row)


def _gla_kernel(q_ref, k_ref, v_ref, r_ref, sm_ref, wg_ref, bg_ref, gn_ref, tri_ref, y_ref,
                st_ref, b_ref, o_ref, *, ts):
    dk, dv, C, SC = GLA_KEY_DIM, GLA_VALUE_DIM, CHUNK, SUB_CHUNK

    @pl.when(pl.program_id(1) == 0)
    def _():
        st_ref[...] = jnp.zeros_like(st_ref)

    z = _dot(sm_ref[...].astype(BF16), wg_ref[...]) + bg_ref[...]
    la = _log_sigmoid(z) * (1.0 / GLA_GATE_TEMP)
    hi = la.astype(BF16)
    r1 = la - hi.astype(F32)
    mid = r1.astype(BF16)
    lo = (r1 - mid.astype(F32)).astype(BF16)
    tri = tri_ref[...]
    b_ref[...] = _dot(tri, hi) + _dot(tri, mid) + _dot(tri, lo)

    row = lax.broadcasted_iota(jnp.int32, (C, C), 0)
    col = lax.broadcasted_iota(jnp.int32, (C, C), 1)
    row_in_sub = row % SC
    sub_base = (row // SC) * SC
    col16 = lax.broadcasted_iota(jnp.int32, (SC, C), 1)

    def chunk(c, carry):
        r0 = pl.multiple_of(c * C, C)
        for h in range(GLA_HEADS):
            kl = slice(h * dk, (h + 1) * dk)
            vl = slice(h * dv, (h + 1) * dv)
            q = q_ref[pl.ds(r0, C), kl].astype(F32)
            k = k_ref[pl.ds(r0, C), kl].astype(F32)
            v = v_ref[pl.ds(r0, C), vl]
            b = b_ref[pl.ds(r0, C), kl]
            st = st_ref[h]
            b_last = b[C - 1:C, :]

            o = _dot_nt((q * jnp.exp(b)).astype(BF16), st.astype(BF16))

            pieces = [jnp.zeros((SC, C), F32)]
            for i in range(1, C // SC):
                ref_row = b[SC * i - 1:SC * i, :]
                qt = (q[SC * i:SC * (i + 1), :] * jnp.exp(b[SC * i:SC * (i + 1), :] - ref_row)).astype(BF16)
                kt = (k * jnp.exp(jnp.minimum(ref_row - b, 0.0))).astype(BF16)
                pieces.append(jnp.where(col16 < SC * i, _dot_nt(qt, kt), 0.0))
            scores = jnp.concatenate(pieces, axis=0)

            for j in range(SC):
                bj = jnp.concatenate(
                    [jnp.broadcast_to(b[SC * i + j:SC * i + j + 1, :], (SC, dk)) for i in range(C // SC)], axis=0)
                kj = jnp.concatenate(
                    [jnp.broadcast_to(k[SC * i + j:SC * i + j + 1, :], (SC, dk)) for i in range(C // SC)], axis=0)
                decay = jnp.exp(jnp.where(row_in_sub >= j, b - bj, -jnp.inf))
                colv = jnp.sum(q * kj * decay, axis=-1, keepdims=True)
                scores = jnp.where(col == sub_base + j, colv, scores)

            o = o + _dot(scores.astype(BF16), v)
            o_ref[pl.ds(r0, C), vl] = o
            kh = (k * jnp.exp(b_last - b)).astype(BF16)
            st_ref[h] = st * jnp.exp(b_last) + _dot_tn(v, kh)
        return carry

    lax.fori_loop(0, ts // C, chunk, 0)

    for h in range(GLA_HEADS):
        vl = slice(h * dv, (h + 1) * dv)
        o = o_ref[:, vl]
        ms = jnp.mean(o * o, axis=-1, keepdims=True)
        g = r_ref[:, vl].astype(F32)
        y = o * lax.rsqrt(ms + LN_EPS) * gn_ref[...] * (g * _sigmoid(g))
        y_ref[:, vl] = y.astype(y_ref.dtype)


def _gla(gq, gk, gv, gr, small, wg_pad, bg, gnorm, tri, B, S):
    T = gq.shape[0]
    ts = TS_GLA
    ns = S // ts
    row = lambda n: pl.BlockSpec((ts, n), lambda b, i: (b * ns + i, 0))
    full = lambda a: pl.BlockSpec(a.shape, lambda b, i: (0,) * a.ndim)
    return pl.pallas_call(
        functools.partial(_gla_kernel, ts=ts), grid=(B, ns),
        in_specs=[row(GLA_KEY_WIDTH), row(GLA_KEY_WIDTH), row(GLA_WIDTH), row(GLA_WIDTH), row(LANES),
                  full(wg_pad), full(bg), full(gnorm), full(tri)],
        out_specs=row(GLA_WIDTH),
        out_shape=jax.ShapeDtypeStruct((T, GLA_WIDTH), BF16),
        scratch_shapes=[pltpu.VMEM((GLA_HEADS, GLA_VALUE_DIM, GLA_KEY_DIM), F32),
                        pltpu.VMEM((ts, GLA_KEY_WIDTH), F32),
                        pltpu.VMEM((ts, GLA_WIDTH), F32)],
        compiler_params=_cparams("parallel", "arbitrary"), name="gla")(
            gq, gk, gv, gr, small, wg_pad, bg, gnorm, tri)


def _oproj_kernel(yf_ref, yg_ref, h_ref, wo_ref, g_ref, b_ref, wr_ref, h1_ref, h1p_ref, st_ref):
    mix = _dot(yf_ref[...], wo_ref[0:FOX_WIDTH, :]) + _dot(yg_ref[...], wo_ref[FOX_WIDTH:, :])
    h1 = _layer_norm(DEEPNORM_ALPHA * h_ref[...] + mix, g_ref[...], b_ref[...])
    h1_ref[...] = h1
    h1p_ref[...] = _pack_bf16_pairs(h1)
    st_ref[...] = _sigmoid(_dot_nt(wr_ref[...], h1.astype(BF16)))


def _oproj(y_fox, y_gla, h, w_out, g, b, w_router_t):
    T, D = h.shape
    tm = TM_OPROJ
    row = lambda n: pl.BlockSpec((tm, n), lambda i: (i, 0))
    full = lambda a: pl.BlockSpec(a.shape, lambda i: (0,) * a.ndim)
    return pl.pallas_call(
        _oproj_kernel, grid=(T // tm,),
        in_specs=[row(FOX_WIDTH), row(GLA_WIDTH), row(D), full(w_out), full(g), full(b), full(w_router_t)],
        out_specs=[row(D), row(D // 2), pl.BlockSpec((N_EXPERTS, tm), lambda i: (0, i))],
        out_shape=[jax.ShapeDtypeStruct((T, D), F32), jax.ShapeDtypeStruct((T, D // 2), jnp.uint32),
                   jax.ShapeDtypeStruct((N_EXPERTS, T), F32)],
        compiler_params=_cparams("parallel"), name="oproj")(y_fox, y_gla, h, w_out, g, b, w_router_t)


def _route_kernel(s_ref, bias_ref, upper_ref, ones_ref, idx_ref, w_ref, rank_ref, cnt_ref, carry_ref):
    E, tm = s_ref.shape

    @pl.when(pl.program_id(0) == 0)
    def _():
        carry_ref[...] = jnp.zeros_like(carry_ref)

    s = s_ref[...]
    biased = s + bias_ref[...]
    neg = -jnp.inf
    erow = lax.broadcasted_iota(jnp.int32, (E, tm), 0).astype(F32)
    grow = lax.broadcasted_iota(jnp.int32, (GROUP_SIZE, tm), 0).astype(F32)

    gs = []
    for g in range(N_GROUPS):
        blk = biased[g * GROUP_SIZE:(g + 1) * GROUP_SIZE, :]
        m1 = jnp.max(blk, axis=0, keepdims=True)
        i1 = jnp.min(jnp.where(blk == m1, grow, float(GROUP_SIZE)), axis=0, keepdims=True)
        m2 = jnp.max(jnp.where(grow == i1, neg, blk), axis=0, keepdims=True)
        gs.append(m1 + m2)
    keep = []
    for g in range(N_GROUPS):
        beaten = jnp.zeros((1, tm), F32)
        for o in range(N_GROUPS):
            if o == g:
                continue
            wins = (gs[o] > gs[g]) | ((gs[o] == gs[g]) & (o < g))
            beaten = beaten + jnp.where(wins, 1.0, 0.0)
        keep.append(jnp.broadcast_to(beaten < float(TOPK_GROUPS), (GROUP_SIZE, tm)))
    cur = jnp.where(jnp.concatenate(keep, axis=0), biased, neg)

    ids, ws, hots = [], [], []
    chosen = jnp.zeros((E, tm), F32)
    for _ in range(TOP_K):
        m = jnp.max(cur, axis=0, keepdims=True)
        ik = jnp.min(jnp.where(cur == m, erow, float(E)), axis=0, keepdims=True)
        hot = erow == ik
        ws.append(jnp.sum(jnp.where(hot, s, 0.0), axis=0, keepdims=True))
        cur = jnp.where(hot, neg, cur)
        chosen = jnp.where(hot, 1.0, chosen)
        ids.append(ik)
        hots.append(hot)
    wsum = ws[0]
    for w in ws[1:]:
        wsum = wsum + w

    chosen_b = chosen.astype(BF16)
    before = carry_ref[...] + _dot(chosen_b, upper_ref[...])
    carry_ref[...] = carry_ref[...] + _dot(chosen_b, ones_ref[...])
    for k in range(TOP_K):
        idx_ref[k:k + 1, :] = ids[k].astype(jnp.int32)
        w_ref[k:k + 1, :] = ws[k] / wsum * ROUTED_SCALE
        rank_ref[k:k + 1, :] = jnp.sum(jnp.where(hots[k], before, 0.0), axis=0, keepdims=True).astype(jnp.int32)
    cnt_ref[...] = carry_ref[:, 0:LANES]


def _route(scores_t, bias_col, upper, ones):
    E, T = scores_t.shape
    tm = TM_ROUTE
    full = lambda a: pl.BlockSpec(a.shape, lambda i: (0,) * a.ndim)
    kt = pl.BlockSpec((TOP_K, tm), lambda i: (0, i))
    return pl.pallas_call(
        _route_kernel, grid=(T // tm,),
        in_specs=[pl.BlockSpec((E, tm), lambda i: (0, i)), full(bias_col), full(upper), full(ones)],
        out_specs=[kt, kt, kt, pl.BlockSpec((E, LANES), lambda i: (0, 0))],
        out_shape=[jax.ShapeDtypeStruct((TOP_K, T), jnp.int32), jax.ShapeDtypeStruct((TOP_K, T), F32),
                   jax.ShapeDtypeStruct((TOP_K, T), jnp.int32), jax.ShapeDtypeStruct((E, LANES), F32)],
        scratch_shapes=[pltpu.VMEM((E, tm), F32)],
        compiler_params=_cparams("arbitrary"), name="route")(scores_t, bias_col, upper, ones)


def _expert_kernel(blk_ref, nb_ref, x_ref, wg_ref, wu_ref, wd_ref, y_ref):
    half = x_ref.shape[1]

    @pl.when(pl.program_id(0) < nb_ref[0])
    def _():
        lo, hi = _unpack_bf16_pairs(x_ref[...])
        lo, hi = lo.astype(BF16), hi.astype(BF16)
        g = _dot(lo, wg_ref[0:half, :].astype(BF16)) + _dot(hi, wg_ref[half:, :].astype(BF16))
        u = _dot(lo, wu_ref[0:half, :].astype(BF16)) + _dot(hi, wu_ref[half:, :].astype(BF16))
        a = (g * _sigmoid(g) * u).astype(BF16)
        y_ref[...] = _pack_bf16_pairs(_dot(a, wd_ref[...].astype(BF16)))


def _experts(blk_exp, nb_used, xs, w_gate, w_up, w_down):
    P, half = xs.shape
    D = 2 * half
    bm = BM_EXPERT
    nb = P // bm
    grid_spec = pltpu.PrefetchScalarGridSpec(
        num_scalar_prefetch=2, grid=(nb,),
        in_specs=[pl.BlockSpec((bm, half), lambda i, be, nu: (i, 0)),
                  pl.BlockSpec((None, D, EXPERT_DIM), lambda i, be, nu: (be[i], 0, 0)),
                  pl.BlockSpec((None, D, EXPERT_DIM), lambda i, be, nu: (be[i], 0, 0)),
                  pl.BlockSpec((None, EXPERT_DIM, D), lambda i, be, nu: (be[i], 0, 0))],
        out_specs=pl.BlockSpec((bm, half), lambda i, be, nu: (i, 0)))
    return pl.pallas_call(
        _expert_kernel, grid_spec=grid_spec,
        out_shape=jax.ShapeDtypeStruct((P, half), jnp.uint32),
        compiler_params=_cparams("arbitrary"), name="experts")(blk_exp, nb_used, xs, w_gate, w_up, w_down)


def _sc_mesh():
    return plsc.VectorSubcoreMesh(core_axis_name="core", subcore_axis_name="subcore",
                                  num_cores=SC_CORES, num_subcores=SC_SUBCORES)


def _sc_scatter_rows(rows, pos, n_out):
    T, W = rows.shape
    K = pos.shape[0]
    win = SC_ROW_WINDOW

    @functools.partial(pl.kernel, out_type=jax.ShapeDtypeStruct((n_out, W), rows.dtype), mesh=_sc_mesh(),
                       name="sc_dispatch")
    def k(x_hbm, p_hbm, o_hbm):
        def body(x_vmem, p_vmem):
            for j in range(K):
                pltpu.sync_copy(x_vmem, o_hbm.at[p_vmem.at[j]])

        pltpu.emit_pipeline(
            body, grid=(T // win,),
            in_specs=[pl.BlockSpec((win, W), lambda i: (i, 0)), pl.BlockSpec((K, win), lambda i: (0, i))],
            out_specs=[], core_axis_name=("core", "subcore"), dimension_semantics=(pltpu.PARALLEL,))(x_hbm, p_hbm)

    return k(rows, pos)


def _sc_gather_rows(table, idx):
    M = idx.shape[0]
    W = table.shape[1]
    win = SC_ROW_WINDOW

    @functools.partial(pl.kernel, out_type=jax.ShapeDtypeStruct((M, W), table.dtype), mesh=_sc_mesh(),
                       name="sc_combine")
    def k(t_hbm, i_hbm, o_hbm):
        def body(i_vmem, o_vmem):
            pltpu.sync_copy(t_hbm.at[i_vmem.at[0]], o_vmem)

        pltpu.emit_pipeline(
            body, grid=(M // win,),
            in_specs=[pl.BlockSpec((1, win), lambda i: (0, i))],
            out_specs=[pl.BlockSpec((win, W), lambda i: (i, 0))],
            core_axis_name=("core", "subcore"), dimension_semantics=(pltpu.PARALLEL,))(i_hbm, o_hbm)

    return k(table, idx.reshape(1, M))


def _final_kernel(h1_ref, yk_ref, wk_ref, wg_ref, wu_ref, wd_ref, g_ref, b_ref, o_ref):
    h1 = h1_ref[...]
    hb = h1.astype(BF16)
    g = _dot(hb, wg_ref[...])
    u = _dot(hb, wu_ref[...])
    ffn = _dot((g * _sigmoid(g) * u).astype(BF16), wd_ref[...])
    wk = wk_ref[...]
    half = yk_ref.shape[2]
    lo_sum = jnp.zeros((h1.shape[0], half), F32)
    hi_sum = jnp.zeros((h1.shape[0], half), F32)
    for k in range(TOP_K):
        lo, hi = _unpack_bf16_pairs(yk_ref[k])
        lo_sum = lo_sum + lo * wk[:, k:k + 1]
        hi_sum = hi_sum + hi * wk[:, k:k + 1]
    ffn = ffn + jnp.concatenate([lo_sum, hi_sum], axis=1)
    o_ref[...] = _layer_norm(DEEPNORM_ALPHA * h1 + ffn, g_ref[...], b_ref[...])


def _final(h1, yk, wk, w_sg, w_su, w_sd, g, b):
    T, D = h1.shape
    tm = TM_FINAL
    row = lambda n: pl.BlockSpec((tm, n), lambda i: (i, 0))
    full = lambda a: pl.BlockSpec(a.shape, lambda i: (0,) * a.ndim)
    return pl.pallas_call(
        _final_kernel, grid=(T // tm,),
        in_specs=[row(D), pl.BlockSpec((TOP_K, tm, D // 2), lambda i: (0, i, 0)), row(TOP_K),
                  full(w_sg), full(w_su), full(w_sd), full(g), full(b)],
        out_specs=row(D), out_shape=jax.ShapeDtypeStruct((T, D), F32),
        compiler_params=_cparams("parallel"), name="final")(h1, yk, wk, w_sg, w_su, w_sd, g, b)


def _block_diag_tri(n, c):
    r = np.arange(n)
    return jnp.asarray(((r[:, None] >= r[None, :]) & (r[:, None] // c == r[None, :] // c)).astype(np.float32), BF16)


def kernel(x, ln_in_g, ln_in_b, w_in, b_fgate, w_gate_up, b_gate, g_gla_norm, w_out, ln1_g, ln1_b, w_router,
           router_bias, w_exp_gate, w_exp_up, w_exp_down, w_sh_gate, w_sh_up, w_sh_down, ln2_g, ln2_b):
    B, S, D = x.shape
    T = B * S
    x2 = x.reshape(T, D)
    l = 0
    row = lambda a: a.reshape(1, -1)

    off = np.cumsum((0,) + IN_SPLITS)
    seg = lambda i: w_in[l][:, off[i]:off[i + 1]]
    w_main = jnp.concatenate([seg(0) * FOX_HEAD_DIM ** -0.5, seg(1), seg(2), seg(4) * GLA_KEY_DIM ** -0.5, seg(5),
                              seg(6), seg(8)], axis=1).astype(BF16)
    n_small = FOX_HEADS + GLA_GATE_RANK
    w_small = jnp.concatenate([seg(3), seg(7), jnp.zeros((D, LANES - n_small), F32)], axis=1).astype(BF16)
    bf_pad = jnp.concatenate([b_fgate[l], jnp.zeros((LANES - FOX_HEADS,), F32)]).reshape(1, LANES)
    wg_pad = jnp.zeros((LANES, GLA_KEY_WIDTH), F32).at[FOX_HEADS:n_small].set(w_gate_up[l]).astype(BF16)

    h, fq, fk, fv, gq, gk, gv, gr, small = _proj(x2, row(ln_in_g), row(ln_in_b), w_main, w_small)

    c, c_t = _fgate(small, bf_pad, B, S)
    nq = S // TQ_FOX
    c_row = c_t.reshape(B, FOX_HEADS, nq, TQ_FOX).transpose(0, 2, 1, 3)
    y_fox = _fox(fq, fk, fv, c, c_row, B, S)

    y_gla = _gla(gq, gk, gv, gr, small, wg_pad, row(b_gate[l]), row(g_gla_norm[l]),
                 _block_diag_tri(TS_GLA, CHUNK), B, S)

    h1, h1p, scores_t = _oproj(y_fox, y_gla, h, w_out[l].astype(BF16), row(ln1_g[l]), row(ln1_b[l]),
                               w_router[l].T.astype(BF16))

    tm = TM_ROUTE
    r = np.arange(tm)
    upper = jnp.asarray((r[:, None] < r[None, :]).astype(np.float32), BF16)
    ones = jnp.ones((tm, tm), BF16)
    idx_t, w_t, rank_t, cnt = _route(scores_t, router_bias[l].reshape(N_EXPERTS, 1), upper, ones)

    bm = BM_EXPERT
    A = T * TOP_K
    nb = A // bm + N_EXPERTS
    P = nb * bm
    counts = cnt[:, 0].astype(jnp.int32)
    padded = (counts + bm - 1) // bm * bm
    pend = jnp.cumsum(padded)
    pstart = pend - padded
    pos = pstart[idx_t] + rank_t
    nb_used = (pend[-1] // bm).astype(jnp.int32)
    blk = jnp.minimum(jnp.searchsorted(pend, jnp.arange(nb, dtype=jnp.int32) * bm, side='right'),
                      N_EXPERTS - 1).astype(jnp.int32)
    blk = jnp.where(jnp.arange(nb) < nb_used, blk, blk[jnp.maximum(nb_used - 1, 0)])

    ns = SC_ROW_SPLIT
    words = D // 2 // ns
    pos_split = (pos[:, :, None] * ns + jnp.arange(ns, dtype=jnp.int32)).reshape(TOP_K, T * ns)
    xs = _sc_scatter_rows(h1p.reshape(T * ns, words), pos_split, P * ns).reshape(P, D // 2)
    ys = _experts(blk, nb_used.reshape(1), xs, w_exp_gate[l], w_exp_up[l], w_exp_down[l])
    yk = _sc_gather_rows(ys.reshape(P * ns, words), pos_split.reshape(-1)).reshape(TOP_K, T, D // 2)

    out = _final(h1, yk, w_t.T, w_sh_gate[l].astype(BF16), w_sh_up[l].astype(BF16), w_sh_down[l].astype(BF16),
                 row(ln2_g[l]), row(ln2_b[l]))
    return out.reshape(B, S, D)
```

```python
import functools

import jax
import jax.numpy as jnp
import numpy as np
from jax import lax
from jax.experimental import pallas as pl
from jax.experimental.pallas import tpu as pltpu
from jax.experimental.pallas import tpu_sc as plsc

F32 = jnp.float32
BF16 = jnp.bfloat16

D_MODEL = 1024
CHUNK = 64
LN_EPS = 1e-5
FOX_HEADS = 8
FOX_HEAD_DIM = 64
FOX_WIDTH = FOX_HEADS * FOX_HEAD_DIM
GLA_HEADS = 4
GLA_KEY_DIM = 64
GLA_VALUE_DIM = 128
GLA_KEY_WIDTH = GLA_HEADS * GLA_KEY_DIM
GLA_WIDTH = GLA_HEADS * GLA_VALUE_DIM
GLA_GATE_RANK = 16
GLA_GATE_TEMP = 16.0
N_EXPERTS = 256
N_GROUPS = 8
GROUP_SIZE = N_EXPERTS // N_GROUPS
TOPK_GROUPS = 4
TOP_K = 8
EXPERT_DIM = 256
SHARED_DIM = 256
ROUTED_SCALE = 2.5
DEPTH = 1
DEEPNORM_ALPHA = (2.0 * DEPTH) ** 0.25
IN_SPLITS = (FOX_WIDTH, FOX_WIDTH, FOX_WIDTH, FOX_HEADS, GLA_KEY_WIDTH, GLA_KEY_WIDTH, GLA_WIDTH,
             GLA_GATE_RANK, GLA_WIDTH)

LANES = 128
SUB_CHUNK = 16
VMEM_LIMIT = 48 * 1024 * 1024

TM_PROJ = 512
TQ_FOX = 512
TS_GLA = 512
TM_OPROJ = 512
TM_ROUTE = 512
BM_EXPERT = 256
TM_FINAL = 512

SC_CORES = 2
SC_SUBCORES = 16
SC_ROW_WINDOW = 128
ROW_PARTS = 2
PART_WORDS = D_MODEL // 2 // ROW_PARTS


def _cparams(*sem):
    return pltpu.CompilerParams(dimension_semantics=sem, vmem_limit_bytes=VMEM_LIMIT)


def _layer_norm(x, g, b):
    mu = jnp.mean(x, axis=-1, keepdims=True)
    xc = x - mu
    var = jnp.mean(xc * xc, axis=-1, keepdims=True)
    return xc * lax.rsqrt(var + LN_EPS) * g + b


def _log_sigmoid(z):
    return jnp.minimum(z, 0.0) - jnp.log1p(jnp.exp(-jnp.abs(z)))


def _sigmoid(z):
    return 1.0 / (1.0 + jnp.exp(-z))


def _dot(a, b):
    return jnp.dot(a, b, preferred_element_type=F32)


def _dot_nt(a, b):
    return lax.dot_general(a, b, (((1,), (1,)), ((), ())), preferred_element_type=F32)


def _dot_tn(a, b):
    return lax.dot_general(a, b, (((0,), (0,)), ((), ())), preferred_element_type=F32)


def _pack_bf16_pairs(x):
    n = x.shape[1] // 2
    u = lax.bitcast_convert_type(x.astype(BF16).astype(F32), jnp.uint32)
    return (u[:, :n] >> 16) | u[:, n:]


def _unpack_bf16_pairs(w):
    lo = lax.bitcast_convert_type(w << 16, F32)
    hi = lax.bitcast_convert_type(w & jnp.uint32(0xFFFF0000), F32)
    return lo, hi


def _store_row_parts(x, part_refs):
    packed = _pack_bf16_pairs(x)
    for p, ref in enumerate(part_refs):
        ref[...] = packed[:, p * PART_WORDS:(p + 1) * PART_WORDS]


def _load_row_parts(parts):
    out = []
    for p, w in enumerate(parts):
        lo, hi = _unpack_bf16_pairs(w)
        out.append((p * PART_WORDS, lo))
        out.append((D_MODEL // 2 + p * PART_WORDS, hi))
    return sorted(out, key=lambda t: t[0])


def _proj_kernel(x_ref, g_ref, b_ref, wm_ref, ws_ref,
                 h_ref, fq_ref, fk_ref, fv_ref, gq_ref, gk_ref, gv_ref, gr_ref, sm_ref):
    h = _layer_norm(x_ref[...], g_ref[...], b_ref[...])
    h_ref[...] = h
    hb = h.astype(BF16)
    off = 0
    for ref in (fq_ref, fk_ref, fv_ref, gq_ref, gk_ref, gv_ref, gr_ref):
        n = ref.shape[1]
        ref[...] = _dot(hb, wm_ref[:, off:off + n]).astype(ref.dtype)
        off += n
    sm_ref[...] = _dot(hb, ws_ref[...])


def _proj(x2, g, b, w_main, w_small):
    T, D = x2.shape
    tm = TM_PROJ
    widths = (FOX_WIDTH, FOX_WIDTH, FOX_WIDTH, GLA_KEY_WIDTH, GLA_KEY_WIDTH, GLA_WIDTH, GLA_WIDTH)
    row = lambda n: pl.BlockSpec((tm, n), lambda i: (i, 0))
    full = lambda a: pl.BlockSpec(a.shape, lambda i: (0,) * a.ndim)
    out_shape = [jax.ShapeDtypeStruct((T, D), F32)]
    out_shape += [jax.ShapeDtypeStruct((T, n), BF16) for n in widths]
    out_shape += [jax.ShapeDtypeStruct((T, LANES), F32)]
    out_specs = [row(D)] + [row(n) for n in widths] + [row(LANES)]
    return pl.pallas_call(
        _proj_kernel, grid=(T // tm,),
        in_specs=[row(D), full(g), full(b), full(w_main), full(w_small)],
        out_specs=out_specs, out_shape=out_shape,
        compiler_params=_cparams("parallel"), name="proj")(x2, g, b, w_main, w_small)


def _fgate_kernel(sm_ref, bf_ref, c_ref, ct_ref):
    S = sm_ref.shape[0]
    lf = _log_sigmoid(sm_ref[...] + bf_ref[...])
    rows = lax.broadcasted_iota(jnp.int32, lf.shape, 0)
    s = 1
    while s < S:
        lf = lf + jnp.where(rows >= s, pltpu.roll(lf, s, axis=0), 0.0)
        s *= 2
    c_ref[...] = lf
    ct_ref[0] = lf.T[0:FOX_HEADS, :]


def _fgate(small, bf_pad, B, S):
    T = small.shape[0]
    return pl.pallas_call(
        _fgate_kernel, grid=(B,),
        in_specs=[pl.BlockSpec((S, LANES), lambda b: (b, 0)),
                  pl.BlockSpec((1, LANES), lambda b: (0, 0))],
        out_specs=[pl.BlockSpec((S, LANES), lambda b: (b, 0)),
                   pl.BlockSpec((1, FOX_HEADS, S), lambda b: (b, 0, 0))],
        out_shape=[jax.ShapeDtypeStruct((T, LANES), F32),
                   jax.ShapeDtypeStruct((B, FOX_HEADS, S), F32)],
        compiler_params=_cparams("parallel"), name="fgate")(small, bf_pad)


def _fox_kernel(q_ref, k_ref, v_ref, c_ref, cr_ref, o_ref, *, tq):
    i = pl.program_id(1)
    dh = FOX_HEAD_DIM
    row = lax.broadcasted_iota(jnp.int32, (tq, tq), 0)
    col = lax.broadcasted_iota(jnp.int32, (tq, tq), 1)
    causal = row >= col
    for h in range(FOX_HEADS):
        lanes = slice(h * dh, (h + 1) * dh)
        q = q_ref[:, lanes]
        ct = c_ref[:, h:h + 1]

        def step(j, carry, masked):
            m, l, acc = carry
            r0 = pl.multiple_of(j * tq, tq)
            k = k_ref[pl.ds(r0, tq), lanes]
            v = v_ref[pl.ds(r0, tq), lanes]
            cs = cr_ref[0, j, h:h + 1, :]
            s = _dot_nt(q, k) + (ct - cs)
            if masked:
                s = jnp.where(causal, s, -jnp.inf)
            m_new = jnp.maximum(m, jnp.max(s, axis=-1, keepdims=True))
            p = jnp.exp(s - m_new)
            alpha = jnp.exp(m - m_new)
            l = alpha * l + jnp.sum(p, axis=-1, keepdims=True)
            acc = alpha * acc + _dot(p.astype(BF16), v)
            return m_new, l, acc

        init = (jnp.full((tq, 1), -jnp.inf, F32), jnp.zeros((tq, 1), F32), jnp.zeros((tq, dh), F32))
        carry = lax.fori_loop(0, i, functools.partial(step, masked=False), init)
        _, l, acc = step(i, carry, True)
        o_ref[:, lanes] = (acc / l).astype(o_ref.dtype)


def _fox(fq, fk, fv, c, c_row, B, S):
    T = fq.shape[0]
    tq = TQ_FOX
    nq = S // tq
    return pl.pallas_call(
        functools.partial(_fox_kernel, tq=tq), grid=(B, nq),
        in_specs=[pl.BlockSpec((tq, FOX_WIDTH), lambda b, i: (b * nq + i, 0)),
                  pl.BlockSpec((S, FOX_WIDTH), lambda b, i: (b, 0)),
                  pl.BlockSpec((S, FOX_WIDTH), lambda b, i: (b, 0)),
                  pl.BlockSpec((tq, LANES), lambda b, i: (b * nq + i, 0)),
                  pl.BlockSpec((1, nq, FOX_HEADS, tq), lambda b, i: (b, 0, 0, 0))],
        out_specs=pl.BlockSpec((tq, FOX_WIDTH), lambda b, i: (b * nq + i, 0)),
        out_shape=jax.ShapeDtypeStruct((T, FOX_WIDTH), BF16),
        compiler_params=_cparams("parallel", "arbitrary"), name="fox")(fq, fk, fv, c, c_row)


def _gla_kernel(q_ref, k_ref, v_ref, r_ref, sm_ref, wg_ref, bg_ref, gn_ref, tri_ref, y_ref,
                st_ref, b_ref, o_ref, *, ts):
    dk, dv, C, SC = GLA_KEY_DIM, GLA_VALUE_DIM, CHUNK, SUB_CHUNK

    @pl.when(pl.program_id(1) == 0)
    def _():
        st_ref[...] = jnp.zeros_like(st_ref)

    z = _dot(sm_ref[...].astype(BF16), wg_ref[...]) + bg_ref[...]
    la = _log_sigmoid(z) * (1.0 / GLA_GATE_TEMP)
    hi = la.astype(BF16)
    r1 = la - hi.astype(F32)
    mid = r1.astype(BF16)
    lo = (r1 - mid.astype(F32)).astype(BF16)
    tri = tri_ref[...]
    b_ref[...] = _dot(tri, hi) + _dot(tri, mid) + _dot(tri, lo)

    row = lax.broadcasted_iota(jnp.int32, (C, C), 0)
    col = lax.broadcasted_iota(jnp.int32, (C, C), 1)
    row_in_sub = row % SC
    sub_base = (row // SC) * SC
    col16 = lax.broadcasted_iota(jnp.int32, (SC, C), 1)

    def chunk(c, carry):
        r0 = pl.multiple_of(c * C, C)
        for h in range(GLA_HEADS):
            kl = slice(h * dk, (h + 1) * dk)
            vl = slice(h * dv, (h + 1) * dv)
            q = q_ref[pl.ds(r0, C), kl].astype(F32)
            k = k_ref[pl.ds(r0, C), kl].astype(F32)
            v = v_ref[pl.ds(r0, C), vl]
            b = b_ref[pl.ds(r0, C), kl]
            st = st_ref[h]
            b_last = b[C - 1:C, :]

            o = _dot_nt((q * jnp.exp(b)).astype(BF16), st.astype(BF16))

            pieces = [jnp.zeros((SC, C), F32)]
            for i in range(1, C // SC):
                ref_row = b[SC * i - 1:SC * i, :]
                qt = (q[SC * i:SC * (i + 1), :] * jnp.exp(b[SC * i:SC * (i + 1), :] - ref_row)).astype(BF16)
                kt = (k * jnp.exp(jnp.minimum(ref_row - b, 0.0))).astype(BF16)
                pieces.append(jnp.where(col16 < SC * i, _dot_nt(qt, kt), 0.0))
            scores = jnp.concatenate(pieces, axis=0)

            for j in range(SC):
                bj = jnp.concatenate(
                    [jnp.broadcast_to(b[SC * i + j:SC * i + j + 1, :], (SC, dk)) for i in range(C // SC)], axis=0)
                kj = jnp.concatenate(
                    [jnp.broadcast_to(k[SC * i + j:SC * i + j + 1, :], (SC, dk)) for i in range(C // SC)], axis=0)
                decay = jnp.exp(jnp.where(row_in_sub >= j, b - bj, -jnp.inf))
                colv = jnp.sum(q * kj * decay, axis=-1, keepdims=True)
                scores = jnp.where(col == sub_base + j, colv, scores)

            o = o + _dot(scores.astype(BF16), v)
            o_ref[pl.ds(r0, C), vl] = o
            kh = (k * jnp.exp(b_last - b)).astype(BF16)
            st_ref[h] = st * jnp.exp(b_last) + _dot_tn(v, kh)
        return carry

    lax.fori_loop(0, ts // C, chunk, 0)

    for h in range(GLA_HEADS):
        vl = slice(h * dv, (h + 1) * dv)
        o = o_ref[:, vl]
        ms = jnp.mean(o * o, axis=-1, keepdims=True)
        g = r_ref[:, vl].astype(F32)
        y = o * lax.rsqrt(ms + LN_EPS) * gn_ref[...] * (g * _sigmoid(g))
        y_ref[:, vl] = y.astype(y_ref.dtype)


def _gla(gq, gk, gv, gr, small, wg_pad, bg, gnorm, tri, B, S):
    T = gq.shape[0]
    ts = TS_GLA
    ns = S // ts
    row = lambda n: pl.BlockSpec((ts, n), lambda b, i: (b * ns + i, 0))
    full = lambda a: pl.BlockSpec(a.shape, lambda b, i: (0,) * a.ndim)
    return pl.pallas_call(
        functools.partial(_gla_kernel, ts=ts), grid=(B, ns),
        in_specs=[row(GLA_KEY_WIDTH), row(GLA_KEY_WIDTH), row(GLA_WIDTH), row(GLA_WIDTH), row(LANES),
                  full(wg_pad), full(bg), full(gnorm), full(tri)],
        out_specs=row(GLA_WIDTH),
        out_shape=jax.ShapeDtypeStruct((T, GLA_WIDTH), BF16),
        scratch_shapes=[pltpu.VMEM((GLA_HEADS, GLA_VALUE_DIM, GLA_KEY_DIM), F32),
                        pltpu.VMEM((ts, GLA_KEY_WIDTH), F32),
                        pltpu.VMEM((ts, GLA_WIDTH), F32)],
        compiler_params=_cparams("parallel", "arbitrary"), name="gla")(
            gq, gk, gv, gr, small, wg_pad, bg, gnorm, tri)


def _oproj_kernel(yf_ref, yg_ref, h_ref, wo_ref, g_ref, b_ref, wr_ref, h1_ref, st_ref, *part_refs):
    mix = _dot(yf_ref[...], wo_ref[0:FOX_WIDTH, :]) + _dot(yg_ref[...], wo_ref[FOX_WIDTH:, :])
    h1 = _layer_norm(DEEPNORM_ALPHA * h_ref[...] + mix, g_ref[...], b_ref[...])
    h1_ref[...] = h1
    _store_row_parts(h1, part_refs)
    st_ref[...] = _sigmoid(_dot_nt(wr_ref[...], h1.astype(BF16)))


def _oproj(y_fox, y_gla, h, w_out, g, b, w_router_t):
    T, D = h.shape
    tm = TM_OPROJ
    row = lambda n: pl.BlockSpec((tm, n), lambda i: (i, 0))
    full = lambda a: pl.BlockSpec(a.shape, lambda i: (0,) * a.ndim)
    return pl.pallas_call(
        _oproj_kernel, grid=(T // tm,),
        in_specs=[row(FOX_WIDTH), row(GLA_WIDTH), row(D), full(w_out), full(g), full(b), full(w_router_t)],
        out_specs=[row(D), pl.BlockSpec((N_EXPERTS, tm), lambda i: (0, i))] + [row(PART_WORDS)] * ROW_PARTS,
        out_shape=[jax.ShapeDtypeStruct((T, D), F32), jax.ShapeDtypeStruct((N_EXPERTS, T), F32)]
        + [jax.ShapeDtypeStruct((T, PART_WORDS), jnp.uint32)] * ROW_PARTS,
        compiler_params=_cparams("parallel"), name="oproj")(y_fox, y_gla, h, w_out, g, b, w_router_t)


def _route_kernel(s_ref, bias_ref, upper_ref, ones_ref, idx_ref, w_ref, rank_ref, cnt_ref, carry_ref):
    E, tm = s_ref.shape

    @pl.when(pl.program_id(0) == 0)
    def _():
        carry_ref[...] = jnp.zeros_like(carry_ref)

    s = s_ref[...]
    biased = s + bias_ref[...]
    neg = -jnp.inf
    erow = lax.broadcasted_iota(jnp.int32, (E, tm), 0).astype(F32)
    grow = lax.broadcasted_iota(jnp.int32, (GROUP_SIZE, tm), 0).astype(F32)

    gs = []
    for g in range(N_GROUPS):
        blk = biased[g * GROUP_SIZE:(g + 1) * GROUP_SIZE, :]
        m1 = jnp.max(blk, axis=0, keepdims=True)
        i1 = jnp.min(jnp.where(blk == m1, grow, float(GROUP_SIZE)), axis=0, keepdims=True)
        m2 = jnp.max(jnp.where(grow == i1, neg, blk), axis=0, keepdims=True)
        gs.append(m1 + m2)
    keep = []
    for g in range(N_GROUPS):
        beaten = jnp.zeros((1, tm), F32)
        for o in range(N_GROUPS):
            if o == g:
                continue
            wins = (gs[o] > gs[g]) | ((gs[o] == gs[g]) & (o < g))
            beaten = beaten + jnp.where(wins, 1.0, 0.0)
        keep.append(jnp.broadcast_to(beaten < float(TOPK_GROUPS), (GROUP_SIZE, tm)))
    cur = jnp.where(jnp.concatenate(keep, axis=0), biased, neg)

    ids, ws, hots = [], [], []
    chosen = jnp.zeros((E, tm), F32)
    for _ in range(TOP_K):
        m = jnp.max(cur, axis=0, keepdims=True)
        ik = jnp.min(jnp.where(cur == m, erow, float(E)), axis=0, keepdims=True)
        hot = erow == ik
        ws.append(jnp.sum(jnp.where(hot, s, 0.0), axis=0, keepdims=True))
        cur = jnp.where(hot, neg, cur)
        chosen = jnp.where(hot, 1.0, chosen)
        ids.append(ik)
        hots.append(hot)
    wsum = ws[0]
    for w in ws[1:]:
        wsum = wsum + w

    chosen_b = chosen.astype(BF16)
    before = carry_ref[...] + _dot(chosen_b, upper_ref[...])
    carry_ref[...] = carry_ref[...] + _dot(chosen_b, ones_ref[...])
    for k in range(TOP_K):
        idx_ref[k:k + 1, :] = ids[k].astype(jnp.int32)
        w_ref[k:k + 1, :] = ws[k] / wsum * ROUTED_SCALE
        rank_ref[k:k + 1, :] = jnp.sum(jnp.where(hots[k], before, 0.0), axis=0, keepdims=True).astype(jnp.int32)
    cnt_ref[...] = carry_ref[:, 0:LANES]


def _route(scores_t, bias_col, upper, ones):
    E, T = scores_t.shape
    tm = TM_ROUTE
    full = lambda a: pl.BlockSpec(a.shape, lambda i: (0,) * a.ndim)
    kt = pl.BlockSpec((TOP_K, tm), lambda i: (0, i))
    return pl.pallas_call(
        _route_kernel, grid=(T // tm,),
        in_specs=[pl.BlockSpec((E, tm), lambda i: (0, i)), full(bias_col), full(upper), full(ones)],
        out_specs=[kt, kt, kt, pl.BlockSpec((E, LANES), lambda i: (0, 0))],
        out_shape=[jax.ShapeDtypeStruct((TOP_K, T), jnp.int32), jax.ShapeDtypeStruct((TOP_K, T), F32),
                   jax.ShapeDtypeStruct((TOP_K, T), jnp.int32), jax.ShapeDtypeStruct((E, LANES), F32)],
        scratch_shapes=[pltpu.VMEM((E, tm), F32)],
        compiler_params=_cparams("arbitrary"), name="route")(scores_t, bias_col, upper, ones)


def _expert_kernel(blk_ref, nb_ref, *refs):
    x_refs = refs[:ROW_PARTS]
    wg_ref, wu_ref, wd_ref = refs[ROW_PARTS:ROW_PARTS + 3]
    y_refs = refs[ROW_PARTS + 3:]

    @pl.when(pl.program_id(0) < nb_ref[0])
    def _():
        g = u = None
        for c0, xc in _load_row_parts([r[...] for r in x_refs]):
            xb = xc.astype(BF16)
            rows = slice(c0, c0 + PART_WORDS)
            gc = _dot(xb, wg_ref[rows, :].astype(BF16))
            uc = _dot(xb, wu_ref[rows, :].astype(BF16))
            g = gc if g is None else g + gc
            u = uc if u is None else u + uc
        a = (g * _sigmoid(g) * u).astype(BF16)
        _store_row_parts(_dot(a, wd_ref[...].astype(BF16)), y_refs)


def _experts(blk_exp, nb_used, xs_parts, w_gate, w_up, w_down):
    P = xs_parts[0].shape[0]
    D = D_MODEL
    bm = BM_EXPERT
    nb = P // bm
    part = pl.BlockSpec((bm, PART_WORDS), lambda i, be, nu: (i, 0))
    grid_spec = pltpu.PrefetchScalarGridSpec(
        num_scalar_prefetch=2, grid=(nb,),
        in_specs=[part] * ROW_PARTS + [
            pl.BlockSpec((None, D, EXPERT_DIM), lambda i, be, nu: (be[i], 0, 0)),
            pl.BlockSpec((None, D, EXPERT_DIM), lambda i, be, nu: (be[i], 0, 0)),
            pl.BlockSpec((None, EXPERT_DIM, D), lambda i, be, nu: (be[i], 0, 0))],
        out_specs=[part] * ROW_PARTS)
    return pl.pallas_call(
        _expert_kernel, grid_spec=grid_spec,
        out_shape=[jax.ShapeDtypeStruct((P, PART_WORDS), jnp.uint32)] * ROW_PARTS,
        compiler_params=_cparams("arbitrary"), name="experts")(blk_exp, nb_used, *xs_parts, w_gate, w_up, w_down)


def _pos_kernel(idx_ref, rank_ref, start_ref, pos_ref):
    K, tm = idx_ref.shape
    E = start_ref.shape[0]
    erow = lax.broadcasted_iota(jnp.int32, (E, tm), 0)
    start = start_ref[...]
    for k in range(K):
        first = jnp.sum(jnp.where(erow == idx_ref[k:k + 1, :], start, 0.0), axis=0, keepdims=True)
        pos_ref[k:k + 1, :] = first.astype(jnp.int32) + rank_ref[k:k + 1, :]


def _positions(idx_t, rank_t, start_col):
    K, T = idx_t.shape
    tm = TM_ROUTE
    kt = pl.BlockSpec((K, tm), lambda i: (0, i))
    return pl.pallas_call(
        _pos_kernel, grid=(T // tm,),
        in_specs=[kt, kt, pl.BlockSpec(start_col.shape, lambda i: (0, 0))],
        out_specs=kt, out_shape=jax.ShapeDtypeStruct((K, T), jnp.int32),
        compiler_params=_cparams("parallel"), name="positions")(idx_t, rank_t, start_col)


def _sc_mesh():
    return plsc.VectorSubcoreMesh(core_axis_name="core", subcore_axis_name="subcore",
                                  num_cores=SC_CORES, num_subcores=SC_SUBCORES)


def _sc_scatter_rows(parts, pos, n_out):
    T, W = parts[0].shape
    K = pos.shape[0]
    n = len(parts)
    win = SC_ROW_WINDOW

    @functools.partial(pl.kernel, out_type=[jax.ShapeDtypeStruct((n_out, W), parts[0].dtype)] * n, mesh=_sc_mesh(),
                       name="sc_dispatch")
    def k(*refs):
        x_hbms, p_hbm, o_hbms = refs[:n], refs[n], refs[n + 1:]
        for x_hbm, o_hbm in zip(x_hbms, o_hbms):
            def body(x_vmem, p_vmem, o_hbm=o_hbm):
                for j in range(K):
                    pltpu.sync_copy(x_vmem, o_hbm.at[p_vmem.at[j]])

            pltpu.emit_pipeline(
                body, grid=(T // win,),
                in_specs=[pl.BlockSpec((win, W), lambda i: (i, 0)), pl.BlockSpec((K, win), lambda i: (0, i))],
                out_specs=[], core_axis_name=("core", "subcore"),
                dimension_semantics=(pltpu.PARALLEL,))(x_hbm, p_hbm)

    return k(*parts, pos)


def _sc_gather_rows(tables, idx):
    M = idx.shape[0]
    W = tables[0].shape[1]
    n = len(tables)
    win = SC_ROW_WINDOW

    @functools.partial(pl.kernel, out_type=[jax.ShapeDtypeStruct((M, W), tables[0].dtype)] * n, mesh=_sc_mesh(),
                       name="sc_combine")
    def k(*refs):
        t_hbms, i_hbm, o_hbms = refs[:n], refs[n], refs[n + 1:]
        for t_hbm, o_hbm in zip(t_hbms, o_hbms):
            def body(i_vmem, o_vmem, t_hbm=t_hbm):
                pltpu.sync_copy(t_hbm.at[i_vmem.at[0]], o_vmem)

            pltpu.emit_pipeline(
                body, grid=(M // win,),
                in_specs=[pl.BlockSpec((1, win), lambda i: (0, i))],
                out_specs=[pl.BlockSpec((win, W), lambda i: (i, 0))],
                core_axis_name=("core", "subcore"), dimension_semantics=(pltpu.PARALLEL,))(i_hbm, o_hbm)

    return k(*tables, idx.reshape(1, M))


def _final_kernel(h1_ref, wk_ref, wg_ref, wu_ref, wd_ref, g_ref, b_ref, *refs):
    yk_refs, o_ref = refs[:ROW_PARTS], refs[ROW_PARTS]
    h1 = h1_ref[...]
    hb = h1.astype(BF16)
    g = _dot(hb, wg_ref[...])
    u = _dot(hb, wu_ref[...])
    ffn = _dot((g * _sigmoid(g) * u).astype(BF16), wd_ref[...])
    wk = wk_ref[...]
    sums = None
    for k in range(TOP_K):
        cols = _load_row_parts([r[k] for r in yk_refs])
        terms = [c * wk[:, k:k + 1] for _, c in cols]
        sums = terms if sums is None else [s + t for s, t in zip(sums, terms)]
    ffn = ffn + jnp.concatenate(sums, axis=1)
    o_ref[...] = _layer_norm(DEEPNORM_ALPHA * h1 + ffn, g_ref[...], b_ref[...])


def _final(h1, yk_parts, wk, w_sg, w_su, w_sd, g, b):
    T, D = h1.shape
    tm = TM_FINAL
    row = lambda n: pl.BlockSpec((tm, n), lambda i: (i, 0))
    full = lambda a: pl.BlockSpec(a.shape, lambda i: (0,) * a.ndim)
    return pl.pallas_call(
        _final_kernel, grid=(T // tm,),
        in_specs=[row(D), row(TOP_K), full(w_sg), full(w_su), full(w_sd), full(g), full(b)]
        + [pl.BlockSpec((TOP_K, tm, PART_WORDS), lambda i: (0, i, 0))] * ROW_PARTS,
        out_specs=row(D), out_shape=jax.ShapeDtypeStruct((T, D), F32),
        compiler_params=_cparams("parallel"), name="final")(h1, wk, w_sg, w_su, w_sd, g, b, *yk_parts)


def _block_diag_tri(n, c):
    r = np.arange(n)
    return jnp.asarray(((r[:, None] >= r[None, :]) & (r[:, None] // c == r[None, :] // c)).astype(np.float32), BF16)


def kernel(x, ln_in_g, ln_in_b, w_in, b_fgate, w_gate_up, b_gate, g_gla_norm, w_out, ln1_g, ln1_b, w_router,
           router_bias, w_exp_gate, w_exp_up, w_exp_down, w_sh_gate, w_sh_up, w_sh_down, ln2_g, ln2_b):
    B, S, D = x.shape
    T = B * S
    x2 = x.reshape(T, D)
    l = 0
    row = lambda a: a.reshape(1, -1)

    off = np.cumsum((0,) + IN_SPLITS)
    seg = lambda i: w_in[l][:, off[i]:off[i + 1]]
    w_main = jnp.concatenate([seg(0) * FOX_HEAD_DIM ** -0.5, seg(1), seg(2), seg(4) * GLA_KEY_DIM ** -0.5, seg(5),
                              seg(6), seg(8)], axis=1).astype(BF16)
    n_small = FOX_HEADS + GLA_GATE_RANK
    w_small = jnp.concatenate([seg(3), seg(7), jnp.zeros((D, LANES - n_small), F32)], axis=1).astype(BF16)
    bf_pad = jnp.concatenate([b_fgate[l], jnp.zeros((LANES - FOX_HEADS,), F32)]).reshape(1, LANES)
    wg_pad = jnp.zeros((LANES, GLA_KEY_WIDTH), F32).at[FOX_HEADS:n_small].set(w_gate_up[l]).astype(BF16)

    h, fq, fk, fv, gq, gk, gv, gr, small = _proj(x2, row(ln_in_g), row(ln_in_b), w_main, w_small)

    c, c_t = _fgate(small, bf_pad, B, S)
    nq = S // TQ_FOX
    c_row = c_t.reshape(B, FOX_HEADS, nq, TQ_FOX).transpose(0, 2, 1, 3)
    y_fox = _fox(fq, fk, fv, c, c_row, B, S)

    y_gla = _gla(gq, gk, gv, gr, small, wg_pad, row(b_gate[l]), row(g_gla_norm[l]),
                 _block_diag_tri(TS_GLA, CHUNK), B, S)

    h1, scores_t, *h1_parts = _oproj(y_fox, y_gla, h, w_out[l].astype(BF16), row(ln1_g[l]), row(ln1_b[l]),
                                     w_router[l].T.astype(BF16))

    tm = TM_ROUTE
    r = np.arange(tm)
    upper = jnp.asarray((r[:, None] < r[None, :]).astype(np.float32), BF16)
    ones = jnp.ones((tm, tm), BF16)
    idx_t, w_t, rank_t, cnt = _route(scores_t, router_bias[l].reshape(N_EXPERTS, 1), upper, ones)

    bm = BM_EXPERT
    A = T * TOP_K
    nb = A // bm + N_EXPERTS
    P = nb * bm
    counts = cnt[:, 0].astype(jnp.int32)
    padded = (counts + bm - 1) // bm * bm
    pend = jnp.cumsum(padded)
    pstart = pend - padded
    pos = _positions(idx_t, rank_t, pstart.astype(F32).reshape(N_EXPERTS, 1))
    nb_used = (pend[-1] // bm).astype(jnp.int32)
    blk_row0 = jnp.arange(nb, dtype=jnp.int32) * bm
    blk = jnp.minimum(jnp.sum(pend[None, :] <= blk_row0[:, None], axis=1), N_EXPERTS - 1).astype(jnp.int32)
    blk = jnp.where(jnp.arange(nb) < nb_used, blk, jnp.max(jnp.where(jnp.arange(nb) < nb_used, blk, 0)))

    xs_parts = _sc_scatter_rows(h1_parts, pos, P)
    ys_parts = _experts(blk, nb_used.reshape(1), xs_parts, w_exp_gate[l], w_exp_up[l], w_exp_down[l])
    yk_parts = [y.reshape(TOP_K, T, PART_WORDS) for y in _sc_gather_rows(ys_parts, pos.reshape(-1))]

    out = _final(h1, yk_parts, w_t.T, w_sh_gate[l].astype(BF16), w_sh_up[l].astype(BF16),
                 w_sh_down[l].astype(BF16), row(ln2_g[l]), row(ln2_b[l]))
    return out.reshape(B, S, D)
```

```python
import functools

import jax
import jax.numpy as jnp
import numpy as np
from jax import lax
from jax.experimental import pallas as pl
from jax.experimental.pallas import tpu as pltpu
from jax.experimental.pallas import tpu_sc as plsc

F32 = jnp.float32
BF16 = jnp.bfloat16

D_MODEL = 1024
CHUNK = 64
LN_EPS = 1e-5
LOG2E = 1.4426950408889634
FOX_HEADS = 8
FOX_HEAD_DIM = 64
FOX_WIDTH = FOX_HEADS * FOX_HEAD_DIM
GLA_HEADS = 4
GLA_KEY_DIM = 64
GLA_VALUE_DIM = 128
GLA_KEY_WIDTH = GLA_HEADS * GLA_KEY_DIM
GLA_WIDTH = GLA_HEADS * GLA_VALUE_DIM
GLA_GATE_RANK = 16
GLA_GATE_TEMP = 16.0
N_EXPERTS = 256
N_GROUPS = 8
GROUP_SIZE = N_EXPERTS // N_GROUPS
TOPK_GROUPS = 4
TOP_K = 8
EXPERT_DIM = 256
SHARED_DIM = 256
ROUTED_SCALE = 2.5
DEPTH = 1
DEEPNORM_ALPHA = (2.0 * DEPTH) ** 0.25
IN_SPLITS = (FOX_WIDTH, FOX_WIDTH, FOX_WIDTH, FOX_HEADS, GLA_KEY_WIDTH, GLA_KEY_WIDTH, GLA_WIDTH,
             GLA_GATE_RANK, GLA_WIDTH)

LANES = 128
SUB_CHUNK = 16
VMEM_LIMIT = 48 * 1024 * 1024

TM_PROJ = 512
TQ_FOX = 512
TK_FOX = 512
FOX_GROUP = 4
TS_GLA = 512
TM_OPROJ = 512
TM_ROUTE = 512
BM_EXPERT = 256
TM_FINAL = 512

SC_CORES = 2
SC_SUBCORES = 16
SC_ROW_WINDOW = 128
ROW_PARTS = 2
PART_WORDS = D_MODEL // 2 // ROW_PARTS


def _cparams(*sem):
    return pltpu.CompilerParams(dimension_semantics=sem, vmem_limit_bytes=VMEM_LIMIT)


def _layer_norm(x, g, b):
    mu = jnp.mean(x, axis=-1, keepdims=True)
    xc = x - mu
    var = jnp.mean(xc * xc, axis=-1, keepdims=True)
    return xc * lax.rsqrt(var + LN_EPS) * g + b


def _log_sigmoid(z):
    return jnp.minimum(z, 0.0) - jnp.log1p(jnp.exp(-jnp.abs(z)))


def _sigmoid(z):
    return 1.0 / (1.0 + jnp.exp(-z))


def _dot(a, b):
    return jnp.dot(a, b, preferred_element_type=F32)


def _dot_nt(a, b):
    return lax.dot_general(a, b, (((1,), (1,)), ((), ())), preferred_element_type=F32)


def _dot_tn(a, b):
    return lax.dot_general(a, b, (((0,), (0,)), ((), ())), preferred_element_type=F32)


def _pack_bf16_pairs(x):
    n = x.shape[1] // 2
    u = lax.bitcast_convert_type(x.astype(BF16).astype(F32), jnp.uint32)
    return (u[:, :n] >> 16) | u[:, n:]


def _unpack_bf16_pairs(w):
    lo = lax.bitcast_convert_type(w << 16, F32)
    hi = lax.bitcast_convert_type(w & jnp.uint32(0xFFFF0000), F32)
    return lo, hi


def _store_row_parts(x, part_refs):
    packed = _pack_bf16_pairs(x)
    for p, ref in enumerate(part_refs):
        ref[...] = packed[:, p * PART_WORDS:(p + 1) * PART_WORDS]


def _load_row_parts(parts):
    out = []
    for p, w in enumerate(parts):
        lo, hi = _unpack_bf16_pairs(w)
        out.append((p * PART_WORDS, lo))
        out.append((D_MODEL // 2 + p * PART_WORDS, hi))
    return sorted(out, key=lambda t: t[0])


def _proj_kernel(x_ref, g_ref, b_ref, wm_ref, ws_ref,
                 h_ref, fq_ref, fk_ref, fv_ref, gq_ref, gk_ref, gv_ref, gr_ref, sm_ref):
    h = _layer_norm(x_ref[...], g_ref[...], b_ref[...])
    h_ref[...] = h
    hb = h.astype(BF16)
    off = 0
    for ref in (fq_ref, fk_ref, fv_ref, gq_ref, gk_ref, gv_ref, gr_ref):
        n = ref.shape[1]
        ref[...] = _dot(hb, wm_ref[:, off:off + n]).astype(ref.dtype)
        off += n
    sm_ref[...] = _dot(hb, ws_ref[...])


def _proj(x2, g, b, w_main, w_small):
    T, D = x2.shape
    tm = TM_PROJ
    widths = (FOX_WIDTH, FOX_WIDTH, FOX_WIDTH, GLA_KEY_WIDTH, GLA_KEY_WIDTH, GLA_WIDTH, GLA_WIDTH)
    row = lambda n: pl.BlockSpec((tm, n), lambda i: (i, 0))
    full = lambda a: pl.BlockSpec(a.shape, lambda i: (0,) * a.ndim)
    out_shape = [jax.ShapeDtypeStruct((T, D), F32)]
    out_shape += [jax.ShapeDtypeStruct((T, n), BF16) for n in widths]
    out_shape += [jax.ShapeDtypeStruct((T, LANES), F32)]
    out_specs = [row(D)] + [row(n) for n in widths] + [row(LANES)]
    return pl.pallas_call(
        _proj_kernel, grid=(T // tm,),
        in_specs=[row(D), full(g), full(b), full(w_main), full(w_small)],
        out_specs=out_specs, out_shape=out_shape,
        compiler_params=_cparams("parallel"), name="proj")(x2, g, b, w_main, w_small)


def _fgate_kernel(sm_ref, bf_ref, c_ref):
    S = sm_ref.shape[0]
    lf = _log_sigmoid(sm_ref[...] + bf_ref[...])
    rows = lax.broadcasted_iota(jnp.int32, lf.shape, 0)
    s = 1
    while s < S:
        lf = lf + jnp.where(rows >= s, pltpu.roll(lf, s, axis=0), 0.0)
        s *= 2
    c_ref[...] = lf * LOG2E


def _fgate(small, bf_pad, B, S):
    T = small.shape[0]
    return pl.pallas_call(
        _fgate_kernel, grid=(B,),
        in_specs=[pl.BlockSpec((S, LANES), lambda b: (b, 0)),
                  pl.BlockSpec((1, LANES), lambda b: (0, 0))],
        out_specs=pl.BlockSpec((S, LANES), lambda b: (b, 0)),
        out_shape=jax.ShapeDtypeStruct((T, LANES), F32),
        compiler_params=_cparams("parallel"), name="fgate")(small, bf_pad)


def _split3(x):
    hi = x.astype(BF16).astype(F32)
    r = x - hi
    mid = r.astype(BF16).astype(F32)
    return hi, mid, (r - mid).astype(BF16).astype(F32)


def _fox_operand(pair_ref_tile, odd, c_col, is_key):
    dh = FOX_HEAD_DIM
    x = pair_ref_tile.astype(F32)
    if odd:
        x = pltpu.roll(x, dh, axis=1)
    lane = lax.broadcasted_iota(jnp.int32, x.shape, 1)
    t0, t1, t2 = _split3(-c_col if is_key else c_col)
    first = dh + 3 if is_key else dh
    bias = jnp.where(lane == first, t0, jnp.where(lane == first + 1, t1, jnp.where(lane == first + 2, t2, 0.0)))
    ones_lo = dh if is_key else dh + 3
    bias = jnp.where((lane >= ones_lo) & (lane < ones_lo + 3), 1.0, bias)
    return jnp.where(lane < dh, x, bias).astype(BF16)


def _fox_kernel(q_ref, k_ref, v_ref, cq_ref, ck_ref, o_ref, ka_ref, vt_ref, *, tq, tk):
    i = pl.program_id(1)
    S = k_ref.shape[0]
    dh = FOX_HEAD_DIM

    @pl.when(i == 0)
    def _():
        def fill(r, carry):
            r0 = pl.multiple_of(r * tk, tk)
            for h in range(FOX_HEADS):
                pair = slice(LANES * (h // 2), LANES * (h // 2 + 1))
                ka_ref[pl.ds(r0, tk), LANES * h:LANES * (h + 1)] = _fox_operand(
                    k_ref[pl.ds(r0, tk), pair], h % 2, ck_ref[pl.ds(r0, tk), h:h + 1], True)
            vt_ref[r] = v_ref[pl.ds(r0, tk), :].astype(F32).T.astype(BF16)
            return carry
        lax.fori_loop(0, S // tk, fill, 0)

    key = lax.broadcasted_iota(jnp.int32, (tk, tq), 0)
    qry = lax.broadcasted_iota(jnp.int32, (tk, tq), 1)
    n_diag = tq // tk
    n_full = i * n_diag

    for p in range(FOX_HEADS // FOX_GROUP):
        group = slice(dh * FOX_GROUP * p, dh * FOX_GROUP * (p + 1))
        heads = tuple(range(FOX_GROUP * p, FOX_GROUP * (p + 1)))
        qa = [_fox_operand(q_ref[:, LANES * (h // 2):LANES * (h // 2 + 1)], h % 2, cq_ref[:, h:h + 1], False)
              for h in heads]

        def step(j, carry, diag):
            r0 = pl.multiple_of(j * tk, tk)
            ss = [_dot_nt(ka_ref[pl.ds(r0, tk), LANES * h:LANES * (h + 1)], qa[n])
                  for n, h in enumerate(heads)]
            if diag is not None:
                ss = [jnp.where(qry >= key + diag * tk, s, -jnp.inf) for s in ss]
            ms = [jnp.maximum(c[0], jnp.max(s, axis=0, keepdims=True)) for c, s in zip(carry, ss)]
            prs = [jnp.exp2(s - m) for s, m in zip(ss, ms)]
            alphas = [jnp.exp2(c[0] - m) for c, m in zip(carry, ms)]
            ls = [a * c[1] + jnp.sum(pr, axis=0, keepdims=True) for a, c, pr in zip(alphas, carry, prs)]
            pvs = [_dot(vt_ref[j, dh * h:dh * (h + 1), :], pr.astype(BF16)) for h, pr in zip(heads, prs)]
            accs = [a * c[2] + pv for a, c, pv in zip(alphas, carry, pvs)]
            return tuple(zip(ms, ls, accs))

        init = tuple((jnp.full((1, tq), -jnp.inf, F32), jnp.zeros((1, tq), F32), jnp.zeros((dh, tq), F32))
                     for _ in heads)
        carry = lax.fori_loop(0, n_full, functools.partial(step, diag=None), init)
        for d in range(n_diag):
            carry = step(n_full + d, carry, d)
        o_ref[:, group] = jnp.concatenate([(acc / l).T for _, l, acc in carry], axis=1).astype(o_ref.dtype)


def _fox(fq, fk, fv, c, B, S):
    T = fq.shape[0]
    tq, tk = TQ_FOX, TK_FOX
    nq = S // tq
    return pl.pallas_call(
        functools.partial(_fox_kernel, tq=tq, tk=tk), grid=(B, nq),
        in_specs=[pl.BlockSpec((tq, FOX_WIDTH), lambda b, i: (b * nq + i, 0)),
                  pl.BlockSpec((S, FOX_WIDTH), lambda b, i: (b, 0)),
                  pl.BlockSpec((S, FOX_WIDTH), lambda b, i: (b, 0)),
                  pl.BlockSpec((tq, LANES), lambda b, i: (b * nq + i, 0)),
                  pl.BlockSpec((S, LANES), lambda b, i: (b, 0))],
        out_specs=pl.BlockSpec((tq, FOX_WIDTH), lambda b, i: (b * nq + i, 0)),
        out_shape=jax.ShapeDtypeStruct((T, FOX_WIDTH), BF16),
        scratch_shapes=[pltpu.VMEM((S, FOX_HEADS * LANES), BF16), pltpu.VMEM((S // tk, FOX_WIDTH, tk), BF16)],
        compiler_params=_cparams("parallel", "arbitrary"), name="fox")(fq, fk, fv, c, c)


def _gla_kernel(q_ref, k_ref, v_ref, r_ref, sm_ref, wg_ref, bg_ref, gn_ref, tri_ref, y_ref,
                st_ref, b_ref, o_ref, *, ts):
    dk, dv, C, SC = GLA_KEY_DIM, GLA_VALUE_DIM, CHUNK, SUB_CHUNK

    @pl.when(pl.program_id(1) == 0)
    def _():
        st_ref[...] = jnp.zeros_like(st_ref)

    z = _dot(sm_ref[...].astype(BF16), wg_ref[...]) + bg_ref[...]
    la = _log_sigmoid(z) * (1.0 / GLA_GATE_TEMP)
    hi = la.astype(BF16)
    r1 = la - hi.astype(F32)
    mid = r1.astype(BF16)
    lo = (r1 - mid.astype(F32)).astype(BF16)
    tri = tri_ref[...]
    b_ref[...] = _dot(tri, hi) + _dot(tri, mid) + _dot(tri, lo)

    row = lax.broadcasted_iota(jnp.int32, (C, C), 0)
    col = lax.broadcasted_iota(jnp.int32, (C, C), 1)
    row_in_sub = row % SC
    sub_base = (row // SC) * SC
    col16 = lax.broadcasted_iota(jnp.int32, (SC, C), 1)

    def chunk(c, carry):
        r0 = pl.multiple_of(c * C, C)
        for h in range(GLA_HEADS):
            kl = slice(h * dk, (h + 1) * dk)
            vl = slice(h * dv, (h + 1) * dv)
            q = q_ref[pl.ds(r0, C), kl].astype(F32)
            k = k_ref[pl.ds(r0, C), kl].astype(F32)
            v = v_ref[pl.ds(r0, C), vl]
            b = b_ref[pl.ds(r0, C), kl]
            st = st_ref[h]
            b_last = b[C - 1:C, :]

            o = _dot_nt((q * jnp.exp(b)).astype(BF16), st.astype(BF16))

            pieces = [jnp.zeros((SC, C), F32)]
            for i in range(1, C // SC):
                ref_row = b[SC * i - 1:SC * i, :]
                qt = (q[SC * i:SC * (i + 1), :] * jnp.exp(b[SC * i:SC * (i + 1), :] - ref_row)).astype(BF16)
                kt = (k * jnp.exp(jnp.minimum(ref_row - b, 0.0))).astype(BF16)
                pieces.append(jnp.where(col16 < SC * i, _dot_nt(qt, kt), 0.0))
            scores = jnp.concatenate(pieces, axis=0)

            for j in range(SC):
                bj = jnp.concatenate(
                    [jnp.broadcast_to(b[SC * i + j:SC * i + j + 1, :], (SC, dk)) for i in range(C // SC)], axis=0)
                kj = jnp.concatenate(
                    [jnp.broadcast_to(k[SC * i + j:SC * i + j + 1, :], (SC, dk)) for i in range(C // SC)], axis=0)
                decay = jnp.exp(jnp.where(row_in_sub >= j, b - bj, -jnp.inf))
                colv = jnp.sum(q * kj * decay, axis=-1, keepdims=True)
                scores = jnp.where(col == sub_base + j, colv, scores)

            o = o + _dot(scores.astype(BF16), v)
            o_ref[pl.ds(r0, C), vl] = o
            kh = (k * jnp.exp(b_last - b)).astype(BF16)
            st_ref[h] = st * jnp.exp(b_last) + _dot_tn(v, kh)
        return carry

    lax.fori_loop(0, ts // C, chunk, 0)

    for h in range(GLA_HEADS):
        vl = slice(h * dv, (h + 1) * dv)
        o = o_ref[:, vl]
        ms = jnp.mean(o * o, axis=-1, keepdims=True)
        g = r_ref[:, vl].astype(F32)
        y = o * lax.rsqrt(ms + LN_EPS) * gn_ref[...] * (g * _sigmoid(g))
        y_ref[:, vl] = y.astype(y_ref.dtype)


def _gla(gq, gk, gv, gr, small, wg_pad, bg, gnorm, tri, B, S):
    T = gq.shape[0]
    ts = TS_GLA
    ns = S // ts
    row = lambda n: pl.BlockSpec((ts, n), lambda b, i: (b * ns + i, 0))
    full = lambda a: pl.BlockSpec(a.shape, lambda b, i: (0,) * a.ndim)
    return pl.pallas_call(
        functools.partial(_gla_kernel, ts=ts), grid=(B, ns),
        in_specs=[row(GLA_KEY_WIDTH), row(GLA_KEY_WIDTH), row(GLA_WIDTH), row(GLA_WIDTH), row(LANES),
                  full(wg_pad), full(bg), full(gnorm), full(tri)],
        out_specs=row(GLA_WIDTH),
        out_shape=jax.ShapeDtypeStruct((T, GLA_WIDTH), BF16),
        scratch_shapes=[pltpu.VMEM((GLA_HEADS, GLA_VALUE_DIM, GLA_KEY_DIM), F32),
                        pltpu.VMEM((ts, GLA_KEY_WIDTH), F32),
                        pltpu.VMEM((ts, GLA_WIDTH), F32)],
        compiler_params=_cparams("parallel", "arbitrary"), name="gla")(
            gq, gk, gv, gr, small, wg_pad, bg, gnorm, tri)


def _oproj_kernel(yf_ref, yg_ref, h_ref, wo_ref, g_ref, b_ref, wr_ref, h1_ref, st_ref, *part_refs):
    mix = _dot(yf_ref[...], wo_ref[0:FOX_WIDTH, :]) + _dot(yg_ref[...], wo_ref[FOX_WIDTH:, :])
    h1 = _layer_norm(DEEPNORM_ALPHA * h_ref[...] + mix, g_ref[...], b_ref[...])
    h1_ref[...] = h1
    _store_row_parts(h1, part_refs)
    st_ref[...] = _sigmoid(_dot_nt(wr_ref[...], h1.astype(BF16)))


def _oproj(y_fox, y_gla, h, w_out, g, b, w_router_t):
    T, D = h.shape
    tm = TM_OPROJ
    row = lambda n: pl.BlockSpec((tm, n), lambda i: (i, 0))
    full = lambda a: pl.BlockSpec(a.shape, lambda i: (0,) * a.ndim)
    return pl.pallas_call(
        _oproj_kernel, grid=(T // tm,),
        in_specs=[row(FOX_WIDTH), row(GLA_WIDTH), row(D), full(w_out), full(g), full(b), full(w_router_t)],
        out_specs=[row(D), pl.BlockSpec((N_EXPERTS, tm), lambda i: (0, i))] + [row(PART_WORDS)] * ROW_PARTS,
        out_shape=[jax.ShapeDtypeStruct((T, D), F32), jax.ShapeDtypeStruct((N_EXPERTS, T), F32)]
        + [jax.ShapeDtypeStruct((T, PART_WORDS), jnp.uint32)] * ROW_PARTS,
        compiler_params=_cparams("parallel"), name="oproj")(y_fox, y_gla, h, w_out, g, b, w_router_t)


def _route_kernel(s_ref, bias_ref, upper_ref, ones_ref, idx_ref, w_ref, rank_ref, cnt_ref, carry_ref):
    E, tm = s_ref.shape

    @pl.when(pl.program_id(0) == 0)
    def _():
        carry_ref[...] = jnp.zeros_like(carry_ref)

    s = s_ref[...]
    biased = s + bias_ref[...]
    neg = -jnp.inf
    erow = lax.broadcasted_iota(jnp.int32, (E, tm), 0).astype(F32)
    grow = lax.broadcasted_iota(jnp.int32, (GROUP_SIZE, tm), 0).astype(F32)

    gs = []
    for g in range(N_GROUPS):
        blk = biased[g * GROUP_SIZE:(g + 1) * GROUP_SIZE, :]
        m1 = jnp.max(blk, axis=0, keepdims=True)
        i1 = jnp.min(jnp.where(blk == m1, grow, float(GROUP_SIZE)), axis=0, keepdims=True)
        m2 = jnp.max(jnp.where(grow == i1, neg, blk), axis=0, keepdims=True)
        gs.append(m1 + m2)
    keep = []
    for g in range(N_GROUPS):
        beaten = jnp.zeros((1, tm), F32)
        for o in range(N_GROUPS):
            if o == g:
                continue
            wins = (gs[o] > gs[g]) | ((gs[o] == gs[g]) & (o < g))
            beaten = beaten + jnp.where(wins, 1.0, 0.0)
        keep.append(jnp.broadcast_to(beaten < float(TOPK_GROUPS), (GROUP_SIZE, tm)))
    cur = jnp.where(jnp.concatenate(keep, axis=0), biased, neg)

    ids, ws, hots = [], [], []
    chosen = jnp.zeros((E, tm), F32)
    for _ in range(TOP_K):
        m = jnp.max(cur, axis=0, keepdims=True)
        ik = jnp.min(jnp.where(cur == m, erow, float(E)), axis=0, keepdims=True)
        hot = erow == ik
        ws.append(jnp.sum(jnp.where(hot, s, 0.0), axis=0, keepdims=True))
        cur = jnp.where(hot, neg, cur)
        chosen = jnp.where(hot, 1.0, chosen)
        ids.append(ik)
        hots.append(hot)
    wsum = ws[0]
    for w in ws[1:]:
        wsum = wsum + w

    chosen_b = chosen.astype(BF16)
    before = carry_ref[...] + _dot(chosen_b, upper_ref[...])
    carry_ref[...] = carry_ref[...] + _dot(chosen_b, ones_ref[...])
    for k in range(TOP_K):
        idx_ref[k:k + 1, :] = ids[k].astype(jnp.int32)
        w_ref[k:k + 1, :] = ws[k] / wsum * ROUTED_SCALE
        rank_ref[k:k + 1, :] = jnp.sum(jnp.where(hots[k], before, 0.0), axis=0, keepdims=True).astype(jnp.int32)
    cnt_ref[...] = carry_ref[:, 0:LANES]


def _route(scores_t, bias_col, upper, ones):
    E, T = scores_t.shape
    tm = TM_ROUTE
    full = lambda a: pl.BlockSpec(a.shape, lambda i: (0,) * a.ndim)
    kt = pl.BlockSpec((TOP_K, tm), lambda i: (0, i))
    return pl.pallas_call(
        _route_kernel, grid=(T // tm,),
        in_specs=[pl.BlockSpec((E, tm), lambda i: (0, i)), full(bias_col), full(upper), full(ones)],
        out_specs=[kt, kt, kt, pl.BlockSpec((E, LANES), lambda i: (0, 0))],
        out_shape=[jax.ShapeDtypeStruct((TOP_K, T), jnp.int32), jax.ShapeDtypeStruct((TOP_K, T), F32),
                   jax.ShapeDtypeStruct((TOP_K, T), jnp.int32), jax.ShapeDtypeStruct((E, LANES), F32)],
        scratch_shapes=[pltpu.VMEM((E, tm), F32)],
        compiler_params=_cparams("arbitrary"), name="route")(scores_t, bias_col, upper, ones)


def _expert_kernel(blk_ref, nb_ref, *refs):
    x_refs = refs[:ROW_PARTS]
    wg_ref, wu_ref, wd_ref = refs[ROW_PARTS:ROW_PARTS + 3]
    y_refs = refs[ROW_PARTS + 3:]

    @pl.when(pl.program_id(0) < nb_ref[0])
    def _():
        g = u = None
        for c0, xc in _load_row_parts([r[...] for r in x_refs]):
            xb = xc.astype(BF16)
            rows = slice(c0, c0 + PART_WORDS)
            gc = _dot(xb, wg_ref[rows, :].astype(BF16))
            uc = _dot(xb, wu_ref[rows, :].astype(BF16))
            g = gc if g is None else g + gc
            u = uc if u is None else u + uc
        a = (g * _sigmoid(g) * u).astype(BF16)
        _store_row_parts(_dot(a, wd_ref[...].astype(BF16)), y_refs)


def _experts(blk_exp, nb_used, xs_parts, w_gate, w_up, w_down):
    P = xs_parts[0].shape[0]
    D = D_MODEL
    bm = BM_EXPERT
    nb = P // bm
    part = pl.BlockSpec((bm, PART_WORDS), lambda i, be, nu: (i, 0))
    grid_spec = pltpu.PrefetchScalarGridSpec(
        num_scalar_prefetch=2, grid=(nb,),
        in_specs=[part] * ROW_PARTS + [
            pl.BlockSpec((None, D, EXPERT_DIM), lambda i, be, nu: (be[i], 0, 0)),
            pl.BlockSpec((None, D, EXPERT_DIM), lambda i, be, nu: (be[i], 0, 0)),
            pl.BlockSpec((None, EXPERT_DIM, D), lambda i, be, nu: (be[i], 0, 0))],
        out_specs=[part] * ROW_PARTS)
    return pl.pallas_call(
        _expert_kernel, grid_spec=grid_spec,
        out_shape=[jax.ShapeDtypeStruct((P, PART_WORDS), jnp.uint32)] * ROW_PARTS,
        compiler_params=_cparams("arbitrary"), name="experts")(blk_exp, nb_used, *xs_parts, w_gate, w_up, w_down)


def _pos_kernel(idx_ref, rank_ref, start_ref, pos_ref):
    K, tm = idx_ref.shape
    E = start_ref.shape[0]
    erow = lax.broadcasted_iota(jnp.int32, (E, tm), 0)
    start = start_ref[...]
    for k in range(K):
        first = jnp.sum(jnp.where(erow == idx_ref[k:k + 1, :], start, 0.0), axis=0, keepdims=True)
        pos_ref[k:k + 1, :] = first.astype(jnp.int32) + rank_ref[k:k + 1, :]


def _positions(idx_t, rank_t, start_col):
    K, T = idx_t.shape
    tm = TM_ROUTE
    kt = pl.BlockSpec((K, tm), lambda i: (0, i))
    return pl.pallas_call(
        _pos_kernel, grid=(T // tm,),
        in_specs=[kt, kt, pl.BlockSpec(start_col.shape, lambda i: (0, 0))],
        out_specs=kt, out_shape=jax.ShapeDtypeStruct((K, T), jnp.int32),
        compiler_params=_cparams("parallel"), name="positions")(idx_t, rank_t, start_col)


def _sc_mesh():
    return plsc.VectorSubcoreMesh(core_axis_name="core", subcore_axis_name="subcore",
                                  num_cores=SC_CORES, num_subcores=SC_SUBCORES)


def _sc_scatter_rows(parts, pos, n_out):
    T, W = parts[0].shape
    K = pos.shape[0]
    n = len(parts)
    win = SC_ROW_WINDOW

    @functools.partial(pl.kernel, out_type=[jax.ShapeDtypeStruct((n_out, W), parts[0].dtype)] * n, mesh=_sc_mesh(),
                       name="sc_dispatch")
    def k(*refs):
        x_hbms, p_hbm, o_hbms = refs[:n], refs[n], refs[n + 1:]
        for x_hbm, o_hbm in zip(x_hbms, o_hbms):
            def body(x_vmem, p_vmem, o_hbm=o_hbm):
                for j in range(K):
                    pltpu.sync_copy(x_vmem, o_hbm.at[p_vmem.at[j]])

            pltpu.emit_pipeline(
                body, grid=(T // win,),
                in_specs=[pl.BlockSpec((win, W), lambda i: (i, 0)), pl.BlockSpec((K, win), lambda i: (0, i))],
                out_specs=[], core_axis_name=("core", "subcore"),
                dimension_semantics=(pltpu.PARALLEL,))(x_hbm, p_hbm)

    return k(*parts, pos)


def _sc_gather_rows(tables, idx):
    M = idx.shape[0]
    W = tables[0].shape[1]
    n = len(tables)
    win = SC_ROW_WINDOW

    @functools.partial(pl.kernel, out_type=[jax.ShapeDtypeStruct((M, W), tables[0].dtype)] * n, mesh=_sc_mesh(),
                       name="sc_combine")
    def k(*refs):
        t_hbms, i_hbm, o_hbms = refs[:n], refs[n], refs[n + 1:]
        for t_hbm, o_hbm in zip(t_hbms, o_hbms):
            def body(i_vmem, o_vmem, t_hbm=t_hbm):
                pltpu.sync_copy(t_hbm.at[i_vmem.at[0]], o_vmem)

            pltpu.emit_pipeline(
                body, grid=(M // win,),
                in_specs=[pl.BlockSpec((1, win), lambda i: (0, i))],
                out_specs=[pl.BlockSpec((win, W), lambda i: (i, 0))],
                core_axis_name=("core", "subcore"), dimension_semantics=(pltpu.PARALLEL,))(i_hbm, o_hbm)

    return k(*tables, idx.reshape(1, M))


def _final_kernel(h1_ref, wk_ref, wg_ref, wu_ref, wd_ref, g_ref, b_ref, *refs):
    yk_refs, o_ref = refs[:ROW_PARTS], refs[ROW_PARTS]
    h1 = h1_ref[...]
    hb = h1.astype(BF16)
    g = _dot(hb, wg_ref[...])
    u = _dot(hb, wu_ref[...])
    ffn = _dot((g * _sigmoid(g) * u).astype(BF16), wd_ref[...])
    wk = wk_ref[...]
    sums = None
    for k in range(TOP_K):
        cols = _load_row_parts([r[k] for r in yk_refs])
        terms = [c * wk[:, k:k + 1] for _, c in cols]
        sums = terms if sums is None else [s + t for s, t in zip(sums, terms)]
    ffn = ffn + jnp.concatenate(sums, axis=1)
    o_ref[...] = _layer_norm(DEEPNORM_ALPHA * h1 + ffn, g_ref[...], b_ref[...])


def _final(h1, yk_parts, wk, w_sg, w_su, w_sd, g, b):
    T, D = h1.shape
    tm = TM_FINAL
    row = lambda n: pl.BlockSpec((tm, n), lambda i: (i, 0))
    full = lambda a: pl.BlockSpec(a.shape, lambda i: (0,) * a.ndim)
    return pl.pallas_call(
        _final_kernel, grid=(T // tm,),
        in_specs=[row(D), row(TOP_K), full(w_sg), full(w_su), full(w_sd), full(g), full(b)]
        + [pl.BlockSpec((TOP_K, tm, PART_WORDS), lambda i: (0, i, 0))] * ROW_PARTS,
        out_specs=row(D), out_shape=jax.ShapeDtypeStruct((T, D), F32),
        compiler_params=_cparams("parallel"), name="final")(h1, wk, w_sg, w_su, w_sd, g, b, *yk_parts)


def _block_diag_tri(n, c):
    r = np.arange(n)
    return jnp.asarray(((r[:, None] >= r[None, :]) & (r[:, None] // c == r[None, :] // c)).astype(np.float32), BF16)


def kernel(x, ln_in_g, ln_in_b, w_in, b_fgate, w_gate_up, b_gate, g_gla_norm, w_out, ln1_g, ln1_b, w_router,
           router_bias, w_exp_gate, w_exp_up, w_exp_down, w_sh_gate, w_sh_up, w_sh_down, ln2_g, ln2_b):
    B, S, D = x.shape
    T = B * S
    x2 = x.reshape(T, D)
    l = 0
    row = lambda a: a.reshape(1, -1)

    off = np.cumsum((0,) + IN_SPLITS)
    seg = lambda i: w_in[l][:, off[i]:off[i + 1]]
    w_main = jnp.concatenate([seg(0) * (FOX_HEAD_DIM ** -0.5 * LOG2E), seg(1), seg(2), seg(4) * GLA_KEY_DIM ** -0.5, seg(5),
                              seg(6), seg(8)], axis=1).astype(BF16)
    n_small = FOX_HEADS + GLA_GATE_RANK
    w_small = jnp.concatenate([seg(3), seg(7), jnp.zeros((D, LANES - n_small), F32)], axis=1).astype(BF16)
    bf_pad = jnp.concatenate([b_fgate[l], jnp.zeros((LANES - FOX_HEADS,), F32)]).reshape(1, LANES)
    wg_pad = jnp.zeros((LANES, GLA_KEY_WIDTH), F32).at[FOX_HEADS:n_small].set(w_gate_up[l]).astype(BF16)

    h, fq, fk, fv, gq, gk, gv, gr, small = _proj(x2, row(ln_in_g), row(ln_in_b), w_main, w_small)

    y_fox = _fox(fq, fk, fv, _fgate(small, bf_pad, B, S), B, S)

    y_gla = _gla(gq, gk, gv, gr, small, wg_pad, row(b_gate[l]), row(g_gla_norm[l]),
                 _block_diag_tri(TS_GLA, CHUNK), B, S)

    h1, scores_t, *h1_parts = _oproj(y_fox, y_gla, h, w_out[l].astype(BF16), row(ln1_g[l]), row(ln1_b[l]),
                                     w_router[l].T.astype(BF16))

    tm = TM_ROUTE
    r = np.arange(tm)
    upper = jnp.asarray((r[:, None] < r[None, :]).astype(np.float32), BF16)
    ones = jnp.ones((tm, tm), BF16)
    idx_t, w_t, rank_t, cnt = _route(scores_t, router_bias[l].reshape(N_EXPERTS, 1), upper, ones)

    bm = BM_EXPERT
    A = T * TOP_K
    nb = A // bm + N_EXPERTS
    P = nb * bm
    counts = cnt[:, 0].astype(jnp.int32)
    padded = (counts + bm - 1) // bm * bm
    pend = jnp.cumsum(padded)
    pstart = pend - padded
    pos = _positions(idx_t, rank_t, pstart.astype(F32).reshape(N_EXPERTS, 1))
    nb_used = (pend[-1] // bm).astype(jnp.int32)
    blk_row0 = jnp.arange(nb, dtype=jnp.int32) * bm
    blk = jnp.minimum(jnp.sum(pend[None, :] <= blk_row0[:, None], axis=1), N_EXPERTS - 1).astype(jnp.int32)
    blk = jnp.where(jnp.arange(nb) < nb_used, blk, jnp.max(jnp.where(jnp.arange(nb) < nb_used, blk, 0)))

    xs_parts = _sc_scatter_rows(h1_parts, pos, P)
    ys_parts = _experts(blk, nb_used.reshape(1), xs_parts, w_exp_gate[l], w_exp_up[l], w_exp_down[l])
    yk_parts = [y.reshape(TOP_K, T, PART_WORDS) for y in _sc_gather_rows(ys_parts, pos.reshape(-1))]

    out = _final(h1, yk_parts, w_t.T, w_sh_gate[l].astype(BF16), w_sh_up[l].astype(BF16),
                 w_sh_down[l].astype(BF16), row(ln2_g[l]), row(ln2_b[l]))
    return out.reshape(B, S, D)
```

```python
import functools

import jax
import jax.numpy as jnp
import numpy as np
from jax import lax
from jax.experimental import pallas as pl
from jax.experimental.pallas import tpu as pltpu
from jax.experimental.pallas import tpu_sc as plsc

F32 = jnp.float32
BF16 = jnp.bfloat16

D_MODEL = 1024
CHUNK = 64
LN_EPS = 1e-5
LOG2E = 1.4426950408889634
FOX_HEADS = 8
FOX_HEAD_DIM = 64
FOX_WIDTH = FOX_HEADS * FOX_HEAD_DIM
GLA_HEADS = 4
GLA_KEY_DIM = 64
GLA_VALUE_DIM = 128
GLA_KEY_WIDTH = GLA_HEADS * GLA_KEY_DIM
GLA_WIDTH = GLA_HEADS * GLA_VALUE_DIM
GLA_GATE_RANK = 16
GLA_GATE_TEMP = 16.0
N_EXPERTS = 256
N_GROUPS = 8
GROUP_SIZE = N_EXPERTS // N_GROUPS
TOPK_GROUPS = 4
TOP_K = 8
EXPERT_DIM = 256
SHARED_DIM = 256
ROUTED_SCALE = 2.5
DEPTH = 1
DEEPNORM_ALPHA = (2.0 * DEPTH) ** 0.25
IN_SPLITS = (FOX_WIDTH, FOX_WIDTH, FOX_WIDTH, FOX_HEADS, GLA_KEY_WIDTH, GLA_KEY_WIDTH, GLA_WIDTH,
             GLA_GATE_RANK, GLA_WIDTH)

LANES = 128
VMEM_LIMIT = 48 * 1024 * 1024

TM_PROJ = 512
TQ_FOX = 512
TK_FOX = 512
FOX_GROUP = 4
TS_GLA = 512
TM_OPROJ = 512
TM_ROUTE = 512
BM_EXPERT = 256
TM_FINAL = 512

SC_CORES = 2
SC_SUBCORES = 16
SC_ROW_WINDOW = 128
ROW_PARTS = 2
PART_WORDS = D_MODEL // 2 // ROW_PARTS


def _cparams(*sem):
    return pltpu.CompilerParams(dimension_semantics=sem, vmem_limit_bytes=VMEM_LIMIT)


def _layer_norm(x, g, b):
    mu = jnp.mean(x, axis=-1, keepdims=True)
    xc = x - mu
    var = jnp.mean(xc * xc, axis=-1, keepdims=True)
    return xc * lax.rsqrt(var + LN_EPS) * g + b


def _log_sigmoid(z):
    return jnp.minimum(z, 0.0) - jnp.log1p(jnp.exp(-jnp.abs(z)))


def _sigmoid(z):
    return 1.0 / (1.0 + jnp.exp(-z))


def _dot(a, b):
    return jnp.dot(a, b, preferred_element_type=F32)


def _dot_nt(a, b):
    return lax.dot_general(a, b, (((1,), (1,)), ((), ())), preferred_element_type=F32)


def _dot_tn(a, b):
    return lax.dot_general(a, b, (((0,), (0,)), ((), ())), preferred_element_type=F32)


def _pack_bf16_pairs(x):
    n = x.shape[1] // 2
    u = lax.bitcast_convert_type(x.astype(BF16).astype(F32), jnp.uint32)
    return (u[:, :n] >> 16) | u[:, n:]


def _unpack_bf16_pairs(w):
    lo = lax.bitcast_convert_type(w << 16, F32)
    hi = lax.bitcast_convert_type(w & jnp.uint32(0xFFFF0000), F32)
    return lo, hi


def _store_row_parts(x, part_refs):
    packed = _pack_bf16_pairs(x)
    for p, ref in enumerate(part_refs):
        ref[...] = packed[:, p * PART_WORDS:(p + 1) * PART_WORDS]


def _load_row_parts(parts):
    out = []
    for p, w in enumerate(parts):
        lo, hi = _unpack_bf16_pairs(w)
        out.append((p * PART_WORDS, lo))
        out.append((D_MODEL // 2 + p * PART_WORDS, hi))
    return sorted(out, key=lambda t: t[0])


def _proj_kernel(x_ref, g_ref, b_ref, wm_ref, ws_ref,
                 h_ref, fq_ref, fk_ref, fv_ref, gq_ref, gk_ref, gv_ref, gr_ref, sm_ref):
    h = _layer_norm(x_ref[...], g_ref[...], b_ref[...])
    h_ref[...] = h
    hb = h.astype(BF16)
    off = 0
    for ref in (fq_ref, fk_ref, fv_ref, gq_ref, gk_ref, gv_ref, gr_ref):
        n = ref.shape[1]
        ref[...] = _dot(hb, wm_ref[:, off:off + n]).astype(ref.dtype)
        off += n
    sm_ref[...] = _dot(hb, ws_ref[...])


def _proj(x2, g, b, w_main, w_small):
    T, D = x2.shape
    tm = TM_PROJ
    widths = (FOX_WIDTH, FOX_WIDTH, FOX_WIDTH, GLA_KEY_WIDTH, GLA_KEY_WIDTH, GLA_WIDTH, GLA_WIDTH)
    row = lambda n: pl.BlockSpec((tm, n), lambda i: (i, 0))
    full = lambda a: pl.BlockSpec(a.shape, lambda i: (0,) * a.ndim)
    out_shape = [jax.ShapeDtypeStruct((T, D), F32)]
    out_shape += [jax.ShapeDtypeStruct((T, n), BF16) for n in widths]
    out_shape += [jax.ShapeDtypeStruct((T, LANES), F32)]
    out_specs = [row(D)] + [row(n) for n in widths] + [row(LANES)]
    return pl.pallas_call(
        _proj_kernel, grid=(T // tm,),
        in_specs=[row(D), full(g), full(b), full(w_main), full(w_small)],
        out_specs=out_specs, out_shape=out_shape,
        compiler_params=_cparams("parallel"), name="proj")(x2, g, b, w_main, w_small)


def _fgate_kernel(sm_ref, bf_ref, c_ref):
    S = sm_ref.shape[0]
    lf = _log_sigmoid(sm_ref[...] + bf_ref[...])
    rows = lax.broadcasted_iota(jnp.int32, lf.shape, 0)
    s = 1
    while s < S:
        lf = lf + jnp.where(rows >= s, pltpu.roll(lf, s, axis=0), 0.0)
        s *= 2
    c_ref[...] = lf * LOG2E


def _fgate(small, bf_pad, B, S):
    T = small.shape[0]
    return pl.pallas_call(
        _fgate_kernel, grid=(B,),
        in_specs=[pl.BlockSpec((S, LANES), lambda b: (b, 0)),
                  pl.BlockSpec((1, LANES), lambda b: (0, 0))],
        out_specs=pl.BlockSpec((S, LANES), lambda b: (b, 0)),
        out_shape=jax.ShapeDtypeStruct((T, LANES), F32),
        compiler_params=_cparams("parallel"), name="fgate")(small, bf_pad)


def _split3(x):
    hi = x.astype(BF16).astype(F32)
    r = x - hi
    mid = r.astype(BF16).astype(F32)
    return hi, mid, (r - mid).astype(BF16).astype(F32)


def _fox_operand(pair_ref_tile, odd, c_col, is_key):
    dh = FOX_HEAD_DIM
    x = pair_ref_tile.astype(F32)
    if odd:
        x = pltpu.roll(x, dh, axis=1)
    lane = lax.broadcasted_iota(jnp.int32, x.shape, 1)
    t0, t1, t2 = _split3(-c_col if is_key else c_col)
    first = dh + 3 if is_key else dh
    bias = jnp.where(lane == first, t0, jnp.where(lane == first + 1, t1, jnp.where(lane == first + 2, t2, 0.0)))
    ones_lo = dh if is_key else dh + 3
    bias = jnp.where((lane >= ones_lo) & (lane < ones_lo + 3), 1.0, bias)
    return jnp.where(lane < dh, x, bias).astype(BF16)


def _fox_kernel(q_ref, k_ref, v_ref, cq_ref, ck_ref, o_ref, ka_ref, vt_ref, *, tq, tk):
    i = pl.program_id(1)
    S = k_ref.shape[0]
    dh = FOX_HEAD_DIM

    @pl.when(i == 0)
    def _():
        def fill(r, carry):
            r0 = pl.multiple_of(r * tk, tk)
            for h in range(FOX_HEADS):
                pair = slice(LANES * (h // 2), LANES * (h // 2 + 1))
                ka_ref[pl.ds(r0, tk), LANES * h:LANES * (h + 1)] = _fox_operand(
                    k_ref[pl.ds(r0, tk), pair], h % 2, ck_ref[pl.ds(r0, tk), h:h + 1], True)
            vt_ref[r] = v_ref[pl.ds(r0, tk), :].astype(F32).T.astype(BF16)
            return carry
        lax.fori_loop(0, S // tk, fill, 0)

    key = lax.broadcasted_iota(jnp.int32, (tk, tq), 0)
    qry = lax.broadcasted_iota(jnp.int32, (tk, tq), 1)
    n_diag = tq // tk
    n_full = i * n_diag

    for p in range(FOX_HEADS // FOX_GROUP):
        group = slice(dh * FOX_GROUP * p, dh * FOX_GROUP * (p + 1))
        heads = tuple(range(FOX_GROUP * p, FOX_GROUP * (p + 1)))
        qa = [_fox_operand(q_ref[:, LANES * (h // 2):LANES * (h // 2 + 1)], h % 2, cq_ref[:, h:h + 1], False)
              for h in heads]

        def step(j, carry, diag):
            r0 = pl.multiple_of(j * tk, tk)
            ss = [_dot_nt(ka_ref[pl.ds(r0, tk), LANES * h:LANES * (h + 1)], qa[n])
                  for n, h in enumerate(heads)]
            if diag is not None:
                ss = [jnp.where(qry >= key + diag * tk, s, -jnp.inf) for s in ss]
            ms = [jnp.maximum(c[0], jnp.max(s, axis=0, keepdims=True)) for c, s in zip(carry, ss)]
            prs = [jnp.exp2(s - m) for s, m in zip(ss, ms)]
            alphas = [jnp.exp2(c[0] - m) for c, m in zip(carry, ms)]
            ls = [a * c[1] + jnp.sum(pr, axis=0, keepdims=True) for a, c, pr in zip(alphas, carry, prs)]
            pvs = [_dot(vt_ref[j, dh * h:dh * (h + 1), :], pr.astype(BF16)) for h, pr in zip(heads, prs)]
            accs = [a * c[2] + pv for a, c, pv in zip(alphas, carry, pvs)]
            return tuple(zip(ms, ls, accs))

        init = tuple((jnp.full((1, tq), -jnp.inf, F32), jnp.zeros((1, tq), F32), jnp.zeros((dh, tq), F32))
                     for _ in heads)
        carry = lax.fori_loop(0, n_full, functools.partial(step, diag=None), init)
        for d in range(n_diag):
            carry = step(n_full + d, carry, d)
        o_ref[:, group] = jnp.concatenate([(acc / l).T for _, l, acc in carry], axis=1).astype(o_ref.dtype)


def _fox(fq, fk, fv, c, B, S):
    T = fq.shape[0]
    tq, tk = TQ_FOX, TK_FOX
    nq = S // tq
    return pl.pallas_call(
        functools.partial(_fox_kernel, tq=tq, tk=tk), grid=(B, nq),
        in_specs=[pl.BlockSpec((tq, FOX_WIDTH), lambda b, i: (b * nq + i, 0)),
                  pl.BlockSpec((S, FOX_WIDTH), lambda b, i: (b, 0)),
                  pl.BlockSpec((S, FOX_WIDTH), lambda b, i: (b, 0)),
                  pl.BlockSpec((tq, LANES), lambda b, i: (b * nq + i, 0)),
                  pl.BlockSpec((S, LANES), lambda b, i: (b, 0))],
        out_specs=pl.BlockSpec((tq, FOX_WIDTH), lambda b, i: (b * nq + i, 0)),
        out_shape=jax.ShapeDtypeStruct((T, FOX_WIDTH), BF16),
        scratch_shapes=[pltpu.VMEM((S, FOX_HEADS * LANES), BF16), pltpu.VMEM((S // tk, FOX_WIDTH, tk), BF16)],
        compiler_params=_cparams("parallel", "arbitrary"), name="fox")(fq, fk, fv, c, c)


def _gla_kernel(q_ref, k_ref, v_ref, r_ref, sm_ref, wg_ref, bg_ref, gn_ref, tri_ref, y_ref,
                st_ref, b_ref, o_ref, *, ts):
    dk, dv, C = GLA_KEY_DIM, GLA_VALUE_DIM, CHUNK

    @pl.when(pl.program_id(1) == 0)
    def _():
        st_ref[...] = jnp.zeros_like(st_ref)

    z = _dot(sm_ref[...].astype(BF16), wg_ref[...]) + bg_ref[...]
    la = _log_sigmoid(z) * (1.0 / GLA_GATE_TEMP)
    hi = la.astype(BF16)
    r1 = la - hi.astype(F32)
    mid = r1.astype(BF16)
    lo = (r1 - mid.astype(F32)).astype(BF16)
    tri = tri_ref[...]
    b_ref[...] = _dot(tri, hi) + _dot(tri, mid) + _dot(tri, lo)

    row = lax.broadcasted_iota(jnp.int32, (C, C), 0)
    col = lax.broadcasted_iota(jnp.int32, (C, C), 1)
    trow = lax.broadcasted_iota(jnp.int32, (C, LANES), 0)
    lane = lax.broadcasted_iota(jnp.int32, (C, LANES), 1)
    head_lanes = [lane < dk, lane >= dk]

    def level_reference(b, m):
        if m >= 4:
            n = 2 * m
            return jnp.concatenate(
                [jnp.broadcast_to(b[s + m - 1:s + m, :], (n, LANES)) for s in range(0, C, n)], axis=0)
        up1 = pltpu.roll(b, 1, axis=0)
        if m == 1:
            return jnp.where(trow % 2 == 0, b, up1)
        up2 = pltpu.roll(b, 2, axis=0)
        down1 = pltpu.roll(b, C - 1, axis=0)
        tm = trow % 4
        return jnp.where(tm == 0, down1, jnp.where(tm == 1, b, jnp.where(tm == 2, up1, up2)))

    def chunk(c, carry):
        r0 = pl.multiple_of(c * C, C)
        for p in range(GLA_HEADS // 2):
            kl = slice(p * LANES, (p + 1) * LANES)
            q = q_ref[pl.ds(r0, C), kl].astype(F32)
            k = k_ref[pl.ds(r0, C), kl].astype(F32)
            b = b_ref[pl.ds(r0, C), kl]
            st = st_ref[p]
            stb = st.astype(BF16)
            b_last = b[C - 1:C, :]
            qb, kb = q.astype(BF16), k.astype(BF16)
            q_in = (q * jnp.exp(b)).astype(BF16)
            zero = jnp.zeros_like(qb)

            scores = [jnp.where(row == col, _dot_nt(jnp.where(hl, qb, zero), kb), 0.0) for hl in head_lanes]
            m = C // 2
            while m >= 1:
                ref = level_reference(b, m)
                upper = (trow % (2 * m)) >= m
                qt = jnp.where(upper, q * jnp.exp(jnp.minimum(b - ref, 0.0)), 0.0).astype(BF16)
                kt = jnp.where(upper, 0.0, k * jnp.exp(jnp.minimum(ref - b, 0.0))).astype(BF16)
                same_block = (row // (2 * m)) == (col // (2 * m))
                scores = [sc + jnp.where(same_block, _dot_nt(jnp.where(hl, qt, zero), kt), 0.0)
                          for sc, hl in zip(scores, head_lanes)]
                m //= 2

            kh = (k * jnp.exp(b_last - b)).astype(BF16)
            upd = []
            for n, hl in enumerate(head_lanes):
                h = 2 * p + n
                vl = slice(h * dv, (h + 1) * dv)
                v = v_ref[pl.ds(r0, C), vl]
                o_ref[pl.ds(r0, C), vl] = (_dot_nt(jnp.where(hl, q_in, zero), stb)
                                           + _dot(scores[n].astype(BF16), v))
                upd.append(_dot_tn(v, kh))
            st_ref[p] = st * jnp.exp(b_last) + jnp.where(lax.broadcasted_iota(jnp.int32, st.shape, 1) < dk,
                                                         upd[0], upd[1])
        return carry

    lax.fori_loop(0, ts // C, chunk, 0)

    for h in range(GLA_HEADS):
        vl = slice(h * dv, (h + 1) * dv)
        o = o_ref[:, vl]
        ms = jnp.mean(o * o, axis=-1, keepdims=True)
        g = r_ref[:, vl].astype(F32)
        y = o * lax.rsqrt(ms + LN_EPS) * gn_ref[...] * (g * _sigmoid(g))
        y_ref[:, vl] = y.astype(y_ref.dtype)


def _gla(gq, gk, gv, gr, small, wg_pad, bg, gnorm, tri, B, S):
    T = gq.shape[0]
    ts = TS_GLA
    ns = S // ts
    row = lambda n: pl.BlockSpec((ts, n), lambda b, i: (b * ns + i, 0))
    full = lambda a: pl.BlockSpec(a.shape, lambda b, i: (0,) * a.ndim)
    return pl.pallas_call(
        functools.partial(_gla_kernel, ts=ts), grid=(B, ns),
        in_specs=[row(GLA_KEY_WIDTH), row(GLA_KEY_WIDTH), row(GLA_WIDTH), row(GLA_WIDTH), row(LANES),
                  full(wg_pad), full(bg), full(gnorm), full(tri)],
        out_specs=row(GLA_WIDTH),
        out_shape=jax.ShapeDtypeStruct((T, GLA_WIDTH), BF16),
        scratch_shapes=[pltpu.VMEM((GLA_HEADS // 2, GLA_VALUE_DIM, LANES), F32),
                        pltpu.VMEM((ts, GLA_KEY_WIDTH), F32),
                        pltpu.VMEM((ts, GLA_WIDTH), F32)],
        compiler_params=_cparams("parallel", "arbitrary"), name="gla")(
            gq, gk, gv, gr, small, wg_pad, bg, gnorm, tri)


def _oproj_kernel(yf_ref, yg_ref, h_ref, wo_ref, g_ref, b_ref, wr_ref, h1_ref, st_ref, *part_refs):
    mix = _dot(yf_ref[...], wo_ref[0:FOX_WIDTH, :]) + _dot(yg_ref[...], wo_ref[FOX_WIDTH:, :])
    h1 = _layer_norm(DEEPNORM_ALPHA * h_ref[...] + mix, g_ref[...], b_ref[...])
    h1_ref[...] = h1
    _store_row_parts(h1, part_refs)
    st_ref[...] = _sigmoid(_dot_nt(wr_ref[...], h1.astype(BF16)))


def _oproj(y_fox, y_gla, h, w_out, g, b, w_router_t):
    T, D = h.shape
    tm = TM_OPROJ
    row = lambda n: pl.BlockSpec((tm, n), lambda i: (i, 0))
    full = lambda a: pl.BlockSpec(a.shape, lambda i: (0,) * a.ndim)
    return pl.pallas_call(
        _oproj_kernel, grid=(T // tm,),
        in_specs=[row(FOX_WIDTH), row(GLA_WIDTH), row(D), full(w_out), full(g), full(b), full(w_router_t)],
        out_specs=[row(D), pl.BlockSpec((N_EXPERTS, tm), lambda i: (0, i))] + [row(PART_WORDS)] * ROW_PARTS,
        out_shape=[jax.ShapeDtypeStruct((T, D), F32), jax.ShapeDtypeStruct((N_EXPERTS, T), F32)]
        + [jax.ShapeDtypeStruct((T, PART_WORDS), jnp.uint32)] * ROW_PARTS,
        compiler_params=_cparams("parallel"), name="oproj")(y_fox, y_gla, h, w_out, g, b, w_router_t)


def _route_kernel(s_ref, bias_ref, upper_ref, ones_ref, idx_ref, w_ref, rank_ref, cnt_ref, carry_ref):
    E, tm = s_ref.shape

    @pl.when(pl.program_id(0) == 0)
    def _():
        carry_ref[...] = jnp.zeros_like(carry_ref)

    s = s_ref[...]
    biased = s + bias_ref[...]
    neg = -jnp.inf
    erow = lax.broadcasted_iota(jnp.int32, (E, tm), 0).astype(F32)
    grow = lax.broadcasted_iota(jnp.int32, (GROUP_SIZE, tm), 0).astype(F32)

    gs = []
    for g in range(N_GROUPS):
        blk = biased[g * GROUP_SIZE:(g + 1) * GROUP_SIZE, :]
        m1 = jnp.max(blk, axis=0, keepdims=True)
        i1 = jnp.min(jnp.where(blk == m1, grow, float(GROUP_SIZE)), axis=0, keepdims=True)
        m2 = jnp.max(jnp.where(grow == i1, neg, blk), axis=0, keepdims=True)
        gs.append(m1 + m2)
    keep = []
    for g in range(N_GROUPS):
        beaten = jnp.zeros((1, tm), F32)
        for o in range(N_GROUPS):
            if o == g:
                continue
            wins = (gs[o] > gs[g]) | ((gs[o] == gs[g]) & (o < g))
            beaten = beaten + jnp.where(wins, 1.0, 0.0)
        keep.append(jnp.broadcast_to(beaten < float(TOPK_GROUPS), (GROUP_SIZE, tm)))
    cur = jnp.where(jnp.concatenate(keep, axis=0), biased, neg)

    ids, ws, hots = [], [], []
    chosen = jnp.zeros((E, tm), F32)
    for _ in range(TOP_K):
        m = jnp.max(cur, axis=0, keepdims=True)
        ik = jnp.min(jnp.where(cur == m, erow, float(E)), axis=0, keepdims=True)
        hot = erow == ik
        ws.append(jnp.sum(jnp.where(hot, s, 0.0), axis=0, keepdims=True))
        cur = jnp.where(hot, neg, cur)
        chosen = jnp.where(hot, 1.0, chosen)
        ids.append(ik)
        hots.append(hot)
    wsum = ws[0]
    for w in ws[1:]:
        wsum = wsum + w

    chosen_b = chosen.astype(BF16)
    before = carry_ref[...] + _dot(chosen_b, upper_ref[...])
    carry_ref[...] = carry_ref[...] + _dot(chosen_b, ones_ref[...])
    for k in range(TOP_K):
        idx_ref[k:k + 1, :] = ids[k].astype(jnp.int32)
        w_ref[k:k + 1, :] = ws[k] / wsum * ROUTED_SCALE
        rank_ref[k:k + 1, :] = jnp.sum(jnp.where(hots[k], before, 0.0), axis=0, keepdims=True).astype(jnp.int32)
    cnt_ref[...] = carry_ref[:, 0:LANES]


def _route(scores_t, bias_col, upper, ones):
    E, T = scores_t.shape
    tm = TM_ROUTE
    full = lambda a: pl.BlockSpec(a.shape, lambda i: (0,) * a.ndim)
    kt = pl.BlockSpec((TOP_K, tm), lambda i: (0, i))
    return pl.pallas_call(
        _route_kernel, grid=(T // tm,),
        in_specs=[pl.BlockSpec((E, tm), lambda i: (0, i)), full(bias_col), full(upper), full(ones)],
        out_specs=[kt, kt, kt, pl.BlockSpec((E, LANES), lambda i: (0, 0))],
        out_shape=[jax.ShapeDtypeStruct((TOP_K, T), jnp.int32), jax.ShapeDtypeStruct((TOP_K, T), F32),
                   jax.ShapeDtypeStruct((TOP_K, T), jnp.int32), jax.ShapeDtypeStruct((E, LANES), F32)],
        scratch_shapes=[pltpu.VMEM((E, tm), F32)],
        compiler_params=_cparams("arbitrary"), name="route")(scores_t, bias_col, upper, ones)


def _expert_kernel(blk_ref, nb_ref, *refs):
    x_refs = refs[:ROW_PARTS]
    wg_ref, wu_ref, wd_ref = refs[ROW_PARTS:ROW_PARTS + 3]
    y_refs = refs[ROW_PARTS + 3:]

    @pl.when(pl.program_id(0) < nb_ref[0])
    def _():
        g = u = None
        for c0, xc in _load_row_parts([r[...] for r in x_refs]):
            xb = xc.astype(BF16)
            rows = slice(c0, c0 + PART_WORDS)
            gc = _dot(xb, wg_ref[rows, :].astype(BF16))
            uc = _dot(xb, wu_ref[rows, :].astype(BF16))
            g = gc if g is None else g + gc
            u = uc if u is None else u + uc
        a = (g * _sigmoid(g) * u).astype(BF16)
        _store_row_parts(_dot(a, wd_ref[...].astype(BF16)), y_refs)


def _experts(blk_exp, nb_used, xs_parts, w_gate, w_up, w_down):
    P = xs_parts[0].shape[0]
    D = D_MODEL
    bm = BM_EXPERT
    nb = P // bm
    part = pl.BlockSpec((bm, PART_WORDS), lambda i, be, nu: (i, 0))
    grid_spec = pltpu.PrefetchScalarGridSpec(
        num_scalar_prefetch=2, grid=(nb,),
        in_specs=[part] * ROW_PARTS + [
            pl.BlockSpec((None, D, EXPERT_DIM), lambda i, be, nu: (be[i], 0, 0)),
            pl.BlockSpec((None, D, EXPERT_DIM), lambda i, be, nu: (be[i], 0, 0)),
            pl.BlockSpec((None, EXPERT_DIM, D), lambda i, be, nu: (be[i], 0, 0))],
        out_specs=[part] * ROW_PARTS)
    return pl.pallas_call(
        _expert_kernel, grid_spec=grid_spec,
        out_shape=[jax.ShapeDtypeStruct((P, PART_WORDS), jnp.uint32)] * ROW_PARTS,
        compiler_params=_cparams("arbitrary"), name="experts")(blk_exp, nb_used, *xs_parts, w_gate, w_up, w_down)


def _pos_kernel(idx_ref, rank_ref, start_ref, pos_ref):
    K, tm = idx_ref.shape
    E = start_ref.shape[0]
    erow = lax.broadcasted_iota(jnp.int32, (E, tm), 0)
    start = start_ref[...]
    for k in range(K):
        first = jnp.sum(jnp.where(erow == idx_ref[k:k + 1, :], start, 0.0), axis=0, keepdims=True)
        pos_ref[k:k + 1, :] = first.astype(jnp.int32) + rank_ref[k:k + 1, :]


def _positions(idx_t, rank_t, start_col):
    K, T = idx_t.shape
    tm = TM_ROUTE
    kt = pl.BlockSpec((K, tm), lambda i: (0, i))
    return pl.pallas_call(
        _pos_kernel, grid=(T // tm,),
        in_specs=[kt, kt, pl.BlockSpec(start_col.shape, lambda i: (0, 0))],
        out_specs=kt, out_shape=jax.ShapeDtypeStruct((K, T), jnp.int32),
        compiler_params=_cparams("parallel"), name="positions")(idx_t, rank_t, start_col)


def _sc_mesh():
    return plsc.VectorSubcoreMesh(core_axis_name="core", subcore_axis_name="subcore",
                                  num_cores=SC_CORES, num_subcores=SC_SUBCORES)


def _sc_scatter_rows(parts, pos, n_out):
    T, W = parts[0].shape
    K = pos.shape[0]
    n = len(parts)
    win = SC_ROW_WINDOW

    @functools.partial(pl.kernel, out_type=[jax.ShapeDtypeStruct((n_out, W), parts[0].dtype)] * n, mesh=_sc_mesh(),
                       name="sc_dispatch")
    def k(*refs):
        x_hbms, p_hbm, o_hbms = refs[:n], refs[n], refs[n + 1:]
        for x_hbm, o_hbm in zip(x_hbms, o_hbms):
            def body(x_vmem, p_vmem, o_hbm=o_hbm):
                for j in range(K):
                    pltpu.sync_copy(x_vmem, o_hbm.at[p_vmem.at[j]])

            pltpu.emit_pipeline(
                body, grid=(T // win,),
                in_specs=[pl.BlockSpec((win, W), lambda i: (i, 0)), pl.BlockSpec((K, win), lambda i: (0, i))],
                out_specs=[], core_axis_name=("core", "subcore"),
                dimension_semantics=(pltpu.PARALLEL,))(x_hbm, p_hbm)

    return k(*parts, pos)


def _sc_gather_rows(tables, idx):
    M = idx.shape[0]
    W = tables[0].shape[1]
    n = len(tables)
    win = SC_ROW_WINDOW

    @functools.partial(pl.kernel, out_type=[jax.ShapeDtypeStruct((M, W), tables[0].dtype)] * n, mesh=_sc_mesh(),
                       name="sc_combine")
    def k(*refs):
        t_hbms, i_hbm, o_hbms = refs[:n], refs[n], refs[n + 1:]
        for t_hbm, o_hbm in zip(t_hbms, o_hbms):
            def body(i_vmem, o_vmem, t_hbm=t_hbm):
                pltpu.sync_copy(t_hbm.at[i_vmem.at[0]], o_vmem)

            pltpu.emit_pipeline(
                body, grid=(M // win,),
                in_specs=[pl.BlockSpec((1, win), lambda i: (0, i))],
                out_specs=[pl.BlockSpec((win, W), lambda i: (i, 0))],
                core_axis_name=("core", "subcore"), dimension_semantics=(pltpu.PARALLEL,))(i_hbm, o_hbm)

    return k(*tables, idx.reshape(1, M))


def _final_kernel(h1_ref, wk_ref, wg_ref, wu_ref, wd_ref, g_ref, b_ref, *refs):
    yk_refs, o_ref = refs[:ROW_PARTS], refs[ROW_PARTS]
    h1 = h1_ref[...]
    hb = h1.astype(BF16)
    g = _dot(hb, wg_ref[...])
    u = _dot(hb, wu_ref[...])
    ffn = _dot((g * _sigmoid(g) * u).astype(BF16), wd_ref[...])
    wk = wk_ref[...]
    sums = None
    for k in range(TOP_K):
        cols = _load_row_parts([r[k] for r in yk_refs])
        terms = [c * wk[:, k:k + 1] for _, c in cols]
        sums = terms if sums is None else [s + t for s, t in zip(sums, terms)]
    ffn = ffn + jnp.concatenate(sums, axis=1)
    o_ref[...] = _layer_norm(DEEPNORM_ALPHA * h1 + ffn, g_ref[...], b_ref[...])


def _final(h1, yk_parts, wk, w_sg, w_su, w_sd, g, b):
    T, D = h1.shape
    tm = TM_FINAL
    row = lambda n: pl.BlockSpec((tm, n), lambda i: (i, 0))
    full = lambda a: pl.BlockSpec(a.shape, lambda i: (0,) * a.ndim)
    return pl.pallas_call(
        _final_kernel, grid=(T // tm,),
        in_specs=[row(D), row(TOP_K), full(w_sg), full(w_su), full(w_sd), full(g), full(b)]
        + [pl.BlockSpec((TOP_K, tm, PART_WORDS), lambda i: (0, i, 0))] * ROW_PARTS,
        out_specs=row(D), out_shape=jax.ShapeDtypeStruct((T, D), F32),
        compiler_params=_cparams("parallel"), name="final")(h1, wk, w_sg, w_su, w_sd, g, b, *yk_parts)


def _block_diag_tri(n, c):
    r = np.arange(n)
    return jnp.asarray(((r[:, None] >= r[None, :]) & (r[:, None] // c == r[None, :] // c)).astype(np.float32), BF16)


def kernel(x, ln_in_g, ln_in_b, w_in, b_fgate, w_gate_up, b_gate, g_gla_norm, w_out, ln1_g, ln1_b, w_router,
           router_bias, w_exp_gate, w_exp_up, w_exp_down, w_sh_gate, w_sh_up, w_sh_down, ln2_g, ln2_b):
    B, S, D = x.shape
    T = B * S
    x2 = x.reshape(T, D)
    l = 0
    row = lambda a: a.reshape(1, -1)

    off = np.cumsum((0,) + IN_SPLITS)
    seg = lambda i: w_in[l][:, off[i]:off[i + 1]]
    w_main = jnp.concatenate([seg(0) * (FOX_HEAD_DIM ** -0.5 * LOG2E), seg(1), seg(2), seg(4) * GLA_KEY_DIM ** -0.5, seg(5),
                              seg(6), seg(8)], axis=1).astype(BF16)
    n_small = FOX_HEADS + GLA_GATE_RANK
    w_small = jnp.concatenate([seg(3), seg(7), jnp.zeros((D, LANES - n_small), F32)], axis=1).astype(BF16)
    bf_pad = jnp.concatenate([b_fgate[l], jnp.zeros((LANES - FOX_HEADS,), F32)]).reshape(1, LANES)
    wg_pad = jnp.zeros((LANES, GLA_KEY_WIDTH), F32).at[FOX_HEADS:n_small].set(w_gate_up[l]).astype(BF16)

    h, fq, fk, fv, gq, gk, gv, gr, small = _proj(x2, row(ln_in_g), row(ln_in_b), w_main, w_small)

    y_fox = _fox(fq, fk, fv, _fgate(small, bf_pad, B, S), B, S)

    y_gla = _gla(gq, gk, gv, gr, small, wg_pad, row(b_gate[l]), row(g_gla_norm[l]),
                 _block_diag_tri(TS_GLA, CHUNK), B, S)

    h1, scores_t, *h1_parts = _oproj(y_fox, y_gla, h, w_out[l].astype(BF16), row(ln1_g[l]), row(ln1_b[l]),
                                     w_router[l].T.astype(BF16))

    tm = TM_ROUTE
    r = np.arange(tm)
    upper = jnp.asarray((r[:, None] < r[None, :]).astype(np.float32), BF16)
    ones = jnp.ones((tm, tm), BF16)
    idx_t, w_t, rank_t, cnt = _route(scores_t, router_bias[l].reshape(N_EXPERTS, 1), upper, ones)

    bm = BM_EXPERT
    A = T * TOP_K
    nb = A // bm + N_EXPERTS
    P = nb * bm
    counts = cnt[:, 0].astype(jnp.int32)
    padded = (counts + bm - 1) // bm * bm
    pend = jnp.cumsum(padded)
    pstart = pend - padded
    pos = _positions(idx_t, rank_t, pstart.astype(F32).reshape(N_EXPERTS, 1))
    nb_used = (pend[-1] // bm).astype(jnp.int32)
    blk_row0 = jnp.arange(nb, dtype=jnp.int32) * bm
    blk = jnp.minimum(jnp.sum(pend[None, :] <= blk_row0[:, None], axis=1), N_EXPERTS - 1).astype(jnp.int32)
    blk = jnp.where(jnp.arange(nb) < nb_used, blk, jnp.max(jnp.where(jnp.arange(nb) < nb_used, blk, 0)))

    xs_parts = _sc_scatter_rows(h1_parts, pos, P)
    ys_parts = _experts(blk, nb_used.reshape(1), xs_parts, w_exp_gate[l], w_exp_up[l], w_exp_down[l])
    yk_parts = [y.reshape(TOP_K, T, PART_WORDS) for y in _sc_gather_rows(ys_parts, pos.reshape(-1))]

    out = _final(h1, yk_parts, w_t.T, w_sh_gate[l].astype(BF16), w_sh_up[l].astype(BF16),
                 w_sh_down[l].astype(BF16), row(ln2_g[l]), row(ln2_b[l]))
    return out.reshape(B, S, D)
```

```python
import functools

import jax
import jax.numpy as jnp
import numpy as np
from jax import lax
from jax.experimental import pallas as pl
from jax.experimental.pallas import tpu as pltpu
from jax.experimental.pallas import tpu_sc as plsc

F32 = jnp.float32
BF16 = jnp.bfloat16

D_MODEL = 1024
CHUNK = 64
LN_EPS = 1e-5
LOG2E = 1.4426950408889634
FOX_HEADS = 8
FOX_HEAD_DIM = 64
FOX_WIDTH = FOX_HEADS * FOX_HEAD_DIM
GLA_HEADS = 4
GLA_KEY_DIM = 64
GLA_VALUE_DIM = 128
GLA_KEY_WIDTH = GLA_HEADS * GLA_KEY_DIM
GLA_WIDTH = GLA_HEADS * GLA_VALUE_DIM
GLA_GATE_RANK = 16
GLA_GATE_TEMP = 16.0
N_EXPERTS = 256
N_GROUPS = 8
GROUP_SIZE = N_EXPERTS // N_GROUPS
TOPK_GROUPS = 4
TOP_K = 8
EXPERT_DIM = 256
SHARED_DIM = 256
ROUTED_SCALE = 2.5
DEPTH = 1
DEEPNORM_ALPHA = (2.0 * DEPTH) ** 0.25
IN_SPLITS = (FOX_WIDTH, FOX_WIDTH, FOX_WIDTH, FOX_HEADS, GLA_KEY_WIDTH, GLA_KEY_WIDTH, GLA_WIDTH,
             GLA_GATE_RANK, GLA_WIDTH)

LANES = 128
VMEM_LIMIT = 48 * 1024 * 1024

TM_PROJ = 512
TQ_FOX = 512
TK_FOX = 512
FOX_GROUP = 4
TS_GLA = 512
TM_OPROJ = 512
TM_ROUTE = 512
BM_EXPERT = 256
TM_FINAL = 512

SC_CORES = 2
SC_SUBCORES = 16
SC_ROW_WINDOW = 128
ROW_PARTS = 2
PART_WORDS = D_MODEL // 2 // ROW_PARTS


def _cparams(*sem):
    return pltpu.CompilerParams(dimension_semantics=sem, vmem_limit_bytes=VMEM_LIMIT)


def _layer_norm(x, g, b):
    mu = jnp.mean(x, axis=-1, keepdims=True)
    xc = x - mu
    var = jnp.mean(xc * xc, axis=-1, keepdims=True)
    return xc * lax.rsqrt(var + LN_EPS) * g + b


def _log_sigmoid(z):
    return jnp.minimum(z, 0.0) - jnp.log1p(jnp.exp(-jnp.abs(z)))


def _sigmoid(z):
    return 1.0 / (1.0 + jnp.exp(-z))


def _dot(a, b):
    return jnp.dot(a, b, preferred_element_type=F32)


def _dot_nt(a, b):
    return lax.dot_general(a, b, (((1,), (1,)), ((), ())), preferred_element_type=F32)


def _dot_tn(a, b):
    return lax.dot_general(a, b, (((0,), (0,)), ((), ())), preferred_element_type=F32)


def _pack_bf16_pairs(x):
    n = x.shape[1] // 2
    u = lax.bitcast_convert_type(x.astype(BF16).astype(F32), jnp.uint32)
    return (u[:, :n] >> 16) | u[:, n:]


def _unpack_bf16_pairs(w):
    lo = lax.bitcast_convert_type(w << 16, F32)
    hi = lax.bitcast_convert_type(w & jnp.uint32(0xFFFF0000), F32)
    return lo, hi


def _store_row_parts(x, part_refs):
    packed = _pack_bf16_pairs(x)
    for p, ref in enumerate(part_refs):
        ref[...] = packed[:, p * PART_WORDS:(p + 1) * PART_WORDS]


def _load_row_parts(parts):
    out = []
    for p, w in enumerate(parts):
        lo, hi = _unpack_bf16_pairs(w)
        out.append((p * PART_WORDS, lo))
        out.append((D_MODEL // 2 + p * PART_WORDS, hi))
    return sorted(out, key=lambda t: t[0])


def _proj_kernel(x_ref, g_ref, b_ref, wm_ref, ws_ref,
                 h_ref, fq_ref, fk_ref, fv_ref, gq_ref, gk_ref, gv_ref, gr_ref, sm_ref):
    h = _layer_norm(x_ref[...], g_ref[...], b_ref[...])
    h_ref[...] = h
    hb = h.astype(BF16)
    off = 0
    for ref in (fq_ref, fk_ref, fv_ref, gq_ref, gk_ref, gv_ref, gr_ref):
        n = ref.shape[1]
        ref[...] = _dot(hb, wm_ref[:, off:off + n]).astype(ref.dtype)
        off += n
    sm_ref[...] = _dot(hb, ws_ref[...])


def _proj(x2, g, b, w_main, w_small):
    T, D = x2.shape
    tm = TM_PROJ
    widths = (FOX_WIDTH, FOX_WIDTH, FOX_WIDTH, GLA_KEY_WIDTH, GLA_KEY_WIDTH, GLA_WIDTH, GLA_WIDTH)
    row = lambda n: pl.BlockSpec((tm, n), lambda i: (i, 0))
    full = lambda a: pl.BlockSpec(a.shape, lambda i: (0,) * a.ndim)
    out_shape = [jax.ShapeDtypeStruct((T, D), F32)]
    out_shape += [jax.ShapeDtypeStruct((T, n), BF16) for n in widths]
    out_shape += [jax.ShapeDtypeStruct((T, LANES), F32)]
    out_specs = [row(D)] + [row(n) for n in widths] + [row(LANES)]
    return pl.pallas_call(
        _proj_kernel, grid=(T // tm,),
        in_specs=[row(D), full(g), full(b), full(w_main), full(w_small)],
        out_specs=out_specs, out_shape=out_shape,
        compiler_params=_cparams("parallel"), name="proj")(x2, g, b, w_main, w_small)


def _fgate_kernel(sm_ref, bf_ref, c_ref):
    S = sm_ref.shape[0]
    lf = _log_sigmoid(sm_ref[...] + bf_ref[...])
    rows = lax.broadcasted_iota(jnp.int32, lf.shape, 0)
    s = 1
    while s < S:
        lf = lf + jnp.where(rows >= s, pltpu.roll(lf, s, axis=0), 0.0)
        s *= 2
    c_ref[...] = lf * LOG2E


def _fgate(small, bf_pad, B, S):
    T = small.shape[0]
    return pl.pallas_call(
        _fgate_kernel, grid=(B,),
        in_specs=[pl.BlockSpec((S, LANES), lambda b: (b, 0)),
                  pl.BlockSpec((1, LANES), lambda b: (0, 0))],
        out_specs=pl.BlockSpec((S, LANES), lambda b: (b, 0)),
        out_shape=jax.ShapeDtypeStruct((T, LANES), F32),
        compiler_params=_cparams("parallel"), name="fgate")(small, bf_pad)


def _split3(x):
    hi = x.astype(BF16).astype(F32)
    r = x - hi
    mid = r.astype(BF16).astype(F32)
    return hi, mid, (r - mid).astype(BF16).astype(F32)


def _fox_operand(pair_ref_tile, odd, c_col, is_key):
    dh = FOX_HEAD_DIM
    x = pair_ref_tile.astype(F32)
    if odd:
        x = pltpu.roll(x, dh, axis=1)
    lane = lax.broadcasted_iota(jnp.int32, x.shape, 1)
    t0, t1, t2 = _split3(-c_col if is_key else c_col)
    first = dh + 3 if is_key else dh
    bias = jnp.where(lane == first, t0, jnp.where(lane == first + 1, t1, jnp.where(lane == first + 2, t2, 0.0)))
    ones_lo = dh if is_key else dh + 3
    bias = jnp.where((lane >= ones_lo) & (lane < ones_lo + 3), 1.0, bias)
    return jnp.where(lane < dh, x, bias).astype(BF16)


def _fox_kernel(q_ref, k_ref, v_ref, cq_ref, ck_ref, o_ref, ka_ref, vt_ref, *, tq, tk):
    i = pl.program_id(1)
    S = k_ref.shape[0]
    dh = FOX_HEAD_DIM

    @pl.when(i == 0)
    def _():
        def fill(r, carry):
            r0 = pl.multiple_of(r * tk, tk)
            for h in range(FOX_HEADS):
                pair = slice(LANES * (h // 2), LANES * (h // 2 + 1))
                ka_ref[pl.ds(r0, tk), LANES * h:LANES * (h + 1)] = _fox_operand(
                    k_ref[pl.ds(r0, tk), pair], h % 2, ck_ref[pl.ds(r0, tk), h:h + 1], True)
            vt_ref[r] = v_ref[pl.ds(r0, tk), :].astype(F32).T.astype(BF16)
            return carry
        lax.fori_loop(0, S // tk, fill, 0)

    key = lax.broadcasted_iota(jnp.int32, (tk, tq), 0)
    qry = lax.broadcasted_iota(jnp.int32, (tk, tq), 1)
    n_diag = tq // tk
    n_full = i * n_diag

    for p in range(FOX_HEADS // FOX_GROUP):
        group = slice(dh * FOX_GROUP * p, dh * FOX_GROUP * (p + 1))
        heads = tuple(range(FOX_GROUP * p, FOX_GROUP * (p + 1)))
        qa = [_fox_operand(q_ref[:, LANES * (h // 2):LANES * (h // 2 + 1)], h % 2, cq_ref[:, h:h + 1], False)
              for h in heads]

        def step(j, carry, diag):
            r0 = pl.multiple_of(j * tk, tk)
            ss = [_dot_nt(ka_ref[pl.ds(r0, tk), LANES * h:LANES * (h + 1)], qa[n])
                  for n, h in enumerate(heads)]
            if diag is not None:
                ss = [jnp.where(qry >= key + diag * tk, s, -jnp.inf) for s in ss]
            ms = [jnp.maximum(c[0], jnp.max(s, axis=0, keepdims=True)) for c, s in zip(carry, ss)]
            prs = [jnp.exp2(s - m) for s, m in zip(ss, ms)]
            alphas = [jnp.exp2(c[0] - m) for c, m in zip(carry, ms)]
            ls = [a * c[1] + jnp.sum(pr, axis=0, keepdims=True) for a, c, pr in zip(alphas, carry, prs)]
            pvs = [_dot(vt_ref[j, dh * h:dh * (h + 1), :], pr.astype(BF16)) for h, pr in zip(heads, prs)]
            accs = [a * c[2] + pv for a, c, pv in zip(alphas, carry, pvs)]
            return tuple(zip(ms, ls, accs))

        init = tuple((jnp.full((1, tq), -jnp.inf, F32), jnp.zeros((1, tq), F32), jnp.zeros((dh, tq), F32))
                     for _ in heads)
        carry = lax.fori_loop(0, n_full, functools.partial(step, diag=None), init)
        for d in range(n_diag):
            carry = step(n_full + d, carry, d)
        o_ref[:, group] = jnp.concatenate([(acc / l).T for _, l, acc in carry], axis=1).astype(o_ref.dtype)


def _fox(fq, fk, fv, c, B, S):
    T = fq.shape[0]
    tq, tk = TQ_FOX, TK_FOX
    nq = S // tq
    return pl.pallas_call(
        functools.partial(_fox_kernel, tq=tq, tk=tk), grid=(B, nq),
        in_specs=[pl.BlockSpec((tq, FOX_WIDTH), lambda b, i: (b * nq + i, 0)),
                  pl.BlockSpec((S, FOX_WIDTH), lambda b, i: (b, 0)),
                  pl.BlockSpec((S, FOX_WIDTH), lambda b, i: (b, 0)),
                  pl.BlockSpec((tq, LANES), lambda b, i: (b * nq + i, 0)),
                  pl.BlockSpec((S, LANES), lambda b, i: (b, 0))],
        out_specs=pl.BlockSpec((tq, FOX_WIDTH), lambda b, i: (b * nq + i, 0)),
        out_shape=jax.ShapeDtypeStruct((T, FOX_WIDTH), BF16),
        scratch_shapes=[pltpu.VMEM((S, FOX_HEADS * LANES), BF16), pltpu.VMEM((S // tk, FOX_WIDTH, tk), BF16)],
        compiler_params=_cparams("parallel", "arbitrary"), name="fox")(fq, fk, fv, c, c)


def _gla_kernel(q_ref, k_ref, v_ref, r_ref, sm_ref, wg_ref, bg_ref, gn_ref, tri_ref, y_ref,
                st_ref, b_ref, o_ref, *, ts):
    dk, dv, C = GLA_KEY_DIM, GLA_VALUE_DIM, CHUNK

    @pl.when(pl.program_id(1) == 0)
    def _():
        st_ref[...] = jnp.zeros_like(st_ref)

    z = _dot(sm_ref[...].astype(BF16), wg_ref[...]) + bg_ref[...]
    la = _log_sigmoid(z) * (1.0 / GLA_GATE_TEMP)
    hi = la.astype(BF16)
    r1 = la - hi.astype(F32)
    mid = r1.astype(BF16)
    lo = (r1 - mid.astype(F32)).astype(BF16)
    tri = tri_ref[...]
    b_ref[...] = _dot(tri, hi) + _dot(tri, mid) + _dot(tri, lo)

    row = lax.broadcasted_iota(jnp.int32, (C, C), 0)
    col = lax.broadcasted_iota(jnp.int32, (C, C), 1)
    trow = lax.broadcasted_iota(jnp.int32, (C, LANES), 0)
    lane = lax.broadcasted_iota(jnp.int32, (C, LANES), 1)
    head_lanes = [lane < dk, lane >= dk]

    def level_reference(b, m):
        if m >= 4:
            n = 2 * m
            return jnp.concatenate(
                [jnp.broadcast_to(b[s + m - 1:s + m, :], (n, LANES)) for s in range(0, C, n)], axis=0)
        up1 = pltpu.roll(b, 1, axis=0)
        if m == 1:
            return jnp.where(trow % 2 == 0, b, up1)
        up2 = pltpu.roll(b, 2, axis=0)
        down1 = pltpu.roll(b, C - 1, axis=0)
        tm = trow % 4
        return jnp.where(tm == 0, down1, jnp.where(tm == 1, b, jnp.where(tm == 2, up1, up2)))

    def chunk(c, carry):
        r0 = pl.multiple_of(c * C, C)
        for p in range(GLA_HEADS // 2):
            kl = slice(p * LANES, (p + 1) * LANES)
            q = q_ref[pl.ds(r0, C), kl].astype(F32)
            k = k_ref[pl.ds(r0, C), kl].astype(F32)
            b = b_ref[pl.ds(r0, C), kl]
            st = st_ref[p]
            stb = st.astype(BF16)
            b_last = b[C - 1:C, :]
            qb, kb = q.astype(BF16), k.astype(BF16)
            q_in = (q * jnp.exp(b)).astype(BF16)
            zero = jnp.zeros_like(qb)

            scores = [jnp.where(row == col, _dot_nt(jnp.where(hl, qb, zero), kb), 0.0) for hl in head_lanes]
            m = C // 2
            while m >= 1:
                ref = level_reference(b, m)
                upper = (trow % (2 * m)) >= m
                qt = jnp.where(upper, q * jnp.exp(jnp.minimum(b - ref, 0.0)), 0.0).astype(BF16)
                kt = jnp.where(upper, 0.0, k * jnp.exp(jnp.minimum(ref - b, 0.0))).astype(BF16)
                same_block = (row // (2 * m)) == (col // (2 * m))
                scores = [sc + jnp.where(same_block, _dot_nt(jnp.where(hl, qt, zero), kt), 0.0)
                          for sc, hl in zip(scores, head_lanes)]
                m //= 2

            kh = (k * jnp.exp(b_last - b)).astype(BF16)
            upd = []
            for n, hl in enumerate(head_lanes):
                h = 2 * p + n
                vl = slice(h * dv, (h + 1) * dv)
                v = v_ref[pl.ds(r0, C), vl]
                o_ref[pl.ds(r0, C), vl] = (_dot_nt(jnp.where(hl, q_in, zero), stb)
                                           + _dot(scores[n].astype(BF16), v))
                upd.append(_dot_tn(v, kh))
            st_ref[p] = st * jnp.exp(b_last) + jnp.where(lax.broadcasted_iota(jnp.int32, st.shape, 1) < dk,
                                                         upd[0], upd[1])
        return carry

    lax.fori_loop(0, ts // C, chunk, 0)

    for h in range(GLA_HEADS):
        vl = slice(h * dv, (h + 1) * dv)
        o = o_ref[:, vl]
        ms = jnp.mean(o * o, axis=-1, keepdims=True)
        g = r_ref[:, vl].astype(F32)
        y = o * lax.rsqrt(ms + LN_EPS) * gn_ref[...] * (g * _sigmoid(g))
        y_ref[:, vl] = y.astype(y_ref.dtype)


def _gla(gq, gk, gv, gr, small, wg_pad, bg, gnorm, tri, B, S):
    T = gq.shape[0]
    ts = TS_GLA
    ns = S // ts
    row = lambda n: pl.BlockSpec((ts, n), lambda b, i: (b * ns + i, 0))
    full = lambda a: pl.BlockSpec(a.shape, lambda b, i: (0,) * a.ndim)
    return pl.pallas_call(
        functools.partial(_gla_kernel, ts=ts), grid=(B, ns),
        in_specs=[row(GLA_KEY_WIDTH), row(GLA_KEY_WIDTH), row(GLA_WIDTH), row(GLA_WIDTH), row(LANES),
                  full(wg_pad), full(bg), full(gnorm), full(tri)],
        out_specs=row(GLA_WIDTH),
        out_shape=jax.ShapeDtypeStruct((T, GLA_WIDTH), BF16),
        scratch_shapes=[pltpu.VMEM((GLA_HEADS // 2, GLA_VALUE_DIM, LANES), F32),
                        pltpu.VMEM((ts, GLA_KEY_WIDTH), F32),
                        pltpu.VMEM((ts, GLA_WIDTH), F32)],
        compiler_params=_cparams("parallel", "arbitrary"), name="gla")(
            gq, gk, gv, gr, small, wg_pad, bg, gnorm, tri)


def _oproj_kernel(yf_ref, yg_ref, h_ref, wo_ref, g_ref, b_ref, wr_ref, h1_ref, st_ref, *part_refs):
    mix = _dot(yf_ref[...], wo_ref[0:FOX_WIDTH, :]) + _dot(yg_ref[...], wo_ref[FOX_WIDTH:, :])
    h1 = _layer_norm(DEEPNORM_ALPHA * h_ref[...] + mix, g_ref[...], b_ref[...])
    h1_ref[...] = h1
    _store_row_parts(h1, part_refs)
    st_ref[...] = _sigmoid(_dot_nt(wr_ref[...], h1.astype(BF16)))


def _oproj(y_fox, y_gla, h, w_out, g, b, w_router_t):
    T, D = h.shape
    tm = TM_OPROJ
    row = lambda n: pl.BlockSpec((tm, n), lambda i: (i, 0))
    full = lambda a: pl.BlockSpec(a.shape, lambda i: (0,) * a.ndim)
    return pl.pallas_call(
        _oproj_kernel, grid=(T // tm,),
        in_specs=[row(FOX_WIDTH), row(GLA_WIDTH), row(D), full(w_out), full(g), full(b), full(w_router_t)],
        out_specs=[row(D), pl.BlockSpec((N_EXPERTS, tm), lambda i: (0, i))] + [row(PART_WORDS)] * ROW_PARTS,
        out_shape=[jax.ShapeDtypeStruct((T, D), F32), jax.ShapeDtypeStruct((N_EXPERTS, T), F32)]
        + [jax.ShapeDtypeStruct((T, PART_WORDS), jnp.uint32)] * ROW_PARTS,
        compiler_params=_cparams("parallel"), name="oproj")(y_fox, y_gla, h, w_out, g, b, w_router_t)


def _route_kernel(s_ref, bias_ref, upper_ref, ones_ref, idx_ref, w_ref, rank_ref, cnt_ref, carry_ref):
    E, tm = s_ref.shape

    @pl.when(pl.program_id(0) == 0)
    def _():
        carry_ref[...] = jnp.zeros_like(carry_ref)

    s = s_ref[...]
    biased = s + bias_ref[...]
    neg = -jnp.inf
    erow = lax.broadcasted_iota(jnp.int32, (E, tm), 0).astype(F32)
    grow = lax.broadcasted_iota(jnp.int32, (GROUP_SIZE, tm), 0).astype(F32)

    gs = []
    for g in range(N_GROUPS):
        blk = biased[g * GROUP_SIZE:(g + 1) * GROUP_SIZE, :]
        m1 = jnp.max(blk, axis=0, keepdims=True)
        i1 = jnp.min(jnp.where(blk == m1, grow, float(GROUP_SIZE)), axis=0, keepdims=True)
        m2 = jnp.max(jnp.where(grow == i1, neg, blk), axis=0, keepdims=True)
        gs.append(m1 + m2)
    keep = []
    for g in range(N_GROUPS):
        beaten = jnp.zeros((1, tm), F32)
        for o in range(N_GROUPS):
            if o == g:
                continue
            wins = (gs[o] > gs[g]) | ((gs[o] == gs[g]) & (o < g))
            beaten = beaten + jnp.where(wins, 1.0, 0.0)
        keep.append(jnp.broadcast_to(beaten < float(TOPK_GROUPS), (GROUP_SIZE, tm)))
    cur = jnp.where(jnp.concatenate(keep, axis=0), biased, neg)

    ids, ws, hots = [], [], []
    chosen = jnp.zeros((E, tm), F32)
    for _ in range(TOP_K):
        m = jnp.max(cur, axis=0, keepdims=True)
        ik = jnp.min(jnp.where(cur == m, erow, float(E)), axis=0, keepdims=True)
        hot = erow == ik
        ws.append(jnp.sum(jnp.where(hot, s, 0.0), axis=0, keepdims=True))
        cur = jnp.where(hot, neg, cur)
        chosen = jnp.where(hot, 1.0, chosen)
        ids.append(ik)
        hots.append(hot)
    wsum = ws[0]
    for w in ws[1:]:
        wsum = wsum + w

    chosen_b = chosen.astype(BF16)
    before = carry_ref[...] + _dot(chosen_b, upper_ref[...])
    carry_ref[...] = carry_ref[...] + _dot(chosen_b, ones_ref[...])
    for k in range(TOP_K):
        idx_ref[k:k + 1, :] = ids[k].astype(jnp.int32)
        w_ref[k:k + 1, :] = ws[k] / wsum * ROUTED_SCALE
        rank_ref[k:k + 1, :] = jnp.sum(jnp.where(hots[k], before, 0.0), axis=0, keepdims=True).astype(jnp.int32)
    cnt_ref[...] = carry_ref[:, 0:LANES]


def _route(scores_t, bias_col, upper, ones):
    E, T = scores_t.shape
    tm = TM_ROUTE
    full = lambda a: pl.BlockSpec(a.shape, lambda i: (0,) * a.ndim)
    kt = pl.BlockSpec((TOP_K, tm), lambda i: (0, i))
    return pl.pallas_call(
        _route_kernel, grid=(T // tm,),
        in_specs=[pl.BlockSpec((E, tm), lambda i: (0, i)), full(bias_col), full(upper), full(ones)],
        out_specs=[kt, kt, kt, pl.BlockSpec((E, LANES), lambda i: (0, 0))],
        out_shape=[jax.ShapeDtypeStruct((TOP_K, T), jnp.int32), jax.ShapeDtypeStruct((TOP_K, T), F32),
                   jax.ShapeDtypeStruct((TOP_K, T), jnp.int32), jax.ShapeDtypeStruct((E, LANES), F32)],
        scratch_shapes=[pltpu.VMEM((E, tm), F32)],
        compiler_params=_cparams("arbitrary"), name="route")(scores_t, bias_col, upper, ones)


def _expert_kernel(first_ref, count_ref, total_ref, *refs):
    x_hbm = refs[:ROW_PARTS]
    wg_ref, wu_ref, wd_ref = refs[ROW_PARTS:ROW_PARTS + 3]
    y_hbm = refs[ROW_PARTS + 3:2 * ROW_PARTS + 3]
    xbuf, ybuf, wgb, wub, wdb, in_sem, out_sem = refs[2 * ROW_PARTS + 3:]
    e = pl.program_id(0)
    bm = xbuf.shape[2]
    total = total_ref[0]

    def in_copy(g, p):
        rows = pl.ds(pl.multiple_of(g * bm, bm), bm)
        return pltpu.make_async_copy(x_hbm[p].at[rows], xbuf.at[g % 2, p], in_sem.at[g % 2, p])

    def out_copy(g, p):
        rows = pl.ds(pl.multiple_of(g * bm, bm), bm)
        return pltpu.make_async_copy(ybuf.at[g % 2, p], y_hbm[p].at[rows], out_sem.at[g % 2, p])

    @pl.when((e == 0) & (total > 0))
    def _():
        for p in range(ROW_PARTS):
            in_copy(0, p).start()

    @pl.when(count_ref[e] > 0)
    def _():
        wgb[...] = wg_ref[...].astype(BF16)
        wub[...] = wu_ref[...].astype(BF16)
        wdb[...] = wd_ref[...].astype(BF16)

        def block(c, carry):
            g = first_ref[e] + c
            slot = g % 2
            for p in range(ROW_PARTS):
                in_copy(g, p).wait()

            @pl.when(g + 1 < total)
            def _():
                for p in range(ROW_PARTS):
                    in_copy(g + 1, p).start()

            gate = up = None
            for c0, xc in _load_row_parts([xbuf[slot, p] for p in range(ROW_PARTS)]):
                xb = xc.astype(BF16)
                rows = slice(c0, c0 + PART_WORDS)
                gc = _dot(xb, wgb[rows, :])
                uc = _dot(xb, wub[rows, :])
                gate = gc if gate is None else gate + gc
                up = uc if up is None else up + uc
            y = _dot((gate * _sigmoid(gate) * up).astype(BF16), wdb[...])

            @pl.when(g >= 2)
            def _():
                for p in range(ROW_PARTS):
                    out_copy(g - 2, p).wait()

            _store_row_parts(y, [ybuf.at[slot, p] for p in range(ROW_PARTS)])
            for p in range(ROW_PARTS):
                out_copy(g, p).start()
            return carry

        lax.fori_loop(0, count_ref[e], block, 0)

    @pl.when(e == pl.num_programs(0) - 1)
    def _():
        for back in (2, 1):
            @pl.when(total >= back)
            def _():
                for p in range(ROW_PARTS):
                    out_copy(total - back, p).wait()


def _experts(first_blk, blk_count, total_blks, xs_parts, w_gate, w_up, w_down):
    P = xs_parts[0].shape[0]
    D = D_MODEL
    bm = BM_EXPERT
    hbm = pl.BlockSpec(memory_space=pl.ANY)
    grid_spec = pltpu.PrefetchScalarGridSpec(
        num_scalar_prefetch=3, grid=(N_EXPERTS,),
        in_specs=[hbm] * ROW_PARTS + [
            pl.BlockSpec((None, D, EXPERT_DIM), lambda e, *_: (e, 0, 0)),
            pl.BlockSpec((None, D, EXPERT_DIM), lambda e, *_: (e, 0, 0)),
            pl.BlockSpec((None, EXPERT_DIM, D), lambda e, *_: (e, 0, 0))],
        out_specs=[hbm] * ROW_PARTS,
        scratch_shapes=[pltpu.VMEM((2, ROW_PARTS, bm, PART_WORDS), jnp.uint32),
                        pltpu.VMEM((2, ROW_PARTS, bm, PART_WORDS), jnp.uint32),
                        pltpu.VMEM((D, EXPERT_DIM), BF16), pltpu.VMEM((D, EXPERT_DIM), BF16),
                        pltpu.VMEM((EXPERT_DIM, D), BF16),
                        pltpu.SemaphoreType.DMA((2, ROW_PARTS)), pltpu.SemaphoreType.DMA((2, ROW_PARTS))])
    return pl.pallas_call(
        _expert_kernel, grid_spec=grid_spec,
        out_shape=[jax.ShapeDtypeStruct((P, PART_WORDS), jnp.uint32)] * ROW_PARTS,
        compiler_params=_cparams("arbitrary"), name="experts")(
            first_blk, blk_count, total_blks, *xs_parts, w_gate, w_up, w_down)


def _pos_kernel(idx_ref, rank_ref, start_ref, pos_ref):
    K, tm = idx_ref.shape
    E = start_ref.shape[0]
    erow = lax.broadcasted_iota(jnp.int32, (E, tm), 0)
    start = start_ref[...]
    for k in range(K):
        first = jnp.sum(jnp.where(erow == idx_ref[k:k + 1, :], start, 0.0), axis=0, keepdims=True)
        pos_ref[k:k + 1, :] = first.astype(jnp.int32) + rank_ref[k:k + 1, :]


def _positions(idx_t, rank_t, start_col):
    K, T = idx_t.shape
    tm = TM_ROUTE
    kt = pl.BlockSpec((K, tm), lambda i: (0, i))
    return pl.pallas_call(
        _pos_kernel, grid=(T // tm,),
        in_specs=[kt, kt, pl.BlockSpec(start_col.shape, lambda i: (0, 0))],
        out_specs=kt, out_shape=jax.ShapeDtypeStruct((K, T), jnp.int32),
        compiler_params=_cparams("parallel"), name="positions")(idx_t, rank_t, start_col)


def _sc_mesh():
    return plsc.VectorSubcoreMesh(core_axis_name="core", subcore_axis_name="subcore",
                                  num_cores=SC_CORES, num_subcores=SC_SUBCORES)


def _sc_scatter_rows(parts, pos, n_out):
    T, W = parts[0].shape
    K = pos.shape[0]
    n = len(parts)
    win = SC_ROW_WINDOW

    @functools.partial(pl.kernel, out_type=[jax.ShapeDtypeStruct((n_out, W), parts[0].dtype)] * n, mesh=_sc_mesh(),
                       name="sc_dispatch")
    def k(*refs):
        x_hbms, p_hbm, o_hbms = refs[:n], refs[n], refs[n + 1:]
        for x_hbm, o_hbm in zip(x_hbms, o_hbms):
            def body(x_vmem, p_vmem, o_hbm=o_hbm):
                for j in range(K):
                    pltpu.sync_copy(x_vmem, o_hbm.at[p_vmem.at[j]])

            pltpu.emit_pipeline(
                body, grid=(T // win,),
                in_specs=[pl.BlockSpec((win, W), lambda i: (i, 0)), pl.BlockSpec((K, win), lambda i: (0, i))],
                out_specs=[], core_axis_name=("core", "subcore"),
                dimension_semantics=(pltpu.PARALLEL,))(x_hbm, p_hbm)

    return k(*parts, pos)


def _sc_gather_rows(tables, idx):
    M = idx.shape[0]
    W = tables[0].shape[1]
    n = len(tables)
    win = SC_ROW_WINDOW

    @functools.partial(pl.kernel, out_type=[jax.ShapeDtypeStruct((M, W), tables[0].dtype)] * n, mesh=_sc_mesh(),
                       name="sc_combine")
    def k(*refs):
        t_hbms, i_hbm, o_hbms = refs[:n], refs[n], refs[n + 1:]
        for t_hbm, o_hbm in zip(t_hbms, o_hbms):
            def body(i_vmem, o_vmem, t_hbm=t_hbm):
                pltpu.sync_copy(t_hbm.at[i_vmem.at[0]], o_vmem)

            pltpu.emit_pipeline(
                body, grid=(M // win,),
                in_specs=[pl.BlockSpec((1, win), lambda i: (0, i))],
                out_specs=[pl.BlockSpec((win, W), lambda i: (i, 0))],
                core_axis_name=("core", "subcore"), dimension_semantics=(pltpu.PARALLEL,))(i_hbm, o_hbm)

    return k(*tables, idx.reshape(1, M))


def _final_kernel(h1_ref, wk_ref, wg_ref, wu_ref, wd_ref, g_ref, b_ref, *refs):
    yk_refs, o_ref = refs[:ROW_PARTS], refs[ROW_PARTS]
    h1 = h1_ref[...]
    hb = h1.astype(BF16)
    g = _dot(hb, wg_ref[...])
    u = _dot(hb, wu_ref[...])
    ffn = _dot((g * _sigmoid(g) * u).astype(BF16), wd_ref[...])
    wk = wk_ref[...]
    sums = None
    for k in range(TOP_K):
        cols = _load_row_parts([r[k] for r in yk_refs])
        terms = [c * wk[:, k:k + 1] for _, c in cols]
        sums = terms if sums is None else [s + t for s, t in zip(sums, terms)]
    ffn = ffn + jnp.concatenate(sums, axis=1)
    o_ref[...] = _layer_norm(DEEPNORM_ALPHA * h1 + ffn, g_ref[...], b_ref[...])


def _final(h1, yk_parts, wk, w_sg, w_su, w_sd, g, b):
    T, D = h1.shape
    tm = TM_FINAL
    row = lambda n: pl.BlockSpec((tm, n), lambda i: (i, 0))
    full = lambda a: pl.BlockSpec(a.shape, lambda i: (0,) * a.ndim)
    return pl.pallas_call(
        _final_kernel, grid=(T // tm,),
        in_specs=[row(D), row(TOP_K), full(w_sg), full(w_su), full(w_sd), full(g), full(b)]
        + [pl.BlockSpec((TOP_K, tm, PART_WORDS), lambda i: (0, i, 0))] * ROW_PARTS,
        out_specs=row(D), out_shape=jax.ShapeDtypeStruct((T, D), F32),
        compiler_params=_cparams("parallel"), name="final")(h1, wk, w_sg, w_su, w_sd, g, b, *yk_parts)


def _block_diag_tri(n, c):
    r = np.arange(n)
    return jnp.asarray(((r[:, None] >= r[None, :]) & (r[:, None] // c == r[None, :] // c)).astype(np.float32), BF16)


def kernel(x, ln_in_g, ln_in_b, w_in, b_fgate, w_gate_up, b_gate, g_gla_norm, w_out, ln1_g, ln1_b, w_router,
           router_bias, w_exp_gate, w_exp_up, w_exp_down, w_sh_gate, w_sh_up, w_sh_down, ln2_g, ln2_b):
    B, S, D = x.shape
    T = B * S
    x2 = x.reshape(T, D)
    l = 0
    row = lambda a: a.reshape(1, -1)

    off = np.cumsum((0,) + IN_SPLITS)
    seg = lambda i: w_in[l][:, off[i]:off[i + 1]]
    w_main = jnp.concatenate([seg(0) * (FOX_HEAD_DIM ** -0.5 * LOG2E), seg(1), seg(2), seg(4) * GLA_KEY_DIM ** -0.5, seg(5),
                              seg(6), seg(8)], axis=1).astype(BF16)
    n_small = FOX_HEADS + GLA_GATE_RANK
    w_small = jnp.concatenate([seg(3), seg(7), jnp.zeros((D, LANES - n_small), F32)], axis=1).astype(BF16)
    bf_pad = jnp.concatenate([b_fgate[l], jnp.zeros((LANES - FOX_HEADS,), F32)]).reshape(1, LANES)
    wg_pad = jnp.zeros((LANES, GLA_KEY_WIDTH), F32).at[FOX_HEADS:n_small].set(w_gate_up[l]).astype(BF16)

    h, fq, fk, fv, gq, gk, gv, gr, small = _proj(x2, row(ln_in_g), row(ln_in_b), w_main, w_small)

    y_fox = _fox(fq, fk, fv, _fgate(small, bf_pad, B, S), B, S)

    y_gla = _gla(gq, gk, gv, gr, small, wg_pad, row(b_gate[l]), row(g_gla_norm[l]),
                 _block_diag_tri(TS_GLA, CHUNK), B, S)

    h1, scores_t, *h1_parts = _oproj(y_fox, y_gla, h, w_out[l].astype(BF16), row(ln1_g[l]), row(ln1_b[l]),
                                     w_router[l].T.astype(BF16))

    tm = TM_ROUTE
    r = np.arange(tm)
    upper = jnp.asarray((r[:, None] < r[None, :]).astype(np.float32), BF16)
    ones = jnp.ones((tm, tm), BF16)
    idx_t, w_t, rank_t, cnt = _route(scores_t, router_bias[l].reshape(N_EXPERTS, 1), upper, ones)

    bm = BM_EXPERT
    A = T * TOP_K
    nb = A // bm + N_EXPERTS
    P = nb * bm
    counts = cnt[:, 0].astype(jnp.int32)
    padded = (counts + bm - 1) // bm * bm
    pend = jnp.cumsum(padded)
    pstart = pend - padded
    pos = _positions(idx_t, rank_t, pstart.astype(F32).reshape(N_EXPERTS, 1))

    xs_parts = _sc_scatter_rows(h1_parts, pos, P)
    ys_parts = _experts(pstart // bm, padded // bm, (pend[-1:] // bm), xs_parts,
                        w_exp_gate[l], w_exp_up[l], w_exp_down[l])
    yk_parts = [y.reshape(TOP_K, T, PART_WORDS) for y in _sc_gather_rows(ys_parts, pos.reshape(-1))]

    out = _final(h1, yk_parts, w_t.T, w_sh_gate[l].astype(BF16), w_sh_up[l].astype(BF16),
                 w_sh_down[l].astype(BF16), row(ln2_g[l]), row(ln2_b[l]))
    return out.reshape(B, S, D)
```

```python
import functools

import jax
import jax.numpy as jnp
import numpy as np
from jax import lax
from jax.experimental import pallas as pl
from jax.experimental.pallas import tpu as pltpu
from jax.experimental.pallas import tpu_sc as plsc

F32 = jnp.float32
BF16 = jnp.bfloat16

D_MODEL = 1024
CHUNK = 64
LN_EPS = 1e-5
LOG2E = 1.4426950408889634
FOX_HEADS = 8
FOX_HEAD_DIM = 64
FOX_WIDTH = FOX_HEADS * FOX_HEAD_DIM
GLA_HEADS = 4
GLA_KEY_DIM = 64
GLA_VALUE_DIM = 128
GLA_KEY_WIDTH = GLA_HEADS * GLA_KEY_DIM
GLA_WIDTH = GLA_HEADS * GLA_VALUE_DIM
GLA_GATE_RANK = 16
GLA_GATE_TEMP = 16.0
N_EXPERTS = 256
N_GROUPS = 8
GROUP_SIZE = N_EXPERTS // N_GROUPS
TOPK_GROUPS = 4
TOP_K = 8
EXPERT_DIM = 256
SHARED_DIM = 256
ROUTED_SCALE = 2.5
DEPTH = 1
DEEPNORM_ALPHA = (2.0 * DEPTH) ** 0.25
IN_SPLITS = (FOX_WIDTH, FOX_WIDTH, FOX_WIDTH, FOX_HEADS, GLA_KEY_WIDTH, GLA_KEY_WIDTH, GLA_WIDTH,
             GLA_GATE_RANK, GLA_WIDTH)

LANES = 128
VMEM_LIMIT = 48 * 1024 * 1024

TM_PROJ = 512
TQ_FOX = 512
TK_FOX = 512
FOX_GROUP = 4
TS_GLA = 512
TM_OPROJ = 512
TM_ROUTE = 512
BM_EXPERT = 512
TM_FINAL = 512

SC_CORES = 2
SC_SUBCORES = 16
SC_ROW_WINDOW = 128
ROW_PARTS = 2
PART_WORDS = D_MODEL // 2 // ROW_PARTS


def _cparams(*sem):
    return pltpu.CompilerParams(dimension_semantics=sem, vmem_limit_bytes=VMEM_LIMIT)


def _layer_norm(x, g, b):
    mu = jnp.mean(x, axis=-1, keepdims=True)
    xc = x - mu
    var = jnp.mean(xc * xc, axis=-1, keepdims=True)
    return xc * lax.rsqrt(var + LN_EPS) * g + b


def _log_sigmoid(z):
    return jnp.minimum(z, 0.0) - jnp.log1p(jnp.exp(-jnp.abs(z)))


def _sigmoid(z):
    return 1.0 / (1.0 + jnp.exp(-z))


def _dot(a, b):
    return jnp.dot(a, b, preferred_element_type=F32)


def _dot_nt(a, b):
    return lax.dot_general(a, b, (((1,), (1,)), ((), ())), preferred_element_type=F32)


def _dot_tn(a, b):
    return lax.dot_general(a, b, (((0,), (0,)), ((), ())), preferred_element_type=F32)


def _pack_bf16_pairs(x):
    n = x.shape[1] // 2
    u = lax.bitcast_convert_type(x.astype(BF16).astype(F32), jnp.uint32)
    return (u[:, :n] >> 16) | u[:, n:]


def _unpack_bf16_pairs(w):
    lo = lax.bitcast_convert_type(w << 16, F32)
    hi = lax.bitcast_convert_type(w & jnp.uint32(0xFFFF0000), F32)
    return lo, hi


def _store_row_parts(x, part_refs):
    packed = _pack_bf16_pairs(x)
    for p, ref in enumerate(part_refs):
        ref[...] = packed[:, p * PART_WORDS:(p + 1) * PART_WORDS]


def _load_row_parts(parts):
    out = []
    for p, w in enumerate(parts):
        lo, hi = _unpack_bf16_pairs(w)
        out.append((p * PART_WORDS, lo))
        out.append((D_MODEL // 2 + p * PART_WORDS, hi))
    return sorted(out, key=lambda t: t[0])


def _proj_kernel(x_ref, g_ref, b_ref, wm_ref, ws_ref,
                 h_ref, fq_ref, fk_ref, fv_ref, gq_ref, gk_ref, gv_ref, gr_ref, sm_ref):
    h = _layer_norm(x_ref[...], g_ref[...], b_ref[...])
    h_ref[...] = h
    hb = h.astype(BF16)
    off = 0
    for ref in (fq_ref, fk_ref, fv_ref, gq_ref, gk_ref, gv_ref, gr_ref):
        n = ref.shape[1]
        ref[...] = _dot(hb, wm_ref[:, off:off + n]).astype(ref.dtype)
        off += n
    sm_ref[...] = _dot(hb, ws_ref[...])


def _proj(x2, g, b, w_main, w_small):
    T, D = x2.shape
    tm = TM_PROJ
    widths = (FOX_WIDTH, FOX_WIDTH, FOX_WIDTH, GLA_KEY_WIDTH, GLA_KEY_WIDTH, GLA_WIDTH, GLA_WIDTH)
    row = lambda n: pl.BlockSpec((tm, n), lambda i: (i, 0))
    full = lambda a: pl.BlockSpec(a.shape, lambda i: (0,) * a.ndim)
    out_shape = [jax.ShapeDtypeStruct((T, D), F32)]
    out_shape += [jax.ShapeDtypeStruct((T, n), BF16) for n in widths]
    out_shape += [jax.ShapeDtypeStruct((T, LANES), F32)]
    out_specs = [row(D)] + [row(n) for n in widths] + [row(LANES)]
    return pl.pallas_call(
        _proj_kernel, grid=(T // tm,),
        in_specs=[row(D), full(g), full(b), full(w_main), full(w_small)],
        out_specs=out_specs, out_shape=out_shape,
        compiler_params=_cparams("parallel"), name="proj")(x2, g, b, w_main, w_small)


def _fgate_kernel(sm_ref, bf_ref, c_ref):
    S = sm_ref.shape[0]
    lf = _log_sigmoid(sm_ref[...] + bf_ref[...])
    rows = lax.broadcasted_iota(jnp.int32, lf.shape, 0)
    s = 1
    while s < S:
        lf = lf + jnp.where(rows >= s, pltpu.roll(lf, s, axis=0), 0.0)
        s *= 2
    c_ref[...] = lf * LOG2E


def _fgate(small, bf_pad, B, S):
    T = small.shape[0]
    return pl.pallas_call(
        _fgate_kernel, grid=(B,),
        in_specs=[pl.BlockSpec((S, LANES), lambda b: (b, 0)),
                  pl.BlockSpec((1, LANES), lambda b: (0, 0))],
        out_specs=pl.BlockSpec((S, LANES), lambda b: (b, 0)),
        out_shape=jax.ShapeDtypeStruct((T, LANES), F32),
        compiler_params=_cparams("parallel"), name="fgate")(small, bf_pad)


def _split3(x):
    hi = x.astype(BF16).astype(F32)
    r = x - hi
    mid = r.astype(BF16).astype(F32)
    return hi, mid, (r - mid).astype(BF16).astype(F32)


def _fox_operand(pair_ref_tile, odd, c_col, is_key):
    dh = FOX_HEAD_DIM
    x = pair_ref_tile.astype(F32)
    if odd:
        x = pltpu.roll(x, dh, axis=1)
    lane = lax.broadcasted_iota(jnp.int32, x.shape, 1)
    t0, t1, t2 = _split3(-c_col if is_key else c_col)
    first = dh + 3 if is_key else dh
    bias = jnp.where(lane == first, t0, jnp.where(lane == first + 1, t1, jnp.where(lane == first + 2, t2, 0.0)))
    ones_lo = dh if is_key else dh + 3
    bias = jnp.where((lane >= ones_lo) & (lane < ones_lo + 3), 1.0, bias)
    return jnp.where(lane < dh, x, bias).astype(BF16)


def _fox_kernel(q_ref, k_ref, v_ref, cq_ref, ck_ref, o_ref, ka_ref, vt_ref, *, tq, tk):
    i = pl.program_id(1)
    S = k_ref.shape[0]
    dh = FOX_HEAD_DIM

    @pl.when(i == 0)
    def _():
        def fill(r, carry):
            r0 = pl.multiple_of(r * tk, tk)
            for h in range(FOX_HEADS):
                pair = slice(LANES * (h // 2), LANES * (h // 2 + 1))
                ka_ref[pl.ds(r0, tk), LANES * h:LANES * (h + 1)] = _fox_operand(
                    k_ref[pl.ds(r0, tk), pair], h % 2, ck_ref[pl.ds(r0, tk), h:h + 1], True)
            vt_ref[r] = v_ref[pl.ds(r0, tk), :].astype(F32).T.astype(BF16)
            return carry
        lax.fori_loop(0, S // tk, fill, 0)

    key = lax.broadcasted_iota(jnp.int32, (tk, tq), 0)
    qry = lax.broadcasted_iota(jnp.int32, (tk, tq), 1)
    n_diag = tq // tk
    n_full = i * n_diag

    for p in range(FOX_HEADS // FOX_GROUP):
        group = slice(dh * FOX_GROUP * p, dh * FOX_GROUP * (p + 1))
        heads = tuple(range(FOX_GROUP * p, FOX_GROUP * (p + 1)))
        qa = [_fox_operand(q_ref[:, LANES * (h // 2):LANES * (h // 2 + 1)], h % 2, cq_ref[:, h:h + 1], False)
              for h in heads]

        def step(j, carry, diag):
            r0 = pl.multiple_of(j * tk, tk)
            ss = [_dot_nt(ka_ref[pl.ds(r0, tk), LANES * h:LANES * (h + 1)], qa[n])
                  for n, h in enumerate(heads)]
            if diag is not None:
                ss = [jnp.where(qry >= key + diag * tk, s, -jnp.inf) for s in ss]
            ms = [jnp.maximum(c[0], jnp.max(s, axis=0, keepdims=True)) for c, s in zip(carry, ss)]
            prs = [jnp.exp2(s - m) for s, m in zip(ss, ms)]
            alphas = [jnp.exp2(c[0] - m) for c, m in zip(carry, ms)]
            ls = [a * c[1] + jnp.sum(pr, axis=0, keepdims=True) for a, c, pr in zip(alphas, carry, prs)]
            pvs = [_dot(vt_ref[j, dh * h:dh * (h + 1), :], pr.astype(BF16)) for h, pr in zip(heads, prs)]
            accs = [a * c[2] + pv for a, c, pv in zip(alphas, carry, pvs)]
            return tuple(zip(ms, ls, accs))

        init = tuple((jnp.full((1, tq), -jnp.inf, F32), jnp.zeros((1, tq), F32), jnp.zeros((dh, tq), F32))
                     for _ in heads)
        carry = lax.fori_loop(0, n_full, functools.partial(step, diag=None), init)
        for d in range(n_diag):
            carry = step(n_full + d, carry, d)
        o_ref[:, group] = jnp.concatenate([(acc / l).T for _, l, acc in carry], axis=1).astype(o_ref.dtype)


def _fox(fq, fk, fv, c, B, S):
    T = fq.shape[0]
    tq, tk = TQ_FOX, TK_FOX
    nq = S // tq
    return pl.pallas_call(
        functools.partial(_fox_kernel, tq=tq, tk=tk), grid=(B, nq),
        in_specs=[pl.BlockSpec((tq, FOX_WIDTH), lambda b, i: (b * nq + i, 0)),
                  pl.BlockSpec((S, FOX_WIDTH), lambda b, i: (b, 0)),
                  pl.BlockSpec((S, FOX_WIDTH), lambda b, i: (b, 0)),
                  pl.BlockSpec((tq, LANES), lambda b, i: (b * nq + i, 0)),
                  pl.BlockSpec((S, LANES), lambda b, i: (b, 0))],
        out_specs=pl.BlockSpec((tq, FOX_WIDTH), lambda b, i: (b * nq + i, 0)),
        out_shape=jax.ShapeDtypeStruct((T, FOX_WIDTH), BF16),
        scratch_shapes=[pltpu.VMEM((S, FOX_HEADS * LANES), BF16), pltpu.VMEM((S // tk, FOX_WIDTH, tk), BF16)],
        compiler_params=_cparams("parallel", "arbitrary"), name="fox")(fq, fk, fv, c, c)


def _gla_kernel(q_ref, k_ref, v_ref, r_ref, sm_ref, wg_ref, bg_ref, gn_ref, tri_ref, y_ref,
                st_ref, b_ref, o_ref, *, ts):
    dk, dv, C = GLA_KEY_DIM, GLA_VALUE_DIM, CHUNK

    @pl.when(pl.program_id(1) == 0)
    def _():
        st_ref[...] = jnp.zeros_like(st_ref)

    z = _dot(sm_ref[...].astype(BF16), wg_ref[...]) + bg_ref[...]
    la = _log_sigmoid(z) * (1.0 / GLA_GATE_TEMP)
    hi = la.astype(BF16)
    r1 = la - hi.astype(F32)
    mid = r1.astype(BF16)
    lo = (r1 - mid.astype(F32)).astype(BF16)
    tri = tri_ref[...]
    b_ref[...] = _dot(tri, hi) + _dot(tri, mid) + _dot(tri, lo)

    row = lax.broadcasted_iota(jnp.int32, (C, C), 0)
    col = lax.broadcasted_iota(jnp.int32, (C, C), 1)
    trow = lax.broadcasted_iota(jnp.int32, (C, LANES), 0)
    lane = lax.broadcasted_iota(jnp.int32, (C, LANES), 1)
    head_lanes = [lane < dk, lane >= dk]

    def level_reference(b, m):
        if m >= 4:
            n = 2 * m
            return jnp.concatenate(
                [jnp.broadcast_to(b[s + m - 1:s + m, :], (n, LANES)) for s in range(0, C, n)], axis=0)
        up1 = pltpu.roll(b, 1, axis=0)
        if m == 1:
            return jnp.where(trow % 2 == 0, b, up1)
        up2 = pltpu.roll(b, 2, axis=0)
        down1 = pltpu.roll(b, C - 1, axis=0)
        tm = trow % 4
        return jnp.where(tm == 0, down1, jnp.where(tm == 1, b, jnp.where(tm == 2, up1, up2)))

    def chunk(c, carry):
        r0 = pl.multiple_of(c * C, C)
        for p in range(GLA_HEADS // 2):
            kl = slice(p * LANES, (p + 1) * LANES)
            q = q_ref[pl.ds(r0, C), kl].astype(F32)
            k = k_ref[pl.ds(r0, C), kl].astype(F32)
            b = b_ref[pl.ds(r0, C), kl]
            st = st_ref[p]
            stb = st.astype(BF16)
            b_last = b[C - 1:C, :]
            qb, kb = q.astype(BF16), k.astype(BF16)
            q_in = (q * jnp.exp(b)).astype(BF16)
            zero = jnp.zeros_like(qb)

            scores = [jnp.where(row == col, _dot_nt(jnp.where(hl, qb, zero), kb), 0.0) for hl in head_lanes]
            m = C // 2
            while m >= 1:
                ref = level_reference(b, m)
                upper = (trow % (2 * m)) >= m
                qt = jnp.where(upper, q * jnp.exp(jnp.minimum(b - ref, 0.0)), 0.0).astype(BF16)
                kt = jnp.where(upper, 0.0, k * jnp.exp(jnp.minimum(ref - b, 0.0))).astype(BF16)
                same_block = (row // (2 * m)) == (col // (2 * m))
                scores = [sc + jnp.where(same_block, _dot_nt(jnp.where(hl, qt, zero), kt), 0.0)
                          for sc, hl in zip(scores, head_lanes)]
                m //= 2

            kh = (k * jnp.exp(b_last - b)).astype(BF16)
            upd = []
            for n, hl in enumerate(head_lanes):
                h = 2 * p + n
                vl = slice(h * dv, (h + 1) * dv)
                v = v_ref[pl.ds(r0, C), vl]
                o_ref[pl.ds(r0, C), vl] = (_dot_nt(jnp.where(hl, q_in, zero), stb)
                                           + _dot(scores[n].astype(BF16), v))
                upd.append(_dot_tn(v, kh))
            st_ref[p] = st * jnp.exp(b_last) + jnp.where(lax.broadcasted_iota(jnp.int32, st.shape, 1) < dk,
                                                         upd[0], upd[1])
        return carry

    lax.fori_loop(0, ts // C, chunk, 0)

    for h in range(GLA_HEADS):
        vl = slice(h * dv, (h + 1) * dv)
        o = o_ref[:, vl]
        ms = jnp.mean(o * o, axis=-1, keepdims=True)
        g = r_ref[:, vl].astype(F32)
        y = o * lax.rsqrt(ms + LN_EPS) * gn_ref[...] * (g * _sigmoid(g))
        y_ref[:, vl] = y.astype(y_ref.dtype)


def _gla(gq, gk, gv, gr, small, wg_pad, bg, gnorm, tri, B, S):
    T = gq.shape[0]
    ts = TS_GLA
    ns = S // ts
    row = lambda n: pl.BlockSpec((ts, n), lambda b, i: (b * ns + i, 0))
    full = lambda a: pl.BlockSpec(a.shape, lambda b, i: (0,) * a.ndim)
    return pl.pallas_call(
        functools.partial(_gla_kernel, ts=ts), grid=(B, ns),
        in_specs=[row(GLA_KEY_WIDTH), row(GLA_KEY_WIDTH), row(GLA_WIDTH), row(GLA_WIDTH), row(LANES),
                  full(wg_pad), full(bg), full(gnorm), full(tri)],
        out_specs=row(GLA_WIDTH),
        out_shape=jax.ShapeDtypeStruct((T, GLA_WIDTH), BF16),
        scratch_shapes=[pltpu.VMEM((GLA_HEADS // 2, GLA_VALUE_DIM, LANES), F32),
                        pltpu.VMEM((ts, GLA_KEY_WIDTH), F32),
                        pltpu.VMEM((ts, GLA_WIDTH), F32)],
        compiler_params=_cparams("parallel", "arbitrary"), name="gla")(
            gq, gk, gv, gr, small, wg_pad, bg, gnorm, tri)


def _oproj_kernel(yf_ref, yg_ref, h_ref, wo_ref, g_ref, b_ref, wr_ref, h1_ref, st_ref, *part_refs):
    mix = _dot(yf_ref[...], wo_ref[0:FOX_WIDTH, :]) + _dot(yg_ref[...], wo_ref[FOX_WIDTH:, :])
    h1 = _layer_norm(DEEPNORM_ALPHA * h_ref[...] + mix, g_ref[...], b_ref[...])
    h1_ref[...] = h1
    _store_row_parts(h1, part_refs)
    st_ref[...] = _sigmoid(_dot_nt(wr_ref[...], h1.astype(BF16)))


def _oproj(y_fox, y_gla, h, w_out, g, b, w_router_t):
    T, D = h.shape
    tm = TM_OPROJ
    row = lambda n: pl.BlockSpec((tm, n), lambda i: (i, 0))
    full = lambda a: pl.BlockSpec(a.shape, lambda i: (0,) * a.ndim)
    return pl.pallas_call(
        _oproj_kernel, grid=(T // tm,),
        in_specs=[row(FOX_WIDTH), row(GLA_WIDTH), row(D), full(w_out), full(g), full(b), full(w_router_t)],
        out_specs=[row(D), pl.BlockSpec((N_EXPERTS, tm), lambda i: (0, i))] + [row(PART_WORDS)] * ROW_PARTS,
        out_shape=[jax.ShapeDtypeStruct((T, D), F32), jax.ShapeDtypeStruct((N_EXPERTS, T), F32)]
        + [jax.ShapeDtypeStruct((T, PART_WORDS), jnp.uint32)] * ROW_PARTS,
        compiler_params=_cparams("parallel"), name="oproj")(y_fox, y_gla, h, w_out, g, b, w_router_t)


def _route_kernel(s_ref, bias_ref, upper_ref, ones_ref, idx_ref, w_ref, rank_ref, cnt_ref, carry_ref):
    E, tm = s_ref.shape

    @pl.when(pl.program_id(0) == 0)
    def _():
        carry_ref[...] = jnp.zeros_like(carry_ref)

    s = s_ref[...]
    biased = s + bias_ref[...]
    neg = -jnp.inf
    erow = lax.broadcasted_iota(jnp.int32, (E, tm), 0).astype(F32)
    grow = lax.broadcasted_iota(jnp.int32, (GROUP_SIZE, tm), 0).astype(F32)

    gs = []
    for g in range(N_GROUPS):
        blk = biased[g * GROUP_SIZE:(g + 1) * GROUP_SIZE, :]
        m1 = jnp.max(blk, axis=0, keepdims=True)
        i1 = jnp.min(jnp.where(blk == m1, grow, float(GROUP_SIZE)), axis=0, keepdims=True)
        m2 = jnp.max(jnp.where(grow == i1, neg, blk), axis=0, keepdims=True)
        gs.append(m1 + m2)
    keep = []
    for g in range(N_GROUPS):
        beaten = jnp.zeros((1, tm), F32)
        for o in range(N_GROUPS):
            if o == g:
                continue
            wins = (gs[o] > gs[g]) | ((gs[o] == gs[g]) & (o < g))
            beaten = beaten + jnp.where(wins, 1.0, 0.0)
        keep.append(jnp.broadcast_to(beaten < float(TOPK_GROUPS), (GROUP_SIZE, tm)))
    cur = jnp.where(jnp.concatenate(keep, axis=0), biased, neg)

    ids, ws, hots = [], [], []
    chosen = jnp.zeros((E, tm), F32)
    for _ in range(TOP_K):
        m = jnp.max(cur, axis=0, keepdims=True)
        ik = jnp.min(jnp.where(cur == m, erow, float(E)), axis=0, keepdims=True)
        hot = erow == ik
        ws.append(jnp.sum(jnp.where(hot, s, 0.0), axis=0, keepdims=True))
        cur = jnp.where(hot, neg, cur)
        chosen = jnp.where(hot, 1.0, chosen)
        ids.append(ik)
        hots.append(hot)
    wsum = ws[0]
    for w in ws[1:]:
        wsum = wsum + w

    chosen_b = chosen.astype(BF16)
    before = carry_ref[...] + _dot(chosen_b, upper_ref[...])
    carry_ref[...] = carry_ref[...] + _dot(chosen_b, ones_ref[...])
    for k in range(TOP_K):
        idx_ref[k:k + 1, :] = ids[k].astype(jnp.int32)
        w_ref[k:k + 1, :] = ws[k] / wsum * ROUTED_SCALE
        rank_ref[k:k + 1, :] = jnp.sum(jnp.where(hots[k], before, 0.0), axis=0, keepdims=True).astype(jnp.int32)
    cnt_ref[...] = carry_ref[:, 0:LANES]


def _route(scores_t, bias_col, upper, ones):
    E, T = scores_t.shape
    tm = TM_ROUTE
    full = lambda a: pl.BlockSpec(a.shape, lambda i: (0,) * a.ndim)
    kt = pl.BlockSpec((TOP_K, tm), lambda i: (0, i))
    return pl.pallas_call(
        _route_kernel, grid=(T // tm,),
        in_specs=[pl.BlockSpec((E, tm), lambda i: (0, i)), full(bias_col), full(upper), full(ones)],
        out_specs=[kt, kt, kt, pl.BlockSpec((E, LANES), lambda i: (0, 0))],
        out_shape=[jax.ShapeDtypeStruct((TOP_K, T), jnp.int32), jax.ShapeDtypeStruct((TOP_K, T), F32),
                   jax.ShapeDtypeStruct((TOP_K, T), jnp.int32), jax.ShapeDtypeStruct((E, LANES), F32)],
        scratch_shapes=[pltpu.VMEM((E, tm), F32)],
        compiler_params=_cparams("arbitrary"), name="route")(scores_t, bias_col, upper, ones)


def _expert_kernel(first_ref, count_ref, total_ref, *refs):
    x_hbm = refs[:ROW_PARTS]
    wg_ref, wu_ref, wd_ref = refs[ROW_PARTS:ROW_PARTS + 3]
    y_hbm = refs[ROW_PARTS + 3:2 * ROW_PARTS + 3]
    xbuf, ybuf, wgb, wub, wdb, in_sem, out_sem = refs[2 * ROW_PARTS + 3:]
    e = pl.program_id(0)
    bm = xbuf.shape[2]
    total = total_ref[0]

    def in_copy(g, p):
        rows = pl.ds(pl.multiple_of(g * bm, bm), bm)
        return pltpu.make_async_copy(x_hbm[p].at[rows], xbuf.at[g % 2, p], in_sem.at[g % 2, p])

    def out_copy(g, p):
        rows = pl.ds(pl.multiple_of(g * bm, bm), bm)
        return pltpu.make_async_copy(ybuf.at[g % 2, p], y_hbm[p].at[rows], out_sem.at[g % 2, p])

    @pl.when((e == 0) & (total > 0))
    def _():
        for p in range(ROW_PARTS):
            in_copy(0, p).start()

    @pl.when(count_ref[e] > 0)
    def _():
        wgb[...] = wg_ref[...].astype(BF16)
        wub[...] = wu_ref[...].astype(BF16)
        wdb[...] = wd_ref[...].astype(BF16)

        def block(c, carry):
            g = first_ref[e] + c
            slot = g % 2
            for p in range(ROW_PARTS):
                in_copy(g, p).wait()

            @pl.when(g + 1 < total)
            def _():
                for p in range(ROW_PARTS):
                    in_copy(g + 1, p).start()

            gate = up = None
            for c0, xc in _load_row_parts([xbuf[slot, p] for p in range(ROW_PARTS)]):
                xb = xc.astype(BF16)
                rows = slice(c0, c0 + PART_WORDS)
                gc = _dot(xb, wgb[rows, :])
                uc = _dot(xb, wub[rows, :])
                gate = gc if gate is None else gate + gc
                up = uc if up is None else up + uc
            y = _dot((gate * _sigmoid(gate) * up).astype(BF16), wdb[...])

            @pl.when(g >= 2)
            def _():
                for p in range(ROW_PARTS):
                    out_copy(g - 2, p).wait()

            _store_row_parts(y, [ybuf.at[slot, p] for p in range(ROW_PARTS)])
            for p in range(ROW_PARTS):
                out_copy(g, p).start()
            return carry

        lax.fori_loop(0, count_ref[e], block, 0)

    @pl.when(e == pl.num_programs(0) - 1)
    def _():
        for back in (2, 1):
            @pl.when(total >= back)
            def _():
                for p in range(ROW_PARTS):
                    out_copy(total - back, p).wait()


def _experts(first_blk, blk_count, total_blks, xs_parts, w_gate, w_up, w_down):
    P = xs_parts[0].shape[0]
    D = D_MODEL
    bm = BM_EXPERT
    hbm = pl.BlockSpec(memory_space=pl.ANY)
    grid_spec = pltpu.PrefetchScalarGridSpec(
        num_scalar_prefetch=3, grid=(N_EXPERTS,),
        in_specs=[hbm] * ROW_PARTS + [
            pl.BlockSpec((None, D, EXPERT_DIM), lambda e, *_: (e, 0, 0)),
            pl.BlockSpec((None, D, EXPERT_DIM), lambda e, *_: (e, 0, 0)),
            pl.BlockSpec((None, EXPERT_DIM, D), lambda e, *_: (e, 0, 0))],
        out_specs=[hbm] * ROW_PARTS,
        scratch_shapes=[pltpu.VMEM((2, ROW_PARTS, bm, PART_WORDS), jnp.uint32),
                        pltpu.VMEM((2, ROW_PARTS, bm, PART_WORDS), jnp.uint32),
                        pltpu.VMEM((D, EXPERT_DIM), BF16), pltpu.VMEM((D, EXPERT_DIM), BF16),
                        pltpu.VMEM((EXPERT_DIM, D), BF16),
                        pltpu.SemaphoreType.DMA((2, ROW_PARTS)), pltpu.SemaphoreType.DMA((2, ROW_PARTS))])
    return pl.pallas_call(
        _expert_kernel, grid_spec=grid_spec,
        out_shape=[jax.ShapeDtypeStruct((P, PART_WORDS), jnp.uint32)] * ROW_PARTS,
        compiler_params=_cparams("arbitrary"), name="experts")(
            first_blk, blk_count, total_blks, *xs_parts, w_gate, w_up, w_down)


def _pos_kernel(idx_ref, rank_ref, start_ref, pos_ref):
    K, tm = idx_ref.shape
    E = start_ref.shape[0]
    erow = lax.broadcasted_iota(jnp.int32, (E, tm), 0)
    start = start_ref[...]
    for k in range(K):
        first = jnp.sum(jnp.where(erow == idx_ref[k:k + 1, :], start, 0.0), axis=0, keepdims=True)
        pos_ref[k:k + 1, :] = first.astype(jnp.int32) + rank_ref[k:k + 1, :]


def _positions(idx_t, rank_t, start_col):
    K, T = idx_t.shape
    tm = TM_ROUTE
    kt = pl.BlockSpec((K, tm), lambda i: (0, i))
    return pl.pallas_call(
        _pos_kernel, grid=(T // tm,),
        in_specs=[kt, kt, pl.BlockSpec(start_col.shape, lambda i: (0, 0))],
        out_specs=kt, out_shape=jax.ShapeDtypeStruct((K, T), jnp.int32),
        compiler_params=_cparams("parallel"), name="positions")(idx_t, rank_t, start_col)


def _sc_mesh():
    return plsc.VectorSubcoreMesh(core_axis_name="core", subcore_axis_name="subcore",
                                  num_cores=SC_CORES, num_subcores=SC_SUBCORES)


def _sc_scatter_rows(parts, pos, n_out):
    T, W = parts[0].shape
    K = pos.shape[0]
    n = len(parts)
    win = SC_ROW_WINDOW

    @functools.partial(pl.kernel, out_type=[jax.ShapeDtypeStruct((n_out, W), parts[0].dtype)] * n, mesh=_sc_mesh(),
                       name="sc_dispatch")
    def k(*refs):
        x_hbms, p_hbm, o_hbms = refs[:n], refs[n], refs[n + 1:]
        for x_hbm, o_hbm in zip(x_hbms, o_hbms):
            def body(x_vmem, p_vmem, o_hbm=o_hbm):
                for j in range(K):
                    pltpu.sync_copy(x_vmem, o_hbm.at[p_vmem.at[j]])

            pltpu.emit_pipeline(
                body, grid=(T // win,),
                in_specs=[pl.BlockSpec((win, W), lambda i: (i, 0)), pl.BlockSpec((K, win), lambda i: (0, i))],
                out_specs=[], core_axis_name=("core", "subcore"),
                dimension_semantics=(pltpu.PARALLEL,))(x_hbm, p_hbm)

    return k(*parts, pos)


def _sc_gather_rows(tables, idx):
    M = idx.shape[0]
    W = tables[0].shape[1]
    n = len(tables)
    win = SC_ROW_WINDOW

    @functools.partial(pl.kernel, out_type=[jax.ShapeDtypeStruct((M, W), tables[0].dtype)] * n, mesh=_sc_mesh(),
                       name="sc_combine")
    def k(*refs):
        t_hbms, i_hbm, o_hbms = refs[:n], refs[n], refs[n + 1:]
        for t_hbm, o_hbm in zip(t_hbms, o_hbms):
            def body(i_vmem, o_vmem, t_hbm=t_hbm):
                pltpu.sync_copy(t_hbm.at[i_vmem.at[0]], o_vmem)

            pltpu.emit_pipeline(
                body, grid=(M // win,),
                in_specs=[pl.BlockSpec((1, win), lambda i: (0, i))],
                out_specs=[pl.BlockSpec((win, W), lambda i: (i, 0))],
                core_axis_name=("core", "subcore"), dimension_semantics=(pltpu.PARALLEL,))(i_hbm, o_hbm)

    return k(*tables, idx.reshape(1, M))


def _final_kernel(h1_ref, wk_ref, wg_ref, wu_ref, wd_ref, g_ref, b_ref, *refs):
    yk_refs, o_ref = refs[:ROW_PARTS], refs[ROW_PARTS]
    h1 = h1_ref[...]
    hb = h1.astype(BF16)
    g = _dot(hb, wg_ref[...])
    u = _dot(hb, wu_ref[...])
    ffn = _dot((g * _sigmoid(g) * u).astype(BF16), wd_ref[...])
    wk = wk_ref[...]
    sums = None
    for k in range(TOP_K):
        cols = _load_row_parts([r[k] for r in yk_refs])
        terms = [c * wk[:, k:k + 1] for _, c in cols]
        sums = terms if sums is None else [s + t for s, t in zip(sums, terms)]
    ffn = ffn + jnp.concatenate(sums, axis=1)
    o_ref[...] = _layer_norm(DEEPNORM_ALPHA * h1 + ffn, g_ref[...], b_ref[...])


def _final(h1, yk_parts, wk, w_sg, w_su, w_sd, g, b):
    T, D = h1.shape
    tm = TM_FINAL
    row = lambda n: pl.BlockSpec((tm, n), lambda i: (i, 0))
    full = lambda a: pl.BlockSpec(a.shape, lambda i: (0,) * a.ndim)
    return pl.pallas_call(
        _final_kernel, grid=(T // tm,),
        in_specs=[row(D), row(TOP_K), full(w_sg), full(w_su), full(w_sd), full(g), full(b)]
        + [pl.BlockSpec((TOP_K, tm, PART_WORDS), lambda i: (0, i, 0))] * ROW_PARTS,
        out_specs=row(D), out_shape=jax.ShapeDtypeStruct((T, D), F32),
        compiler_params=_cparams("parallel"), name="final")(h1, wk, w_sg, w_su, w_sd, g, b, *yk_parts)


def _block_diag_tri(n, c):
    r = np.arange(n)
    return jnp.asarray(((r[:, None] >= r[None, :]) & (r[:, None] // c == r[None, :] // c)).astype(np.float32), BF16)


def kernel(x, ln_in_g, ln_in_b, w_in, b_fgate, w_gate_up, b_gate, g_gla_norm, w_out, ln1_g, ln1_b, w_router,
           router_bias, w_exp_gate, w_exp_up, w_exp_down, w_sh_gate, w_sh_up, w_sh_down, ln2_g, ln2_b):
    B, S, D = x.shape
    T = B * S
    x2 = x.reshape(T, D)
    l = 0
    row = lambda a: a.reshape(1, -1)

    off = np.cumsum((0,) + IN_SPLITS)
    seg = lambda i: w_in[l][:, off[i]:off[i + 1]]
    w_main = jnp.concatenate([seg(0) * (FOX_HEAD_DIM ** -0.5 * LOG2E), seg(1), seg(2), seg(4) * GLA_KEY_DIM ** -0.5, seg(5),
                              seg(6), seg(8)], axis=1).astype(BF16)
    n_small = FOX_HEADS + GLA_GATE_RANK
    w_small = jnp.concatenate([seg(3), seg(7), jnp.zeros((D, LANES - n_small), F32)], axis=1).astype(BF16)
    bf_pad = jnp.concatenate([b_fgate[l], jnp.zeros((LANES - FOX_HEADS,), F32)]).reshape(1, LANES)
    wg_pad = jnp.zeros((LANES, GLA_KEY_WIDTH), F32).at[FOX_HEADS:n_small].set(w_gate_up[l]).astype(BF16)

    h, fq, fk, fv, gq, gk, gv, gr, small = _proj(x2, row(ln_in_g), row(ln_in_b), w_main, w_small)

    y_fox = _fox(fq, fk, fv, _fgate(small, bf_pad, B, S), B, S)

    y_gla = _gla(gq, gk, gv, gr, small, wg_pad, row(b_gate[l]), row(g_gla_norm[l]),
                 _block_diag_tri(TS_GLA, CHUNK), B, S)

    h1, scores_t, *h1_parts = _oproj(y_fox, y_gla, h, w_out[l].astype(BF16), row(ln1_g[l]), row(ln1_b[l]),
                                     w_router[l].T.astype(BF16))

    tm = TM_ROUTE
    r = np.arange(tm)
    upper = jnp.asarray((r[:, None] < r[None, :]).astype(np.float32), BF16)
    ones = jnp.ones((tm, tm), BF16)
    idx_t, w_t, rank_t, cnt = _route(scores_t, router_bias[l].reshape(N_EXPERTS, 1), upper, ones)

    bm = BM_EXPERT
    A = T * TOP_K
    nb = A // bm + N_EXPERTS
    P = nb * bm
    counts = cnt[:, 0].astype(jnp.int32)
    padded = (counts + bm - 1) // bm * bm
    pend = jnp.cumsum(padded)
    pstart = pend - padded
    pos = _positions(idx_t, rank_t, pstart.astype(F32).reshape(N_EXPERTS, 1))

    xs_parts = _sc_scatter_rows(h1_parts, pos, P)
    ys_parts = _experts(pstart // bm, padded // bm, (pend[-1:] // bm), xs_parts,
                        w_exp_gate[l], w_exp_up[l], w_exp_down[l])
    yk_parts = [y.reshape(TOP_K, T, PART_WORDS) for y in _sc_gather_rows(ys_parts, pos.reshape(-1))]

    out = _final(h1, yk_parts, w_t.T, w_sh_gate[l].astype(BF16), w_sh_up[l].astype(BF16),
                 w_sh_down[l].astype(BF16), row(ln2_g[l]), row(ln2_b[l]))
    return out.reshape(B, S, D)
```

```python
import functools

import jax
import jax.numpy as jnp
import numpy as np
from jax import lax
from jax.experimental import pallas as pl
from jax.experimental.pallas import tpu as pltpu
from jax.experimental.pallas import tpu_sc as plsc

F32 = jnp.float32
BF16 = jnp.bfloat16

D_MODEL = 1024
CHUNK = 64
LN_EPS = 1e-5
LOG2E = 1.4426950408889634
FOX_HEADS = 8
FOX_HEAD_DIM = 64
FOX_WIDTH = FOX_HEADS * FOX_HEAD_DIM
GLA_HEADS = 4
GLA_KEY_DIM = 64
GLA_VALUE_DIM = 128
GLA_KEY_WIDTH = GLA_HEADS * GLA_KEY_DIM
GLA_WIDTH = GLA_HEADS * GLA_VALUE_DIM
GLA_GATE_RANK = 16
GLA_GATE_TEMP = 16.0
N_EXPERTS = 256
N_GROUPS = 8
GROUP_SIZE = N_EXPERTS // N_GROUPS
TOPK_GROUPS = 4
TOP_K = 8
EXPERT_DIM = 256
SHARED_DIM = 256
ROUTED_SCALE = 2.5
DEPTH = 1
DEEPNORM_ALPHA = (2.0 * DEPTH) ** 0.25
IN_SPLITS = (FOX_WIDTH, FOX_WIDTH, FOX_WIDTH, FOX_HEADS, GLA_KEY_WIDTH, GLA_KEY_WIDTH, GLA_WIDTH,
             GLA_GATE_RANK, GLA_WIDTH)

LANES = 128
VMEM_LIMIT = 48 * 1024 * 1024

TM_PROJ = 512
TQ_FOX = 512
TK_FOX = 512
FOX_GROUP = 4
TS_GLA = 512
TM_OPROJ = 512
TM_ROUTE = 512
EXPERT_ROW_BLOCK = 256
EXPERT_UNIT_BLOCKS = 6
EXPERT_VMEM_LIMIT = 56 * 1024 * 1024
TM_FINAL = 512

SC_CORES = 2
SC_SUBCORES = 16
SC_ROW_WINDOW = 128
ROW_PARTS = 2
PART_WORDS = D_MODEL // 2 // ROW_PARTS


def _cparams(*sem):
    return pltpu.CompilerParams(dimension_semantics=sem, vmem_limit_bytes=VMEM_LIMIT)


def _layer_norm(x, g, b):
    mu = jnp.mean(x, axis=-1, keepdims=True)
    xc = x - mu
    var = jnp.mean(xc * xc, axis=-1, keepdims=True)
    return xc * lax.rsqrt(var + LN_EPS) * g + b


def _log_sigmoid(z):
    return jnp.minimum(z, 0.0) - jnp.log1p(jnp.exp(-jnp.abs(z)))


def _sigmoid(z):
    return 1.0 / (1.0 + jnp.exp(-z))


def _dot(a, b):
    return jnp.dot(a, b, preferred_element_type=F32)


def _dot_nt(a, b):
    return lax.dot_general(a, b, (((1,), (1,)), ((), ())), preferred_element_type=F32)


def _dot_tn(a, b):
    return lax.dot_general(a, b, (((0,), (0,)), ((), ())), preferred_element_type=F32)


def _pack_bf16_pairs(x):
    n = x.shape[1] // 2
    u = lax.bitcast_convert_type(x.astype(BF16).astype(F32), jnp.uint32)
    return (u[:, :n] >> 16) | u[:, n:]


def _unpack_bf16_pairs(w):
    lo = lax.bitcast_convert_type(w << 16, F32)
    hi = lax.bitcast_convert_type(w & jnp.uint32(0xFFFF0000), F32)
    return lo, hi


def _store_row_parts(x, part_refs):
    packed = _pack_bf16_pairs(x)
    for p, ref in enumerate(part_refs):
        ref[...] = packed[:, p * PART_WORDS:(p + 1) * PART_WORDS]


def _load_row_parts(parts):
    out = []
    for p, w in enumerate(parts):
        lo, hi = _unpack_bf16_pairs(w)
        out.append((p * PART_WORDS, lo))
        out.append((D_MODEL // 2 + p * PART_WORDS, hi))
    return sorted(out, key=lambda t: t[0])


def _proj_kernel(x_ref, g_ref, b_ref, wm_ref, ws_ref,
                 h_ref, fq_ref, fk_ref, fv_ref, gq_ref, gk_ref, gv_ref, gr_ref, sm_ref):
    h = _layer_norm(x_ref[...], g_ref[...], b_ref[...])
    h_ref[...] = h
    hb = h.astype(BF16)
    off = 0
    for ref in (fq_ref, fk_ref, fv_ref, gq_ref, gk_ref, gv_ref, gr_ref):
        n = ref.shape[1]
        ref[...] = _dot(hb, wm_ref[:, off:off + n]).astype(ref.dtype)
        off += n
    sm_ref[...] = _dot(hb, ws_ref[...])


def _proj(x2, g, b, w_main, w_small):
    T, D = x2.shape
    tm = TM_PROJ
    widths = (FOX_WIDTH, FOX_WIDTH, FOX_WIDTH, GLA_KEY_WIDTH, GLA_KEY_WIDTH, GLA_WIDTH, GLA_WIDTH)
    row = lambda n: pl.BlockSpec((tm, n), lambda i: (i, 0))
    full = lambda a: pl.BlockSpec(a.shape, lambda i: (0,) * a.ndim)
    out_shape = [jax.ShapeDtypeStruct((T, D), F32)]
    out_shape += [jax.ShapeDtypeStruct((T, n), BF16) for n in widths]
    out_shape += [jax.ShapeDtypeStruct((T, LANES), F32)]
    out_specs = [row(D)] + [row(n) for n in widths] + [row(LANES)]
    return pl.pallas_call(
        _proj_kernel, grid=(T // tm,),
        in_specs=[row(D), full(g), full(b), full(w_main), full(w_small)],
        out_specs=out_specs, out_shape=out_shape,
        compiler_params=_cparams("parallel"), name="proj")(x2, g, b, w_main, w_small)


def _fgate_kernel(sm_ref, bf_ref, c_ref):
    S = sm_ref.shape[0]
    lf = _log_sigmoid(sm_ref[...] + bf_ref[...])
    rows = lax.broadcasted_iota(jnp.int32, lf.shape, 0)
    s = 1
    while s < S:
        lf = lf + jnp.where(rows >= s, pltpu.roll(lf, s, axis=0), 0.0)
        s *= 2
    c_ref[...] = lf * LOG2E


def _fgate(small, bf_pad, B, S):
    T = small.shape[0]
    return pl.pallas_call(
        _fgate_kernel, grid=(B,),
        in_specs=[pl.BlockSpec((S, LANES), lambda b: (b, 0)),
                  pl.BlockSpec((1, LANES), lambda b: (0, 0))],
        out_specs=pl.BlockSpec((S, LANES), lambda b: (b, 0)),
        out_shape=jax.ShapeDtypeStruct((T, LANES), F32),
        compiler_params=_cparams("parallel"), name="fgate")(small, bf_pad)


def _split3(x):
    hi = x.astype(BF16).astype(F32)
    r = x - hi
    mid = r.astype(BF16).astype(F32)
    return hi, mid, (r - mid).astype(BF16).astype(F32)


def _fox_operand(pair_ref_tile, odd, c_col, is_key):
    dh = FOX_HEAD_DIM
    x = pair_ref_tile.astype(F32)
    if odd:
        x = pltpu.roll(x, dh, axis=1)
    lane = lax.broadcasted_iota(jnp.int32, x.shape, 1)
    t0, t1, t2 = _split3(-c_col if is_key else c_col)
    first = dh + 3 if is_key else dh
    bias = jnp.where(lane == first, t0, jnp.where(lane == first + 1, t1, jnp.where(lane == first + 2, t2, 0.0)))
    ones_lo = dh if is_key else dh + 3
    bias = jnp.where((lane >= ones_lo) & (lane < ones_lo + 3), 1.0, bias)
    return jnp.where(lane < dh, x, bias).astype(BF16)


def _fox_kernel(q_ref, k_ref, v_ref, cq_ref, ck_ref, o_ref, ka_ref, vt_ref, *, tq, tk):
    i = pl.program_id(1)
    S = k_ref.shape[0]
    dh = FOX_HEAD_DIM

    @pl.when(i == 0)
    def _():
        def fill(r, carry):
            r0 = pl.multiple_of(r * tk, tk)
            for h in range(FOX_HEADS):
                pair = slice(LANES * (h // 2), LANES * (h // 2 + 1))
                ka_ref[pl.ds(r0, tk), LANES * h:LANES * (h + 1)] = _fox_operand(
                    k_ref[pl.ds(r0, tk), pair], h % 2, ck_ref[pl.ds(r0, tk), h:h + 1], True)
            vt_ref[r] = v_ref[pl.ds(r0, tk), :].astype(F32).T.astype(BF16)
            return carry
        lax.fori_loop(0, S // tk, fill, 0)

    key = lax.broadcasted_iota(jnp.int32, (tk, tq), 0)
    qry = lax.broadcasted_iota(jnp.int32, (tk, tq), 1)
    n_diag = tq // tk
    n_full = i * n_diag

    for p in range(FOX_HEADS // FOX_GROUP):
        group = slice(dh * FOX_GROUP * p, dh * FOX_GROUP * (p + 1))
        heads = tuple(range(FOX_GROUP * p, FOX_GROUP * (p + 1)))
        qa = [_fox_operand(q_ref[:, LANES * (h // 2):LANES * (h // 2 + 1)], h % 2, cq_ref[:, h:h + 1], False)
              for h in heads]

        def step(j, carry, diag):
            r0 = pl.multiple_of(j * tk, tk)
            ss = [_dot_nt(ka_ref[pl.ds(r0, tk), LANES * h:LANES * (h + 1)], qa[n])
                  for n, h in enumerate(heads)]
            if diag is not None:
                ss = [jnp.where(qry >= key + diag * tk, s, -jnp.inf) for s in ss]
            ms = [jnp.maximum(c[0], jnp.max(s, axis=0, keepdims=True)) for c, s in zip(carry, ss)]
            prs = [jnp.exp2(s - m) for s, m in zip(ss, ms)]
            alphas = [jnp.exp2(c[0] - m) for c, m in zip(carry, ms)]
            ls = [a * c[1] + jnp.sum(pr, axis=0, keepdims=True) for a, c, pr in zip(alphas, carry, prs)]
            pvs = [_dot(vt_ref[j, dh * h:dh * (h + 1), :], pr.astype(BF16)) for h, pr in zip(heads, prs)]
            accs = [a * c[2] + pv for a, c, pv in zip(alphas, carry, pvs)]
            return tuple(zip(ms, ls, accs))

        init = tuple((jnp.full((1, tq), -jnp.inf, F32), jnp.zeros((1, tq), F32), jnp.zeros((dh, tq), F32))
                     for _ in heads)
        carry = lax.fori_loop(0, n_full, functools.partial(step, diag=None), init)
        for d in range(n_diag):
            carry = step(n_full + d, carry, d)
        o_ref[:, group] = jnp.concatenate([(acc / l).T for _, l, acc in carry], axis=1).astype(o_ref.dtype)


def _fox(fq, fk, fv, c, B, S):
    T = fq.shape[0]
    tq, tk = TQ_FOX, TK_FOX
    nq = S // tq
    return pl.pallas_call(
        functools.partial(_fox_kernel, tq=tq, tk=tk), grid=(B, nq),
        in_specs=[pl.BlockSpec((tq, FOX_WIDTH), lambda b, i: (b * nq + i, 0)),
                  pl.BlockSpec((S, FOX_WIDTH), lambda b, i: (b, 0)),
                  pl.BlockSpec((S, FOX_WIDTH), lambda b, i: (b, 0)),
                  pl.BlockSpec((tq, LANES), lambda b, i: (b * nq + i, 0)),
                  pl.BlockSpec((S, LANES), lambda b, i: (b, 0))],
        out_specs=pl.BlockSpec((tq, FOX_WIDTH), lambda b, i: (b * nq + i, 0)),
        out_shape=jax.ShapeDtypeStruct((T, FOX_WIDTH), BF16),
        scratch_shapes=[pltpu.VMEM((S, FOX_HEADS * LANES), BF16), pltpu.VMEM((S // tk, FOX_WIDTH, tk), BF16)],
        compiler_params=_cparams("parallel", "arbitrary"), name="fox")(fq, fk, fv, c, c)


def _gla_kernel(q_ref, k_ref, v_ref, r_ref, sm_ref, wg_ref, bg_ref, gn_ref, tri_ref, y_ref,
                st_ref, b_ref, o_ref, *, ts):
    dk, dv, C = GLA_KEY_DIM, GLA_VALUE_DIM, CHUNK

    @pl.when(pl.program_id(1) == 0)
    def _():
        st_ref[...] = jnp.zeros_like(st_ref)

    z = _dot(sm_ref[...].astype(BF16), wg_ref[...]) + bg_ref[...]
    la = _log_sigmoid(z) * (1.0 / GLA_GATE_TEMP)
    hi = la.astype(BF16)
    r1 = la - hi.astype(F32)
    mid = r1.astype(BF16)
    lo = (r1 - mid.astype(F32)).astype(BF16)
    tri = tri_ref[...]
    b_ref[...] = _dot(tri, hi) + _dot(tri, mid) + _dot(tri, lo)

    row = lax.broadcasted_iota(jnp.int32, (C, C), 0)
    col = lax.broadcasted_iota(jnp.int32, (C, C), 1)
    trow = lax.broadcasted_iota(jnp.int32, (C, LANES), 0)
    lane = lax.broadcasted_iota(jnp.int32, (C, LANES), 1)
    head_lanes = [lane < dk, lane >= dk]

    def level_reference(b, m):
        if m >= 4:
            n = 2 * m
            return jnp.concatenate(
                [jnp.broadcast_to(b[s + m - 1:s + m, :], (n, LANES)) for s in range(0, C, n)], axis=0)
        up1 = pltpu.roll(b, 1, axis=0)
        if m == 1:
            return jnp.where(trow % 2 == 0, b, up1)
        up2 = pltpu.roll(b, 2, axis=0)
        down1 = pltpu.roll(b, C - 1, axis=0)
        tm = trow % 4
        return jnp.where(tm == 0, down1, jnp.where(tm == 1, b, jnp.where(tm == 2, up1, up2)))

    def chunk(c, carry):
        r0 = pl.multiple_of(c * C, C)
        for p in range(GLA_HEADS // 2):
            kl = slice(p * LANES, (p + 1) * LANES)
            q = q_ref[pl.ds(r0, C), kl].astype(F32)
            k = k_ref[pl.ds(r0, C), kl].astype(F32)
            b = b_ref[pl.ds(r0, C), kl]
            st = st_ref[p]
            stb = st.astype(BF16)
            b_last = b[C - 1:C, :]
            qb, kb = q.astype(BF16), k.astype(BF16)
            q_in = (q * jnp.exp(b)).astype(BF16)
            zero = jnp.zeros_like(qb)

            scores = [jnp.where(row == col, _dot_nt(jnp.where(hl, qb, zero), kb), 0.0) for hl in head_lanes]
            m = C // 2
            while m >= 1:
                ref = level_reference(b, m)
                upper = (trow % (2 * m)) >= m
                qt = jnp.where(upper, q * jnp.exp(jnp.minimum(b - ref, 0.0)), 0.0).astype(BF16)
                kt = jnp.where(upper, 0.0, k * jnp.exp(jnp.minimum(ref - b, 0.0))).astype(BF16)
                same_block = (row // (2 * m)) == (col // (2 * m))
                scores = [sc + jnp.where(same_block, _dot_nt(jnp.where(hl, qt, zero), kt), 0.0)
                          for sc, hl in zip(scores, head_lanes)]
                m //= 2

            kh = (k * jnp.exp(b_last - b)).astype(BF16)
            upd = []
            for n, hl in enumerate(head_lanes):
                h = 2 * p + n
                vl = slice(h * dv, (h + 1) * dv)
                v = v_ref[pl.ds(r0, C), vl]
                o_ref[pl.ds(r0, C), vl] = (_dot_nt(jnp.where(hl, q_in, zero), stb)
                                           + _dot(scores[n].astype(BF16), v))
                upd.append(_dot_tn(v, kh))
            st_ref[p] = st * jnp.exp(b_last) + jnp.where(lax.broadcasted_iota(jnp.int32, st.shape, 1) < dk,
                                                         upd[0], upd[1])
        return carry

    lax.fori_loop(0, ts // C, chunk, 0)

    for h in range(GLA_HEADS):
        vl = slice(h * dv, (h + 1) * dv)
        o = o_ref[:, vl]
        ms = jnp.mean(o * o, axis=-1, keepdims=True)
        g = r_ref[:, vl].astype(F32)
        y = o * lax.rsqrt(ms + LN_EPS) * gn_ref[...] * (g * _sigmoid(g))
        y_ref[:, vl] = y.astype(y_ref.dtype)


def _gla(gq, gk, gv, gr, small, wg_pad, bg, gnorm, tri, B, S):
    T = gq.shape[0]
    ts = TS_GLA
    ns = S // ts
    row = lambda n: pl.BlockSpec((ts, n), lambda b, i: (b * ns + i, 0))
    full = lambda a: pl.BlockSpec(a.shape, lambda b, i: (0,) * a.ndim)
    return pl.pallas_call(
        functools.partial(_gla_kernel, ts=ts), grid=(B, ns),
        in_specs=[row(GLA_KEY_WIDTH), row(GLA_KEY_WIDTH), row(GLA_WIDTH), row(GLA_WIDTH), row(LANES),
                  full(wg_pad), full(bg), full(gnorm), full(tri)],
        out_specs=row(GLA_WIDTH),
        out_shape=jax.ShapeDtypeStruct((T, GLA_WIDTH), BF16),
        scratch_shapes=[pltpu.VMEM((GLA_HEADS // 2, GLA_VALUE_DIM, LANES), F32),
                        pltpu.VMEM((ts, GLA_KEY_WIDTH), F32),
                        pltpu.VMEM((ts, GLA_WIDTH), F32)],
        compiler_params=_cparams("parallel", "arbitrary"), name="gla")(
            gq, gk, gv, gr, small, wg_pad, bg, gnorm, tri)


def _oproj_kernel(yf_ref, yg_ref, h_ref, wo_ref, g_ref, b_ref, wr_ref, h1_ref, st_ref, *part_refs):
    mix = _dot(yf_ref[...], wo_ref[0:FOX_WIDTH, :]) + _dot(yg_ref[...], wo_ref[FOX_WIDTH:, :])
    h1 = _layer_norm(DEEPNORM_ALPHA * h_ref[...] + mix, g_ref[...], b_ref[...])
    h1_ref[...] = h1
    _store_row_parts(h1, part_refs)
    st_ref[...] = _sigmoid(_dot_nt(wr_ref[...], h1.astype(BF16)))


def _oproj(y_fox, y_gla, h, w_out, g, b, w_router_t):
    T, D = h.shape
    tm = TM_OPROJ
    row = lambda n: pl.BlockSpec((tm, n), lambda i: (i, 0))
    full = lambda a: pl.BlockSpec(a.shape, lambda i: (0,) * a.ndim)
    return pl.pallas_call(
        _oproj_kernel, grid=(T // tm,),
        in_specs=[row(FOX_WIDTH), row(GLA_WIDTH), row(D), full(w_out), full(g), full(b), full(w_router_t)],
        out_specs=[row(D), pl.BlockSpec((N_EXPERTS, tm), lambda i: (0, i))] + [row(PART_WORDS)] * ROW_PARTS,
        out_shape=[jax.ShapeDtypeStruct((T, D), F32), jax.ShapeDtypeStruct((N_EXPERTS, T), F32)]
        + [jax.ShapeDtypeStruct((T, PART_WORDS), jnp.uint32)] * ROW_PARTS,
        compiler_params=_cparams("parallel"), name="oproj")(y_fox, y_gla, h, w_out, g, b, w_router_t)


def _route_kernel(s_ref, bias_ref, upper_ref, ones_ref, idx_ref, w_ref, rank_ref, cnt_ref, carry_ref):
    E, tm = s_ref.shape

    @pl.when(pl.program_id(0) == 0)
    def _():
        carry_ref[...] = jnp.zeros_like(carry_ref)

    s = s_ref[...]
    biased = s + bias_ref[...]
    neg = -jnp.inf
    erow = lax.broadcasted_iota(jnp.int32, (E, tm), 0).astype(F32)
    grow = lax.broadcasted_iota(jnp.int32, (GROUP_SIZE, tm), 0).astype(F32)

    gs = []
    for g in range(N_GROUPS):
        blk = biased[g * GROUP_SIZE:(g + 1) * GROUP_SIZE, :]
        m1 = jnp.max(blk, axis=0, keepdims=True)
        i1 = jnp.min(jnp.where(blk == m1, grow, float(GROUP_SIZE)), axis=0, keepdims=True)
        m2 = jnp.max(jnp.where(grow == i1, neg, blk), axis=0, keepdims=True)
        gs.append(m1 + m2)
    keep = []
    for g in range(N_GROUPS):
        beaten = jnp.zeros((1, tm), F32)
        for o in range(N_GROUPS):
            if o == g:
                continue
            wins = (gs[o] > gs[g]) | ((gs[o] == gs[g]) & (o < g))
            beaten = beaten + jnp.where(wins, 1.0, 0.0)
        keep.append(jnp.broadcast_to(beaten < float(TOPK_GROUPS), (GROUP_SIZE, tm)))
    cur = jnp.where(jnp.concatenate(keep, axis=0), biased, neg)

    ids, ws, hots = [], [], []
    chosen = jnp.zeros((E, tm), F32)
    for _ in range(TOP_K):
        m = jnp.max(cur, axis=0, keepdims=True)
        ik = jnp.min(jnp.where(cur == m, erow, float(E)), axis=0, keepdims=True)
        hot = erow == ik
        ws.append(jnp.sum(jnp.where(hot, s, 0.0), axis=0, keepdims=True))
        cur = jnp.where(hot, neg, cur)
        chosen = jnp.where(hot, 1.0, chosen)
        ids.append(ik)
        hots.append(hot)
    wsum = ws[0]
    for w in ws[1:]:
        wsum = wsum + w

    chosen_b = chosen.astype(BF16)
    before = carry_ref[...] + _dot(chosen_b, upper_ref[...])
    carry_ref[...] = carry_ref[...] + _dot(chosen_b, ones_ref[...])
    for k in range(TOP_K):
        idx_ref[k:k + 1, :] = ids[k].astype(jnp.int32)
        w_ref[k:k + 1, :] = ws[k] / wsum * ROUTED_SCALE
        rank_ref[k:k + 1, :] = jnp.sum(jnp.where(hots[k], before, 0.0), axis=0, keepdims=True).astype(jnp.int32)
    cnt_ref[...] = carry_ref[:, 0:LANES]


def _route(scores_t, bias_col, upper, ones):
    E, T = scores_t.shape
    tm = TM_ROUTE
    full = lambda a: pl.BlockSpec(a.shape, lambda i: (0,) * a.ndim)
    kt = pl.BlockSpec((TOP_K, tm), lambda i: (0, i))
    return pl.pallas_call(
        _route_kernel, grid=(T // tm,),
        in_specs=[pl.BlockSpec((E, tm), lambda i: (0, i)), full(bias_col), full(upper), full(ones)],
        out_specs=[kt, kt, kt, pl.BlockSpec((E, LANES), lambda i: (0, 0))],
        out_shape=[jax.ShapeDtypeStruct((TOP_K, T), jnp.int32), jax.ShapeDtypeStruct((TOP_K, T), F32),
                   jax.ShapeDtypeStruct((TOP_K, T), jnp.int32), jax.ShapeDtypeStruct((E, LANES), F32)],
        scratch_shapes=[pltpu.VMEM((E, tm), F32)],
        compiler_params=_cparams("arbitrary"), name="route")(scores_t, bias_col, upper, ones)


def _expert_kernel(ufirst_ref, ucount_ref, ustart_ref, usize_ref, total_ref, *refs):
    x_hbm = refs[:ROW_PARTS]
    wg_ref, wu_ref, wd_ref = refs[ROW_PARTS:ROW_PARTS + 3]
    y_hbm = refs[ROW_PARTS + 3:2 * ROW_PARTS + 3]
    xbuf, ybuf, wgb, wub, wdb, in_sem, out_sem = refs[2 * ROW_PARTS + 3:]
    e = pl.program_id(0)
    blk = EXPERT_ROW_BLOCK
    total = total_ref[0]

    def hbm_rows(u, k):
        return pl.ds(pl.multiple_of(ustart_ref[u] * blk, blk), k * blk)

    def in_copy(u, k, p):
        return pltpu.make_async_copy(x_hbm[p].at[hbm_rows(u, k)], xbuf.at[u % 2, p, pl.ds(0, k * blk)],
                                     in_sem.at[u % 2, p])

    def out_copy(u, k, p):
        return pltpu.make_async_copy(ybuf.at[u % 2, p, pl.ds(0, k * blk)], y_hbm[p].at[hbm_rows(u, k)],
                                     out_sem.at[u % 2, p])

    def for_size(u, fn):
        for k in range(1, EXPERT_UNIT_BLOCKS + 1):
            pl.when(usize_ref[u] == k)(functools.partial(fn, k))

    def start_read(u):
        def go(k):
            for p in range(ROW_PARTS):
                in_copy(u, k, p).start()
        for_size(u, go)

    def wait_read(u):
        def go(k):
            for p in range(ROW_PARTS):
                in_copy(u, k, p).wait()
        for_size(u, go)

    def wait_write(u):
        def go(k):
            for p in range(ROW_PARTS):
                out_copy(u, k, p).wait()
        for_size(u, go)

    def compute(u, k):
        slot = u % 2
        rows = pl.ds(0, k * blk)
        gate = up = None
        for c0, xc in _load_row_parts([xbuf[slot, p, rows] for p in range(ROW_PARTS)]):
            xb = xc.astype(BF16)
            wrows = slice(c0, c0 + PART_WORDS)
            gc = _dot(xb, wgb[wrows, :])
            uc = _dot(xb, wub[wrows, :])
            gate = gc if gate is None else gate + gc
            up = uc if up is None else up + uc
        y = _dot((gate * _sigmoid(gate) * up).astype(BF16), wdb[...])
        _store_row_parts(y, [ybuf.at[slot, p, rows] for p in range(ROW_PARTS)])
        for p in range(ROW_PARTS):
            out_copy(u, k, p).start()

    @pl.when((e == 0) & (total > 0))
    def _():
        start_read(0)

    @pl.when(ucount_ref[e] > 0)
    def _():
        wgb[...] = wg_ref[...].astype(BF16)
        wub[...] = wu_ref[...].astype(BF16)
        wdb[...] = wd_ref[...].astype(BF16)

        def unit(c, carry):
            u = ufirst_ref[e] + c
            wait_read(u)
            pl.when(u + 1 < total)(lambda: start_read(u + 1))
            pl.when(u >= 2)(lambda: wait_write(u - 2))
            for_size(u, lambda k: compute(u, k))
            return carry

        lax.fori_loop(0, ucount_ref[e], unit, 0)

    @pl.when(e == pl.num_programs(0) - 1)
    def _():
        for back in (2, 1):
            pl.when(total >= back)(lambda back=back: wait_write(total - back))


def _experts(ufirst, ucount, ustart, usize, total, xs_parts, w_gate, w_up, w_down):
    P = xs_parts[0].shape[0]
    D = D_MODEL
    unit_rows = EXPERT_UNIT_BLOCKS * EXPERT_ROW_BLOCK
    hbm = pl.BlockSpec(memory_space=pl.ANY)
    grid_spec = pltpu.PrefetchScalarGridSpec(
        num_scalar_prefetch=5, grid=(N_EXPERTS,),
        in_specs=[hbm] * ROW_PARTS + [
            pl.BlockSpec((None, D, EXPERT_DIM), lambda e, *_: (e, 0, 0)),
            pl.BlockSpec((None, D, EXPERT_DIM), lambda e, *_: (e, 0, 0)),
            pl.BlockSpec((None, EXPERT_DIM, D), lambda e, *_: (e, 0, 0))],
        out_specs=[hbm] * ROW_PARTS,
        scratch_shapes=[pltpu.VMEM((2, ROW_PARTS, unit_rows, PART_WORDS), jnp.uint32),
                        pltpu.VMEM((2, ROW_PARTS, unit_rows, PART_WORDS), jnp.uint32),
                        pltpu.VMEM((D, EXPERT_DIM), BF16), pltpu.VMEM((D, EXPERT_DIM), BF16),
                        pltpu.VMEM((EXPERT_DIM, D), BF16),
                        pltpu.SemaphoreType.DMA((2, ROW_PARTS)), pltpu.SemaphoreType.DMA((2, ROW_PARTS))])
    return pl.pallas_call(
        _expert_kernel, grid_spec=grid_spec,
        out_shape=[jax.ShapeDtypeStruct((P, PART_WORDS), jnp.uint32)] * ROW_PARTS,
        compiler_params=pltpu.CompilerParams(dimension_semantics=("arbitrary",),
                                             vmem_limit_bytes=EXPERT_VMEM_LIMIT), name="experts")(
            ufirst, ucount, ustart, usize, total, *xs_parts, w_gate, w_up, w_down)


def _pos_kernel(idx_ref, rank_ref, start_ref, pos_ref):
    K, tm = idx_ref.shape
    E = start_ref.shape[0]
    erow = lax.broadcasted_iota(jnp.int32, (E, tm), 0)
    start = start_ref[...]
    for k in range(K):
        first = jnp.sum(jnp.where(erow == idx_ref[k:k + 1, :], start, 0.0), axis=0, keepdims=True)
        pos_ref[k:k + 1, :] = first.astype(jnp.int32) + rank_ref[k:k + 1, :]


def _positions(idx_t, rank_t, start_col):
    K, T = idx_t.shape
    tm = TM_ROUTE
    kt = pl.BlockSpec((K, tm), lambda i: (0, i))
    return pl.pallas_call(
        _pos_kernel, grid=(T // tm,),
        in_specs=[kt, kt, pl.BlockSpec(start_col.shape, lambda i: (0, 0))],
        out_specs=kt, out_shape=jax.ShapeDtypeStruct((K, T), jnp.int32),
        compiler_params=_cparams("parallel"), name="positions")(idx_t, rank_t, start_col)


def _sc_mesh():
    return plsc.VectorSubcoreMesh(core_axis_name="core", subcore_axis_name="subcore",
                                  num_cores=SC_CORES, num_subcores=SC_SUBCORES)


def _sc_scatter_rows(parts, pos, n_out):
    T, W = parts[0].shape
    K = pos.shape[0]
    n = len(parts)
    win = SC_ROW_WINDOW

    @functools.partial(pl.kernel, out_type=[jax.ShapeDtypeStruct((n_out, W), parts[0].dtype)] * n, mesh=_sc_mesh(),
                       name="sc_dispatch")
    def k(*refs):
        x_hbms, p_hbm, o_hbms = refs[:n], refs[n], refs[n + 1:]
        for x_hbm, o_hbm in zip(x_hbms, o_hbms):
            def body(x_vmem, p_vmem, o_hbm=o_hbm):
                for j in range(K):
                    pltpu.sync_copy(x_vmem, o_hbm.at[p_vmem.at[j]])

            pltpu.emit_pipeline(
                body, grid=(T // win,),
                in_specs=[pl.BlockSpec((win, W), lambda i: (i, 0)), pl.BlockSpec((K, win), lambda i: (0, i))],
                out_specs=[], core_axis_name=("core", "subcore"),
                dimension_semantics=(pltpu.PARALLEL,))(x_hbm, p_hbm)

    return k(*parts, pos)


def _sc_gather_rows(tables, idx):
    M = idx.shape[0]
    W = tables[0].shape[1]
    n = len(tables)
    win = SC_ROW_WINDOW

    @functools.partial(pl.kernel, out_type=[jax.ShapeDtypeStruct((M, W), tables[0].dtype)] * n, mesh=_sc_mesh(),
                       name="sc_combine")
    def k(*refs):
        t_hbms, i_hbm, o_hbms = refs[:n], refs[n], refs[n + 1:]
        for t_hbm, o_hbm in zip(t_hbms, o_hbms):
            def body(i_vmem, o_vmem, t_hbm=t_hbm):
                pltpu.sync_copy(t_hbm.at[i_vmem.at[0]], o_vmem)

            pltpu.emit_pipeline(
                body, grid=(M // win,),
                in_specs=[pl.BlockSpec((1, win), lambda i: (0, i))],
                out_specs=[pl.BlockSpec((win, W), lambda i: (i, 0))],
                core_axis_name=("core", "subcore"), dimension_semantics=(pltpu.PARALLEL,))(i_hbm, o_hbm)

    return k(*tables, idx.reshape(1, M))


def _final_kernel(h1_ref, wk_ref, wg_ref, wu_ref, wd_ref, g_ref, b_ref, *refs):
    yk_refs, o_ref = refs[:ROW_PARTS], refs[ROW_PARTS]
    h1 = h1_ref[...]
    hb = h1.astype(BF16)
    g = _dot(hb, wg_ref[...])
    u = _dot(hb, wu_ref[...])
    ffn = _dot((g * _sigmoid(g) * u).astype(BF16), wd_ref[...])
    wk = wk_ref[...]
    sums = None
    for k in range(TOP_K):
        cols = _load_row_parts([r[k] for r in yk_refs])
        terms = [c * wk[:, k:k + 1] for _, c in cols]
        sums = terms if sums is None else [s + t for s, t in zip(sums, terms)]
    ffn = ffn + jnp.concatenate(sums, axis=1)
    o_ref[...] = _layer_norm(DEEPNORM_ALPHA * h1 + ffn, g_ref[...], b_ref[...])


def _final(h1, yk_parts, wk, w_sg, w_su, w_sd, g, b):
    T, D = h1.shape
    tm = TM_FINAL
    row = lambda n: pl.BlockSpec((tm, n), lambda i: (i, 0))
    full = lambda a: pl.BlockSpec(a.shape, lambda i: (0,) * a.ndim)
    return pl.pallas_call(
        _final_kernel, grid=(T // tm,),
        in_specs=[row(D), row(TOP_K), full(w_sg), full(w_su), full(w_sd), full(g), full(b)]
        + [pl.BlockSpec((TOP_K, tm, PART_WORDS), lambda i: (0, i, 0))] * ROW_PARTS,
        out_specs=row(D), out_shape=jax.ShapeDtypeStruct((T, D), F32),
        compiler_params=_cparams("parallel"), name="final")(h1, wk, w_sg, w_su, w_sd, g, b, *yk_parts)


def _block_diag_tri(n, c):
    r = np.arange(n)
    return jnp.asarray(((r[:, None] >= r[None, :]) & (r[:, None] // c == r[None, :] // c)).astype(np.float32), BF16)


def kernel(x, ln_in_g, ln_in_b, w_in, b_fgate, w_gate_up, b_gate, g_gla_norm, w_out, ln1_g, ln1_b, w_router,
           router_bias, w_exp_gate, w_exp_up, w_exp_down, w_sh_gate, w_sh_up, w_sh_down, ln2_g, ln2_b):
    B, S, D = x.shape
    T = B * S
    x2 = x.reshape(T, D)
    l = 0
    row = lambda a: a.reshape(1, -1)

    off = np.cumsum((0,) + IN_SPLITS)
    seg = lambda i: w_in[l][:, off[i]:off[i + 1]]
    w_main = jnp.concatenate([seg(0) * (FOX_HEAD_DIM ** -0.5 * LOG2E), seg(1), seg(2), seg(4) * GLA_KEY_DIM ** -0.5, seg(5),
                              seg(6), seg(8)], axis=1).astype(BF16)
    n_small = FOX_HEADS + GLA_GATE_RANK
    w_small = jnp.concatenate([seg(3), seg(7), jnp.zeros((D, LANES - n_small), F32)], axis=1).astype(BF16)
    bf_pad = jnp.concatenate([b_fgate[l], jnp.zeros((LANES - FOX_HEADS,), F32)]).reshape(1, LANES)
    wg_pad = jnp.zeros((LANES, GLA_KEY_WIDTH), F32).at[FOX_HEADS:n_small].set(w_gate_up[l]).astype(BF16)

    h, fq, fk, fv, gq, gk, gv, gr, small = _proj(x2, row(ln_in_g), row(ln_in_b), w_main, w_small)

    y_fox = _fox(fq, fk, fv, _fgate(small, bf_pad, B, S), B, S)

    y_gla = _gla(gq, gk, gv, gr, small, wg_pad, row(b_gate[l]), row(g_gla_norm[l]),
                 _block_diag_tri(TS_GLA, CHUNK), B, S)

    h1, scores_t, *h1_parts = _oproj(y_fox, y_gla, h, w_out[l].astype(BF16), row(ln1_g[l]), row(ln1_b[l]),
                                     w_router[l].T.astype(BF16))

    tm = TM_ROUTE
    r = np.arange(tm)
    upper = jnp.asarray((r[:, None] < r[None, :]).astype(np.float32), BF16)
    ones = jnp.ones((tm, tm), BF16)
    idx_t, w_t, rank_t, cnt = _route(scores_t, router_bias[l].reshape(N_EXPERTS, 1), upper, ones)

    blk, ub, E = EXPERT_ROW_BLOCK, EXPERT_UNIT_BLOCKS, N_EXPERTS
    A = T * TOP_K
    P = A + E * blk
    counts = cnt[:, 0].astype(jnp.int32)
    nblk = (counts + blk - 1) // blk
    blk_end = jnp.cumsum(nblk)
    blk_start = blk_end - nblk
    pos = _positions(idx_t, rank_t, (blk_start * blk).astype(F32).reshape(E, 1))
    nunit = (nblk + ub - 1) // ub
    unit_end = jnp.cumsum(nunit)
    ufirst = unit_end - nunit
    u = jnp.arange(E + (A // blk + E) // ub + 1, dtype=jnp.int32)
    owner = jnp.minimum(jnp.sum(unit_end[None, :] <= u[:, None], axis=1), E - 1)
    onehot = owner[:, None] == jnp.arange(E)[None, :]
    pick = lambda v: jnp.sum(jnp.where(onehot, v[None, :], 0), axis=1)
    j = u - pick(ufirst)
    n_units = jnp.maximum(pick(nunit), 1)
    base, rem = pick(nblk) // n_units, pick(nblk) % n_units
    live = u < unit_end[-1]
    usize = jnp.where(live, base + (j < rem), 1).astype(jnp.int32)
    ustart = jnp.where(live, pick(blk_start) + j * base + jnp.minimum(j, rem), 0).astype(jnp.int32)

    xs_parts = _sc_scatter_rows(h1_parts, pos, P)
    ys_parts = _experts(ufirst, nunit, ustart, usize, unit_end[-1:], xs_parts,
                        w_exp_gate[l], w_exp_up[l], w_exp_down[l])
    yk_parts = [y.reshape(TOP_K, T, PART_WORDS) for y in _sc_gather_rows(ys_parts, pos.reshape(-1))]

    out = _final(h1, yk_parts, w_t.T, w_sh_gate[l].astype(BF16), w_sh_up[l].astype(BF16),
                 w_sh_down[l].astype(BF16), row(ln2_g[l]), row(ln2_b[l]))
    return out.reshape(B, S, D)
```

```python
import functools

import jax
import jax.numpy as jnp
import numpy as np
from jax import lax
from jax.experimental import pallas as pl
from jax.experimental.pallas import tpu as pltpu
from jax.experimental.pallas import tpu_sc as plsc

F32 = jnp.float32
BF16 = jnp.bfloat16

D_MODEL = 1024
CHUNK = 64
LN_EPS = 1e-5
LOG2E = 1.4426950408889634
FOX_HEADS = 8
FOX_HEAD_DIM = 64
FOX_WIDTH = FOX_HEADS * FOX_HEAD_DIM
GLA_HEADS = 4
GLA_KEY_DIM = 64
GLA_VALUE_DIM = 128
GLA_KEY_WIDTH = GLA_HEADS * GLA_KEY_DIM
GLA_WIDTH = GLA_HEADS * GLA_VALUE_DIM
GLA_GATE_RANK = 16
GLA_GATE_TEMP = 16.0
N_EXPERTS = 256
N_GROUPS = 8
GROUP_SIZE = N_EXPERTS // N_GROUPS
TOPK_GROUPS = 4
TOP_K = 8
EXPERT_DIM = 256
SHARED_DIM = 256
ROUTED_SCALE = 2.5
DEPTH = 1
DEEPNORM_ALPHA = (2.0 * DEPTH) ** 0.25
IN_SPLITS = (FOX_WIDTH, FOX_WIDTH, FOX_WIDTH, FOX_HEADS, GLA_KEY_WIDTH, GLA_KEY_WIDTH, GLA_WIDTH,
             GLA_GATE_RANK, GLA_WIDTH)

LANES = 128
VMEM_LIMIT = 48 * 1024 * 1024
BIG_VMEM_LIMIT = 56 * 1024 * 1024

TM_PROJ = 512
TQ_FOX = 512
TK_FOX = 512
FOX_GROUP = 8
TS_GLA = 512
TM_OPROJ = 512
TM_ROUTE = 512
EXPERT_ROW_BLOCK = 256
EXPERT_UNIT_BLOCKS = 6
TM_FINAL = 512

SC_CORES = 2
SC_SUBCORES = 16
SC_ROW_WINDOW = 128
ROW_PARTS = 2
PART_WORDS = D_MODEL // 2 // ROW_PARTS


def _cparams(*sem, vmem=VMEM_LIMIT):
    return pltpu.CompilerParams(dimension_semantics=sem, vmem_limit_bytes=vmem)


def _layer_norm(x, g, b):
    mu = jnp.mean(x, axis=-1, keepdims=True)
    xc = x - mu
    var = jnp.mean(xc * xc, axis=-1, keepdims=True)
    return xc * lax.rsqrt(var + LN_EPS) * g + b


def _log_sigmoid(z):
    return jnp.minimum(z, 0.0) - jnp.log1p(jnp.exp(-jnp.abs(z)))


def _sigmoid(z):
    return 1.0 / (1.0 + jnp.exp(-z))


def _dot(a, b):
    return jnp.dot(a, b, preferred_element_type=F32)


def _dot_nt(a, b):
    return lax.dot_general(a, b, (((1,), (1,)), ((), ())), preferred_element_type=F32)


def _dot_tn(a, b):
    return lax.dot_general(a, b, (((0,), (0,)), ((), ())), preferred_element_type=F32)


def _pack_bf16_pairs(x):
    n = x.shape[1] // 2
    u = lax.bitcast_convert_type(x.astype(BF16).astype(F32), jnp.uint32)
    return (u[:, :n] >> 16) | u[:, n:]


def _unpack_bf16_pairs(w):
    lo = lax.bitcast_convert_type(w << 16, F32)
    hi = lax.bitcast_convert_type(w & jnp.uint32(0xFFFF0000), F32)
    return lo, hi


def _store_row_parts(x, part_refs):
    packed = _pack_bf16_pairs(x)
    for p, ref in enumerate(part_refs):
        ref[...] = packed[:, p * PART_WORDS:(p + 1) * PART_WORDS]


def _load_row_parts(parts):
    out = []
    for p, w in enumerate(parts):
        lo, hi = _unpack_bf16_pairs(w)
        out.append((p * PART_WORDS, lo))
        out.append((D_MODEL // 2 + p * PART_WORDS, hi))
    return sorted(out, key=lambda t: t[0])


def _proj_kernel(x_ref, g_ref, b_ref, wm_ref, ws_ref,
                 h_ref, fq_ref, fk_ref, fv_ref, gq_ref, gk_ref, gv_ref, gr_ref, sm_ref):
    h = _layer_norm(x_ref[...], g_ref[...], b_ref[...])
    h_ref[...] = h
    hb = h.astype(BF16)
    off = 0
    for ref in (fq_ref, fk_ref, fv_ref, gq_ref, gk_ref, gv_ref, gr_ref):
        n = ref.shape[1]
        ref[...] = _dot(hb, wm_ref[:, off:off + n]).astype(ref.dtype)
        off += n
    sm_ref[...] = _dot(hb, ws_ref[...])


def _proj(x2, g, b, w_main, w_small):
    T, D = x2.shape
    tm = TM_PROJ
    widths = (FOX_WIDTH, FOX_WIDTH, FOX_WIDTH, GLA_KEY_WIDTH, GLA_KEY_WIDTH, GLA_WIDTH, GLA_WIDTH)
    row = lambda n: pl.BlockSpec((tm, n), lambda i: (i, 0))
    full = lambda a: pl.BlockSpec(a.shape, lambda i: (0,) * a.ndim)
    out_shape = [jax.ShapeDtypeStruct((T, D), F32)]
    out_shape += [jax.ShapeDtypeStruct((T, n), BF16) for n in widths]
    out_shape += [jax.ShapeDtypeStruct((T, LANES), F32)]
    out_specs = [row(D)] + [row(n) for n in widths] + [row(LANES)]
    return pl.pallas_call(
        _proj_kernel, grid=(T // tm,),
        in_specs=[row(D), full(g), full(b), full(w_main), full(w_small)],
        out_specs=out_specs, out_shape=out_shape,
        compiler_params=_cparams("parallel"), name="proj")(x2, g, b, w_main, w_small)


def _fgate_kernel(sm_ref, bf_ref, c_ref):
    S = sm_ref.shape[0]
    lf = _log_sigmoid(sm_ref[...] + bf_ref[...])
    rows = lax.broadcasted_iota(jnp.int32, lf.shape, 0)
    s = 1
    while s < S:
        lf = lf + jnp.where(rows >= s, pltpu.roll(lf, s, axis=0), 0.0)
        s *= 2
    c_ref[...] = lf * LOG2E


def _fgate(small, bf_pad, B, S):
    T = small.shape[0]
    return pl.pallas_call(
        _fgate_kernel, grid=(B,),
        in_specs=[pl.BlockSpec((S, LANES), lambda b: (b, 0)),
                  pl.BlockSpec((1, LANES), lambda b: (0, 0))],
        out_specs=pl.BlockSpec((S, LANES), lambda b: (b, 0)),
        out_shape=jax.ShapeDtypeStruct((T, LANES), F32),
        compiler_params=_cparams("parallel"), name="fgate")(small, bf_pad)


def _split3(x):
    hi = x.astype(BF16).astype(F32)
    r = x - hi
    mid = r.astype(BF16).astype(F32)
    return hi, mid, (r - mid).astype(BF16).astype(F32)


def _fox_operand(pair_ref_tile, odd, c_col, is_key):
    dh = FOX_HEAD_DIM
    x = pair_ref_tile.astype(F32)
    if odd:
        x = pltpu.roll(x, dh, axis=1)
    lane = lax.broadcasted_iota(jnp.int32, x.shape, 1)
    t0, t1, t2 = _split3(-c_col if is_key else c_col)
    first = dh + 3 if is_key else dh
    bias = jnp.where(lane == first, t0, jnp.where(lane == first + 1, t1, jnp.where(lane == first + 2, t2, 0.0)))
    ones_lo = dh if is_key else dh + 3
    bias = jnp.where((lane >= ones_lo) & (lane < ones_lo + 3), 1.0, bias)
    return jnp.where(lane < dh, x, bias).astype(BF16)


def _fox_kernel(q_ref, k_ref, v_ref, cq_ref, ck_ref, o_ref, ka_ref, vt_ref, *, tq, tk):
    i = pl.program_id(1)
    S = k_ref.shape[0]
    dh = FOX_HEAD_DIM

    @pl.when(i == 0)
    def _():
        def fill(r, carry):
            r0 = pl.multiple_of(r * tk, tk)
            for h in range(FOX_HEADS):
                pair = slice(LANES * (h // 2), LANES * (h // 2 + 1))
                ka_ref[pl.ds(r0, tk), LANES * h:LANES * (h + 1)] = _fox_operand(
                    k_ref[pl.ds(r0, tk), pair], h % 2, ck_ref[pl.ds(r0, tk), h:h + 1], True)
            vt_ref[r] = v_ref[pl.ds(r0, tk), :].astype(F32).T.astype(BF16)
            return carry
        lax.fori_loop(0, S // tk, fill, 0)

    key = lax.broadcasted_iota(jnp.int32, (tk, tq), 0)
    qry = lax.broadcasted_iota(jnp.int32, (tk, tq), 1)
    n_diag = tq // tk
    n_full = i * n_diag

    for p in range(FOX_HEADS // FOX_GROUP):
        group = slice(dh * FOX_GROUP * p, dh * FOX_GROUP * (p + 1))
        heads = tuple(range(FOX_GROUP * p, FOX_GROUP * (p + 1)))
        qa = [_fox_operand(q_ref[:, LANES * (h // 2):LANES * (h // 2 + 1)], h % 2, cq_ref[:, h:h + 1], False)
              for h in heads]

        def logits(j):
            r0 = pl.multiple_of(j * tk, tk)
            return [_dot_nt(ka_ref[pl.ds(r0, tk), LANES * h:LANES * (h + 1)], qa[n]) for n, h in enumerate(heads)]

        def step(j, carry, diag):
            ss = logits(j)
            if diag is not None:
                ss = [jnp.where(qry >= key + diag * tk, s, -jnp.inf) for s in ss]
            ms = [jnp.maximum(c[0], jnp.max(s, axis=0, keepdims=True)) for c, s in zip(carry, ss)]
            prs = [jnp.exp2(s - m) for s, m in zip(ss, ms)]
            alphas = [jnp.exp2(c[0] - m) for c, m in zip(carry, ms)]
            ls = [a * c[1] + jnp.sum(pr, axis=0, keepdims=True) for a, c, pr in zip(alphas, carry, prs)]
            pvs = [_dot(vt_ref[j, dh * h:dh * (h + 1), :], pr.astype(BF16)) for h, pr in zip(heads, prs)]
            accs = [a * c[2] + pv for a, c, pv in zip(alphas, carry, pvs)]
            return tuple(zip(ms, ls, accs))

        init = tuple((jnp.full((1, tq), -jnp.inf, F32), jnp.zeros((1, tq), F32), jnp.zeros((dh, tq), F32))
                     for _ in heads)
        carry = lax.fori_loop(0, n_full, functools.partial(step, diag=None), init)
        for d in range(n_diag):
            carry = step(n_full + d, carry, d)
        o_ref[:, group] = jnp.concatenate([(acc / l).T for _, l, acc in carry], axis=1).astype(o_ref.dtype)


def _fox(fq, fk, fv, c, B, S):
    T = fq.shape[0]
    tq, tk = TQ_FOX, TK_FOX
    nq = S // tq
    once = pl.Buffered(1)
    return pl.pallas_call(
        functools.partial(_fox_kernel, tq=tq, tk=tk), grid=(B, nq),
        in_specs=[pl.BlockSpec((tq, FOX_WIDTH), lambda b, i: (b * nq + i, 0)),
                  pl.BlockSpec((S, FOX_WIDTH), lambda b, i: (b, 0), pipeline_mode=once),
                  pl.BlockSpec((S, FOX_WIDTH), lambda b, i: (b, 0), pipeline_mode=once),
                  pl.BlockSpec((tq, LANES), lambda b, i: (b * nq + i, 0)),
                  pl.BlockSpec((S, LANES), lambda b, i: (b, 0), pipeline_mode=once)],
        out_specs=pl.BlockSpec((tq, FOX_WIDTH), lambda b, i: (b * nq + i, 0)),
        out_shape=jax.ShapeDtypeStruct((T, FOX_WIDTH), BF16),
        scratch_shapes=[pltpu.VMEM((S, FOX_HEADS * LANES), BF16), pltpu.VMEM((S // tk, FOX_WIDTH, tk), BF16)],
        compiler_params=_cparams("parallel", "arbitrary", vmem=BIG_VMEM_LIMIT), name="fox")(fq, fk, fv, c, c)


def _gla_kernel(q_ref, k_ref, v_ref, r_ref, sm_ref, wg_ref, bg_ref, gn_ref, tri_ref, y_ref,
                st_ref, b_ref, o_ref, *, ts):
    dk, dv, C = GLA_KEY_DIM, GLA_VALUE_DIM, CHUNK

    @pl.when(pl.program_id(1) == 0)
    def _():
        st_ref[...] = jnp.zeros_like(st_ref)

    z = _dot(sm_ref[...].astype(BF16), wg_ref[...]) + bg_ref[...]
    la = _log_sigmoid(z) * (1.0 / GLA_GATE_TEMP)
    hi = la.astype(BF16)
    r1 = la - hi.astype(F32)
    mid = r1.astype(BF16)
    lo = (r1 - mid.astype(F32)).astype(BF16)
    tri = tri_ref[...]
    b_ref[...] = _dot(tri, hi) + _dot(tri, mid) + _dot(tri, lo)

    row = lax.broadcasted_iota(jnp.int32, (C, C), 0)
    col = lax.broadcasted_iota(jnp.int32, (C, C), 1)
    trow = lax.broadcasted_iota(jnp.int32, (C, LANES), 0)
    lane = lax.broadcasted_iota(jnp.int32, (C, LANES), 1)
    head_lanes = [lane < dk, lane >= dk]

    def level_reference(b, m):
        if m >= 4:
            n = 2 * m
            return jnp.concatenate(
                [jnp.broadcast_to(b[s + m - 1:s + m, :], (n, LANES)) for s in range(0, C, n)], axis=0)
        up1 = pltpu.roll(b, 1, axis=0)
        if m == 1:
            return jnp.where(trow % 2 == 0, b, up1)
        up2 = pltpu.roll(b, 2, axis=0)
        down1 = pltpu.roll(b, C - 1, axis=0)
        tm = trow % 4
        return jnp.where(tm == 0, down1, jnp.where(tm == 1, b, jnp.where(tm == 2, up1, up2)))

    def chunk(c, carry):
        r0 = pl.multiple_of(c * C, C)
        for p in range(GLA_HEADS // 2):
            kl = slice(p * LANES, (p + 1) * LANES)
            q = q_ref[pl.ds(r0, C), kl].astype(F32)
            k = k_ref[pl.ds(r0, C), kl].astype(F32)
            b = b_ref[pl.ds(r0, C), kl]
            st = st_ref[p]
            stb = st.astype(BF16)
            b_last = b[C - 1:C, :]
            qb, kb = q.astype(BF16), k.astype(BF16)
            q_in = (q * jnp.exp(b)).astype(BF16)
            zero = jnp.zeros_like(qb)

            scores = [jnp.where(row == col, _dot_nt(jnp.where(hl, qb, zero), kb), 0.0) for hl in head_lanes]
            m = C // 2
            while m >= 1:
                ref = level_reference(b, m)
                upper = (trow % (2 * m)) >= m
                qt = jnp.where(upper, q * jnp.exp(jnp.minimum(b - ref, 0.0)), 0.0).astype(BF16)
                kt = jnp.where(upper, 0.0, k * jnp.exp(jnp.minimum(ref - b, 0.0))).astype(BF16)
                same_block = (row // (2 * m)) == (col // (2 * m))
                scores = [sc + jnp.where(same_block, _dot_nt(jnp.where(hl, qt, zero), kt), 0.0)
                          for sc, hl in zip(scores, head_lanes)]
                m //= 2

            kh = (k * jnp.exp(b_last - b)).astype(BF16)
            upd = []
            for n, hl in enumerate(head_lanes):
                h = 2 * p + n
                vl = slice(h * dv, (h + 1) * dv)
                v = v_ref[pl.ds(r0, C), vl]
                o_ref[pl.ds(r0, C), vl] = (_dot_nt(jnp.where(hl, q_in, zero), stb)
                                           + _dot(scores[n].astype(BF16), v))
                upd.append(_dot_tn(v, kh))
            st_ref[p] = st * jnp.exp(b_last) + jnp.where(lax.broadcasted_iota(jnp.int32, st.shape, 1) < dk,
                                                         upd[0], upd[1])
        return carry

    lax.fori_loop(0, ts // C, chunk, 0)

    for h in range(GLA_HEADS):
        vl = slice(h * dv, (h + 1) * dv)
        o = o_ref[:, vl]
        ms = jnp.mean(o * o, axis=-1, keepdims=True)
        g = r_ref[:, vl].astype(F32)
        y = o * lax.rsqrt(ms + LN_EPS) * gn_ref[...] * (g * _sigmoid(g))
        y_ref[:, vl] = y.astype(y_ref.dtype)


def _gla(gq, gk, gv, gr, small, wg_pad, bg, gnorm, tri, B, S):
    T = gq.shape[0]
    ts = TS_GLA
    ns = S // ts
    row = lambda n: pl.BlockSpec((ts, n), lambda b, i: (b * ns + i, 0))
    full = lambda a: pl.BlockSpec(a.shape, lambda b, i: (0,) * a.ndim)
    return pl.pallas_call(
        functools.partial(_gla_kernel, ts=ts), grid=(B, ns),
        in_specs=[row(GLA_KEY_WIDTH), row(GLA_KEY_WIDTH), row(GLA_WIDTH), row(GLA_WIDTH), row(LANES),
                  full(wg_pad), full(bg), full(gnorm), full(tri)],
        out_specs=row(GLA_WIDTH),
        out_shape=jax.ShapeDtypeStruct((T, GLA_WIDTH), BF16),
        scratch_shapes=[pltpu.VMEM((GLA_HEADS // 2, GLA_VALUE_DIM, LANES), F32),
                        pltpu.VMEM((ts, GLA_KEY_WIDTH), F32),
                        pltpu.VMEM((ts, GLA_WIDTH), F32)],
        compiler_params=_cparams("parallel", "arbitrary"), name="gla")(
            gq, gk, gv, gr, small, wg_pad, bg, gnorm, tri)


def _oproj_kernel(yf_ref, yg_ref, h_ref, wo_ref, g_ref, b_ref, wr_ref, h1_ref, st_ref, *part_refs):
    mix = _dot(yf_ref[...], wo_ref[0:FOX_WIDTH, :]) + _dot(yg_ref[...], wo_ref[FOX_WIDTH:, :])
    h1 = _layer_norm(DEEPNORM_ALPHA * h_ref[...] + mix, g_ref[...], b_ref[...])
    h1_ref[...] = h1
    _store_row_parts(h1, part_refs)
    st_ref[...] = _sigmoid(_dot_nt(wr_ref[...], h1.astype(BF16)))


def _oproj(y_fox, y_gla, h, w_out, g, b, w_router_t):
    T, D = h.shape
    tm = TM_OPROJ
    row = lambda n: pl.BlockSpec((tm, n), lambda i: (i, 0))
    full = lambda a: pl.BlockSpec(a.shape, lambda i: (0,) * a.ndim)
    return pl.pallas_call(
        _oproj_kernel, grid=(T // tm,),
        in_specs=[row(FOX_WIDTH), row(GLA_WIDTH), row(D), full(w_out), full(g), full(b), full(w_router_t)],
        out_specs=[row(D), pl.BlockSpec((N_EXPERTS, tm), lambda i: (0, i))] + [row(PART_WORDS)] * ROW_PARTS,
        out_shape=[jax.ShapeDtypeStruct((T, D), F32), jax.ShapeDtypeStruct((N_EXPERTS, T), F32)]
        + [jax.ShapeDtypeStruct((T, PART_WORDS), jnp.uint32)] * ROW_PARTS,
        compiler_params=_cparams("parallel"), name="oproj")(y_fox, y_gla, h, w_out, g, b, w_router_t)


def _route_kernel(s_ref, bias_ref, upper_ref, ones_ref, idx_ref, w_ref, rank_ref, cnt_ref, carry_ref):
    E, tm = s_ref.shape

    @pl.when(pl.program_id(0) == 0)
    def _():
        carry_ref[...] = jnp.zeros_like(carry_ref)

    s = s_ref[...]
    biased = s + bias_ref[...]
    neg = -jnp.inf
    erow = lax.broadcasted_iota(jnp.int32, (E, tm), 0).astype(F32)
    grow = lax.broadcasted_iota(jnp.int32, (GROUP_SIZE, tm), 0).astype(F32)

    gs = []
    for g in range(N_GROUPS):
        blk = biased[g * GROUP_SIZE:(g + 1) * GROUP_SIZE, :]
        m1 = jnp.max(blk, axis=0, keepdims=True)
        i1 = jnp.min(jnp.where(blk == m1, grow, float(GROUP_SIZE)), axis=0, keepdims=True)
        m2 = jnp.max(jnp.where(grow == i1, neg, blk), axis=0, keepdims=True)
        gs.append(m1 + m2)
    keep = []
    for g in range(N_GROUPS):
        beaten = jnp.zeros((1, tm), F32)
        for o in range(N_GROUPS):
            if o == g:
                continue
            wins = (gs[o] > gs[g]) | ((gs[o] == gs[g]) & (o < g))
            beaten = beaten + jnp.where(wins, 1.0, 0.0)
        keep.append(jnp.broadcast_to(beaten < float(TOPK_GROUPS), (GROUP_SIZE, tm)))
    cur = jnp.where(jnp.concatenate(keep, axis=0), biased, neg)

    ids, ws, hots = [], [], []
    chosen = jnp.zeros((E, tm), F32)
    for _ in range(TOP_K):
        m = jnp.max(cur, axis=0, keepdims=True)
        ik = jnp.min(jnp.where(cur == m, erow, float(E)), axis=0, keepdims=True)
        hot = erow == ik
        ws.append(jnp.sum(jnp.where(hot, s, 0.0), axis=0, keepdims=True))
        cur = jnp.where(hot, neg, cur)
        chosen = jnp.where(hot, 1.0, chosen)
        ids.append(ik)
        hots.append(hot)
    wsum = ws[0]
    for w in ws[1:]:
        wsum = wsum + w

    chosen_b = chosen.astype(BF16)
    before = carry_ref[...] + _dot(chosen_b, upper_ref[...])
    carry_ref[...] = carry_ref[...] + _dot(chosen_b, ones_ref[...])
    for k in range(TOP_K):
        idx_ref[k:k + 1, :] = ids[k].astype(jnp.int32)
        w_ref[k:k + 1, :] = ws[k] / wsum * ROUTED_SCALE
        rank_ref[k:k + 1, :] = jnp.sum(jnp.where(hots[k], before, 0.0), axis=0, keepdims=True).astype(jnp.int32)
    cnt_ref[...] = carry_ref[:, 0:LANES]


def _route(scores_t, bias_col, upper, ones):
    E, T = scores_t.shape
    tm = TM_ROUTE
    full = lambda a: pl.BlockSpec(a.shape, lambda i: (0,) * a.ndim)
    kt = pl.BlockSpec((TOP_K, tm), lambda i: (0, i))
    return pl.pallas_call(
        _route_kernel, grid=(T // tm,),
        in_specs=[pl.BlockSpec((E, tm), lambda i: (0, i)), full(bias_col), full(upper), full(ones)],
        out_specs=[kt, kt, kt, pl.BlockSpec((E, LANES), lambda i: (0, 0))],
        out_shape=[jax.ShapeDtypeStruct((TOP_K, T), jnp.int32), jax.ShapeDtypeStruct((TOP_K, T), F32),
                   jax.ShapeDtypeStruct((TOP_K, T), jnp.int32), jax.ShapeDtypeStruct((E, LANES), F32)],
        scratch_shapes=[pltpu.VMEM((E, tm), F32)],
        compiler_params=_cparams("arbitrary"), name="route")(scores_t, bias_col, upper, ones)


def _expert_kernel(ufirst_ref, ucount_ref, ustart_ref, usize_ref, total_ref, *refs):
    x_hbm = refs[:ROW_PARTS]
    wg_ref, wu_ref, wd_ref = refs[ROW_PARTS:ROW_PARTS + 3]
    y_hbm = refs[ROW_PARTS + 3:2 * ROW_PARTS + 3]
    xbuf, ybuf, wgb, wub, wdb, in_sem, out_sem = refs[2 * ROW_PARTS + 3:]
    e = pl.program_id(0)
    blk = EXPERT_ROW_BLOCK
    total = total_ref[0]

    def hbm_rows(u, k):
        return pl.ds(pl.multiple_of(ustart_ref[u] * blk, blk), k * blk)

    def in_copy(u, k, p):
        return pltpu.make_async_copy(x_hbm[p].at[hbm_rows(u, k)], xbuf.at[u % 2, p, pl.ds(0, k * blk)],
                                     in_sem.at[u % 2, p])

    def out_copy(u, k, p):
        return pltpu.make_async_copy(ybuf.at[u % 2, p, pl.ds(0, k * blk)], y_hbm[p].at[hbm_rows(u, k)],
                                     out_sem.at[u % 2, p])

    def for_size(u, fn):
        for k in range(1, EXPERT_UNIT_BLOCKS + 1):
            pl.when(usize_ref[u] == k)(functools.partial(fn, k))

    def start_read(u):
        def go(k):
            for p in range(ROW_PARTS):
                in_copy(u, k, p).start()
        for_size(u, go)

    def wait_read(u):
        def go(k):
            for p in range(ROW_PARTS):
                in_copy(u, k, p).wait()
        for_size(u, go)

    def wait_write(u):
        def go(k):
            for p in range(ROW_PARTS):
                out_copy(u, k, p).wait()
        for_size(u, go)

    def compute(u, k):
        slot = u % 2
        rows = pl.ds(0, k * blk)
        gate = up = None
        for c0, xc in _load_row_parts([xbuf[slot, p, rows] for p in range(ROW_PARTS)]):
            xb = xc.astype(BF16)
            wrows = slice(c0, c0 + PART_WORDS)
            gc = _dot(xb, wgb[wrows, :])
            uc = _dot(xb, wub[wrows, :])
            gate = gc if gate is None else gate + gc
            up = uc if up is None else up + uc
        y = _dot((gate * _sigmoid(gate) * up).astype(BF16), wdb[...])
        _store_row_parts(y, [ybuf.at[slot, p, rows] for p in range(ROW_PARTS)])
        for p in range(ROW_PARTS):
            out_copy(u, k, p).start()

    @pl.when((e == 0) & (total > 0))
    def _():
        start_read(0)

    @pl.when(ucount_ref[e] > 0)
    def _():
        wgb[...] = wg_ref[...].astype(BF16)
        wub[...] = wu_ref[...].astype(BF16)
        wdb[...] = wd_ref[...].astype(BF16)

        def unit(c, carry):
            u = ufirst_ref[e] + c
            wait_read(u)
            pl.when(u + 1 < total)(lambda: start_read(u + 1))
            pl.when(u >= 2)(lambda: wait_write(u - 2))
            for_size(u, lambda k: compute(u, k))
            return carry

        lax.fori_loop(0, ucount_ref[e], unit, 0)

    @pl.when(e == pl.num_programs(0) - 1)
    def _():
        for back in (2, 1):
            pl.when(total >= back)(lambda back=back: wait_write(total - back))


def _experts(ufirst, ucount, ustart, usize, total, xs_parts, w_gate, w_up, w_down):
    P = xs_parts[0].shape[0]
    D = D_MODEL
    unit_rows = EXPERT_UNIT_BLOCKS * EXPERT_ROW_BLOCK
    hbm = pl.BlockSpec(memory_space=pl.ANY)
    grid_spec = pltpu.PrefetchScalarGridSpec(
        num_scalar_prefetch=5, grid=(N_EXPERTS,),
        in_specs=[hbm] * ROW_PARTS + [
            pl.BlockSpec((None, D, EXPERT_DIM), lambda e, *_: (e, 0, 0)),
            pl.BlockSpec((None, D, EXPERT_DIM), lambda e, *_: (e, 0, 0)),
            pl.BlockSpec((None, EXPERT_DIM, D), lambda e, *_: (e, 0, 0))],
        out_specs=[hbm] * ROW_PARTS,
        scratch_shapes=[pltpu.VMEM((2, ROW_PARTS, unit_rows, PART_WORDS), jnp.uint32),
                        pltpu.VMEM((2, ROW_PARTS, unit_rows, PART_WORDS), jnp.uint32),
                        pltpu.VMEM((D, EXPERT_DIM), BF16), pltpu.VMEM((D, EXPERT_DIM), BF16),
                        pltpu.VMEM((EXPERT_DIM, D), BF16),
                        pltpu.SemaphoreType.DMA((2, ROW_PARTS)), pltpu.SemaphoreType.DMA((2, ROW_PARTS))])
    return pl.pallas_call(
        _expert_kernel, grid_spec=grid_spec,
        out_shape=[jax.ShapeDtypeStruct((P, PART_WORDS), jnp.uint32)] * ROW_PARTS,
        compiler_params=_cparams("arbitrary", vmem=BIG_VMEM_LIMIT), name="experts")(
            ufirst, ucount, ustart, usize, total, *xs_parts, w_gate, w_up, w_down)


def _pos_kernel(idx_ref, rank_ref, start_ref, pos_ref):
    K, tm = idx_ref.shape
    E = start_ref.shape[0]
    erow = lax.broadcasted_iota(jnp.int32, (E, tm), 0)
    start = start_ref[...]
    for k in range(K):
        first = jnp.sum(jnp.where(erow == idx_ref[k:k + 1, :], start, 0.0), axis=0, keepdims=True)
        pos_ref[k:k + 1, :] = first.astype(jnp.int32) + rank_ref[k:k + 1, :]


def _positions(idx_t, rank_t, start_col):
    K, T = idx_t.shape
    tm = TM_ROUTE
    kt = pl.BlockSpec((K, tm), lambda i: (0, i))
    return pl.pallas_call(
        _pos_kernel, grid=(T // tm,),
        in_specs=[kt, kt, pl.BlockSpec(start_col.shape, lambda i: (0, 0))],
        out_specs=kt, out_shape=jax.ShapeDtypeStruct((K, T), jnp.int32),
        compiler_params=_cparams("parallel"), name="positions")(idx_t, rank_t, start_col)


def _sc_mesh():
    return plsc.VectorSubcoreMesh(core_axis_name="core", subcore_axis_name="subcore",
                                  num_cores=SC_CORES, num_subcores=SC_SUBCORES)


def _sc_scatter_rows(parts, pos, n_out):
    T, W = parts[0].shape
    K = pos.shape[0]
    n = len(parts)
    win = SC_ROW_WINDOW

    @functools.partial(pl.kernel, out_type=[jax.ShapeDtypeStruct((n_out, W), parts[0].dtype)] * n, mesh=_sc_mesh(),
                       name="sc_dispatch")
    def k(*refs):
        x_hbms, p_hbm, o_hbms = refs[:n], refs[n], refs[n + 1:]
        for x_hbm, o_hbm in zip(x_hbms, o_hbms):
            def body(x_vmem, p_vmem, o_hbm=o_hbm):
                for j in range(K):
                    pltpu.sync_copy(x_vmem, o_hbm.at[p_vmem.at[j]])

            pltpu.emit_pipeline(
                body, grid=(T // win,),
                in_specs=[pl.BlockSpec((win, W), lambda i: (i, 0)), pl.BlockSpec((K, win), lambda i: (0, i))],
                out_specs=[], core_axis_name=("core", "subcore"),
                dimension_semantics=(pltpu.PARALLEL,))(x_hbm, p_hbm)

    return k(*parts, pos)


def _sc_gather_rows(tables, idx):
    M = idx.shape[0]
    W = tables[0].shape[1]
    n = len(tables)
    win = SC_ROW_WINDOW

    @functools.partial(pl.kernel, out_type=[jax.ShapeDtypeStruct((M, W), tables[0].dtype)] * n, mesh=_sc_mesh(),
                       name="sc_combine")
    def k(*refs):
        t_hbms, i_hbm, o_hbms = refs[:n], refs[n], refs[n + 1:]
        for t_hbm, o_hbm in zip(t_hbms, o_hbms):
            def body(i_vmem, o_vmem, t_hbm=t_hbm):
                pltpu.sync_copy(t_hbm.at[i_vmem.at[0]], o_vmem)

            pltpu.emit_pipeline(
                body, grid=(M // win,),
                in_specs=[pl.BlockSpec((1, win), lambda i: (0, i))],
                out_specs=[pl.BlockSpec((win, W), lambda i: (i, 0))],
                core_axis_name=("core", "subcore"), dimension_semantics=(pltpu.PARALLEL,))(i_hbm, o_hbm)

    return k(*tables, idx.reshape(1, M))


def _final_kernel(h1_ref, wk_ref, wg_ref, wu_ref, wd_ref, g_ref, b_ref, *refs):
    yk_refs, o_ref = refs[:ROW_PARTS], refs[ROW_PARTS]
    h1 = h1_ref[...]
    hb = h1.astype(BF16)
    g = _dot(hb, wg_ref[...])
    u = _dot(hb, wu_ref[...])
    ffn = _dot((g * _sigmoid(g) * u).astype(BF16), wd_ref[...])
    wk = wk_ref[...]
    sums = None
    for k in range(TOP_K):
        cols = _load_row_parts([r[k] for r in yk_refs])
        terms = [c * wk[:, k:k + 1] for _, c in cols]
        sums = terms if sums is None else [s + t for s, t in zip(sums, terms)]
    ffn = ffn + jnp.concatenate(sums, axis=1)
    o_ref[...] = _layer_norm(DEEPNORM_ALPHA * h1 + ffn, g_ref[...], b_ref[...])


def _final(h1, yk_parts, wk, w_sg, w_su, w_sd, g, b):
    T, D = h1.shape
    tm = TM_FINAL
    row = lambda n: pl.BlockSpec((tm, n), lambda i: (i, 0))
    full = lambda a: pl.BlockSpec(a.shape, lambda i: (0,) * a.ndim)
    return pl.pallas_call(
        _final_kernel, grid=(T // tm,),
        in_specs=[row(D), row(TOP_K), full(w_sg), full(w_su), full(w_sd), full(g), full(b)]
        + [pl.BlockSpec((TOP_K, tm, PART_WORDS), lambda i: (0, i, 0))] * ROW_PARTS,
        out_specs=row(D), out_shape=jax.ShapeDtypeStruct((T, D), F32),
        compiler_params=_cparams("parallel"), name="final")(h1, wk, w_sg, w_su, w_sd, g, b, *yk_parts)


def _block_diag_tri(n, c):
    r = np.arange(n)
    return jnp.asarray(((r[:, None] >= r[None, :]) & (r[:, None] // c == r[None, :] // c)).astype(np.float32), BF16)


def kernel(x, ln_in_g, ln_in_b, w_in, b_fgate, w_gate_up, b_gate, g_gla_norm, w_out, ln1_g, ln1_b, w_router,
           router_bias, w_exp_gate, w_exp_up, w_exp_down, w_sh_gate, w_sh_up, w_sh_down, ln2_g, ln2_b):
    B, S, D = x.shape
    T = B * S
    x2 = x.reshape(T, D)
    l = 0
    row = lambda a: a.reshape(1, -1)

    off = np.cumsum((0,) + IN_SPLITS)
    seg = lambda i: w_in[l][:, off[i]:off[i + 1]]
    w_main = jnp.concatenate([seg(0) * (FOX_HEAD_DIM ** -0.5 * LOG2E), seg(1), seg(2), seg(4) * GLA_KEY_DIM ** -0.5, seg(5),
                              seg(6), seg(8)], axis=1).astype(BF16)
    n_small = FOX_HEADS + GLA_GATE_RANK
    w_small = jnp.concatenate([seg(3), seg(7), jnp.zeros((D, LANES - n_small), F32)], axis=1).astype(BF16)
    bf_pad = jnp.concatenate([b_fgate[l], jnp.zeros((LANES - FOX_HEADS,), F32)]).reshape(1, LANES)
    wg_pad = jnp.zeros((LANES, GLA_KEY_WIDTH), F32).at[FOX_HEADS:n_small].set(w_gate_up[l]).astype(BF16)

    h, fq, fk, fv, gq, gk, gv, gr, small = _proj(x2, row(ln_in_g), row(ln_in_b), w_main, w_small)

    y_fox = _fox(fq, fk, fv, _fgate(small, bf_pad, B, S), B, S)

    y_gla = _gla(gq, gk, gv, gr, small, wg_pad, row(b_gate[l]), row(g_gla_norm[l]),
                 _block_diag_tri(TS_GLA, CHUNK), B, S)

    h1, scores_t, *h1_parts = _oproj(y_fox, y_gla, h, w_out[l].astype(BF16), row(ln1_g[l]), row(ln1_b[l]),
                                     w_router[l].T.astype(BF16))

    tm = TM_ROUTE
    r = np.arange(tm)
    upper = jnp.asarray((r[:, None] < r[None, :]).astype(np.float32), BF16)
    ones = jnp.ones((tm, tm), BF16)
    idx_t, w_t, rank_t, cnt = _route(scores_t, router_bias[l].reshape(N_EXPERTS, 1), upper, ones)

    blk, ub, E = EXPERT_ROW_BLOCK, EXPERT_UNIT_BLOCKS, N_EXPERTS
    A = T * TOP_K
    P = A + E * blk
    counts = cnt[:, 0].astype(jnp.int32)
    nblk = (counts + blk - 1) // blk
    blk_end = jnp.cumsum(nblk)
    blk_start = blk_end - nblk
    pos = _positions(idx_t, rank_t, (blk_start * blk).astype(F32).reshape(E, 1))
    nunit = (nblk + ub - 1) // ub
    unit_end = jnp.cumsum(nunit)
    ufirst = unit_end - nunit
    u = jnp.arange(E + (A // blk + E) // ub + 1, dtype=jnp.int32)
    owner = jnp.minimum(jnp.sum(unit_end[None, :] <= u[:, None], axis=1), E - 1)
    onehot = owner[:, None] == jnp.arange(E)[None, :]
    pick = lambda v: jnp.sum(jnp.where(onehot, v[None, :], 0), axis=1)
    j = u - pick(ufirst)
    n_units = jnp.maximum(pick(nunit), 1)
    base, rem = pick(nblk) // n_units, pick(nblk) % n_units
    live = u < unit_end[-1]
    usize = jnp.where(live, base + (j < rem), 1).astype(jnp.int32)
    ustart = jnp.where(live, pick(blk_start) + j * base + jnp.minimum(j, rem), 0).astype(jnp.int32)

    xs_parts = _sc_scatter_rows(h1_parts, pos, P)
    ys_parts = _experts(ufirst, nunit, ustart, usize, unit_end[-1:], xs_parts,
                        w_exp_gate[l], w_exp_up[l], w_exp_down[l])
    yk_parts = [y.reshape(TOP_K, T, PART_WORDS) for y in _sc_gather_rows(ys_parts, pos.reshape(-1))]

    out = _final(h1, yk_parts, w_t.T, w_sh_gate[l].astype(BF16), w_sh_up[l].astype(BF16),
                 w_sh_down[l].astype(BF16), row(ln2_g[l]), row(ln2_b[l]))
    return out.reshape(B, S, D)
```

```python
import functools

import jax
import jax.numpy as jnp
import numpy as np
from jax import lax
from jax.experimental import pallas as pl
from jax.experimental.pallas import tpu as pltpu
from jax.experimental.pallas import tpu_sc as plsc

F32 = jnp.float32
BF16 = jnp.bfloat16

D_MODEL = 1024
CHUNK = 64
LN_EPS = 1e-5
LOG2E = 1.4426950408889634
FOX_HEADS = 8
FOX_HEAD_DIM = 64
FOX_WIDTH = FOX_HEADS * FOX_HEAD_DIM
GLA_HEADS = 4
GLA_KEY_DIM = 64
GLA_VALUE_DIM = 128
GLA_KEY_WIDTH = GLA_HEADS * GLA_KEY_DIM
GLA_WIDTH = GLA_HEADS * GLA_VALUE_DIM
GLA_GATE_RANK = 16
GLA_GATE_TEMP = 16.0
N_EXPERTS = 256
N_GROUPS = 8
GROUP_SIZE = N_EXPERTS // N_GROUPS
TOPK_GROUPS = 4
TOP_K = 8
EXPERT_DIM = 256
SHARED_DIM = 256
ROUTED_SCALE = 2.5
DEPTH = 1
DEEPNORM_ALPHA = (2.0 * DEPTH) ** 0.25
IN_SPLITS = (FOX_WIDTH, FOX_WIDTH, FOX_WIDTH, FOX_HEADS, GLA_KEY_WIDTH, GLA_KEY_WIDTH, GLA_WIDTH,
             GLA_GATE_RANK, GLA_WIDTH)

LANES = 128
VMEM_LIMIT = 48 * 1024 * 1024
BIG_VMEM_LIMIT = 56 * 1024 * 1024

TM_PROJ = 512
TQ_FOX = 512
TK_FOX = 512
FOX_GROUP = 8
TS_GLA = 512
GLA_CHUNKS_PER_STEP = 4
TM_OPROJ = 512
TM_ROUTE = 512
EXPERT_ROW_BLOCK = 256
EXPERT_UNIT_BLOCKS = 6
TM_FINAL = 512

SC_CORES = 2
SC_SUBCORES = 16
SC_ROW_WINDOW = 128
ROW_PARTS = 2
PART_WORDS = D_MODEL // 2 // ROW_PARTS


def _cparams(*sem, vmem=VMEM_LIMIT):
    return pltpu.CompilerParams(dimension_semantics=sem, vmem_limit_bytes=vmem)


def _layer_norm(x, g, b):
    mu = jnp.mean(x, axis=-1, keepdims=True)
    xc = x - mu
    var = jnp.mean(xc * xc, axis=-1, keepdims=True)
    return xc * lax.rsqrt(var + LN_EPS) * g + b


def _log_sigmoid(z):
    return jnp.minimum(z, 0.0) - jnp.log1p(jnp.exp(-jnp.abs(z)))


def _sigmoid(z):
    return 1.0 / (1.0 + jnp.exp(-z))


def _dot(a, b):
    return jnp.dot(a, b, preferred_element_type=F32)


def _dot_nt(a, b):
    return lax.dot_general(a, b, (((1,), (1,)), ((), ())), preferred_element_type=F32)


def _dot_tn(a, b):
    return lax.dot_general(a, b, (((0,), (0,)), ((), ())), preferred_element_type=F32)


def _pack_bf16_pairs(x):
    n = x.shape[1] // 2
    u = lax.bitcast_convert_type(x.astype(BF16).astype(F32), jnp.uint32)
    return (u[:, :n] >> 16) | u[:, n:]


def _unpack_bf16_pairs(w):
    lo = lax.bitcast_convert_type(w << 16, F32)
    hi = lax.bitcast_convert_type(w & jnp.uint32(0xFFFF0000), F32)
    return lo, hi


def _store_row_parts(x, part_refs):
    packed = _pack_bf16_pairs(x)
    for p, ref in enumerate(part_refs):
        ref[...] = packed[:, p * PART_WORDS:(p + 1) * PART_WORDS]


def _load_row_parts(parts):
    out = []
    for p, w in enumerate(parts):
        lo, hi = _unpack_bf16_pairs(w)
        out.append((p * PART_WORDS, lo))
        out.append((D_MODEL // 2 + p * PART_WORDS, hi))
    return sorted(out, key=lambda t: t[0])


def _proj_kernel(x_ref, g_ref, b_ref, wm_ref, ws_ref,
                 h_ref, fq_ref, fk_ref, fv_ref, gq_ref, gk_ref, gv_ref, gr_ref, sm_ref):
    h = _layer_norm(x_ref[...], g_ref[...], b_ref[...])
    h_ref[...] = h
    hb = h.astype(BF16)
    off = 0
    for ref in (fq_ref, fk_ref, fv_ref, gq_ref, gk_ref, gv_ref, gr_ref):
        n = ref.shape[1]
        ref[...] = _dot(hb, wm_ref[:, off:off + n]).astype(ref.dtype)
        off += n
    sm_ref[...] = _dot(hb, ws_ref[...])


def _proj(x2, g, b, w_main, w_small):
    T, D = x2.shape
    tm = TM_PROJ
    widths = (FOX_WIDTH, FOX_WIDTH, FOX_WIDTH, GLA_KEY_WIDTH, GLA_KEY_WIDTH, GLA_WIDTH, GLA_WIDTH)
    row = lambda n: pl.BlockSpec((tm, n), lambda i: (i, 0))
    full = lambda a: pl.BlockSpec(a.shape, lambda i: (0,) * a.ndim)
    out_shape = [jax.ShapeDtypeStruct((T, D), F32)]
    out_shape += [jax.ShapeDtypeStruct((T, n), BF16) for n in widths]
    out_shape += [jax.ShapeDtypeStruct((T, LANES), F32)]
    out_specs = [row(D)] + [row(n) for n in widths] + [row(LANES)]
    return pl.pallas_call(
        _proj_kernel, grid=(T // tm,),
        in_specs=[row(D), full(g), full(b), full(w_main), full(w_small)],
        out_specs=out_specs, out_shape=out_shape,
        compiler_params=_cparams("parallel"), name="proj")(x2, g, b, w_main, w_small)


def _fgate_kernel(sm_ref, bf_ref, c_ref):
    S = sm_ref.shape[0]
    lf = _log_sigmoid(sm_ref[...] + bf_ref[...])
    rows = lax.broadcasted_iota(jnp.int32, lf.shape, 0)
    s = 1
    while s < S:
        lf = lf + jnp.where(rows >= s, pltpu.roll(lf, s, axis=0), 0.0)
        s *= 2
    c_ref[...] = lf * LOG2E


def _fgate(small, bf_pad, B, S):
    T = small.shape[0]
    return pl.pallas_call(
        _fgate_kernel, grid=(B,),
        in_specs=[pl.BlockSpec((S, LANES), lambda b: (b, 0)),
                  pl.BlockSpec((1, LANES), lambda b: (0, 0))],
        out_specs=pl.BlockSpec((S, LANES), lambda b: (b, 0)),
        out_shape=jax.ShapeDtypeStruct((T, LANES), F32),
        compiler_params=_cparams("parallel"), name="fgate")(small, bf_pad)


def _split3(x):
    hi = x.astype(BF16).astype(F32)
    r = x - hi
    mid = r.astype(BF16).astype(F32)
    return hi, mid, (r - mid).astype(BF16).astype(F32)


def _fox_bias_maps():
    dh = FOX_HEAD_DIM
    maps = np.zeros((2, FOX_HEADS, 3 * LANES, LANES), np.float32)
    for is_key in (0, 1):
        for h in range(FOX_HEADS):
            for piece in range(3):
                maps[is_key, h, piece * LANES + h, dh + 3 * is_key + piece] = 1.0 - 2.0 * is_key
    return jnp.asarray(maps, BF16)


def _gate_pieces(c_tile):
    return jnp.concatenate([t.astype(BF16) for t in _split3(c_tile)], axis=1)


def _fox_operand(pair_ref_tile, odd, pieces, lane_map, is_key):
    dh = FOX_HEAD_DIM
    x = pair_ref_tile.astype(F32)
    if odd:
        x = pltpu.roll(x, dh, axis=1)
    lane = lax.broadcasted_iota(jnp.int32, x.shape, 1)
    ones_lo = dh if is_key else dh + 3
    bias = jnp.where((lane >= ones_lo) & (lane < ones_lo + 3), 1.0, _dot(pieces, lane_map))
    return jnp.where(lane < dh, x, bias).astype(BF16)


def _fox_kernel(q_ref, k_ref, v_ref, cq_ref, ck_ref, map_ref, o_ref, ka_ref, vt_ref, *, tq, tk):
    i = pl.program_id(1)
    S = k_ref.shape[0]
    dh = FOX_HEAD_DIM

    @pl.when(i == 0)
    def _():
        def fill(r, carry):
            r0 = pl.multiple_of(r * tk, tk)
            pieces = _gate_pieces(ck_ref[pl.ds(r0, tk), :])
            for h in range(FOX_HEADS):
                pair = slice(LANES * (h // 2), LANES * (h // 2 + 1))
                ka_ref[pl.ds(r0, tk), LANES * h:LANES * (h + 1)] = _fox_operand(
                    k_ref[pl.ds(r0, tk), pair], h % 2, pieces, map_ref[1, h], True)
            vt_ref[r] = v_ref[pl.ds(r0, tk), :].astype(F32).T.astype(BF16)
            return carry
        lax.fori_loop(0, S // tk, fill, 0)

    q_pieces = _gate_pieces(cq_ref[...])

    key = lax.broadcasted_iota(jnp.int32, (tk, tq), 0)
    qry = lax.broadcasted_iota(jnp.int32, (tk, tq), 1)
    n_diag = tq // tk
    n_full = i * n_diag

    for p in range(FOX_HEADS // FOX_GROUP):
        group = slice(dh * FOX_GROUP * p, dh * FOX_GROUP * (p + 1))
        heads = tuple(range(FOX_GROUP * p, FOX_GROUP * (p + 1)))
        qa = [_fox_operand(q_ref[:, LANES * (h // 2):LANES * (h // 2 + 1)], h % 2, q_pieces, map_ref[0, h], False)
              for h in heads]

        def logits(j):
            r0 = pl.multiple_of(j * tk, tk)
            return [_dot_nt(ka_ref[pl.ds(r0, tk), LANES * h:LANES * (h + 1)], qa[n]) for n, h in enumerate(heads)]

        def step(j, carry, diag):
            ss = logits(j)
            if diag is not None:
                ss = [jnp.where(qry >= key + diag * tk, s, -jnp.inf) for s in ss]
            ms = [jnp.maximum(c[0], jnp.max(s, axis=0, keepdims=True)) for c, s in zip(carry, ss)]
            prs = [jnp.exp2(s - m) for s, m in zip(ss, ms)]
            alphas = [jnp.exp2(c[0] - m) for c, m in zip(carry, ms)]
            ls = [a * c[1] + jnp.sum(pr, axis=0, keepdims=True) for a, c, pr in zip(alphas, carry, prs)]
            pvs = [_dot(vt_ref[j, dh * h:dh * (h + 1), :], pr.astype(BF16)) for h, pr in zip(heads, prs)]
            accs = [a * c[2] + pv for a, c, pv in zip(alphas, carry, pvs)]
            return tuple(zip(ms, ls, accs))

        init = tuple((jnp.full((1, tq), -jnp.inf, F32), jnp.zeros((1, tq), F32), jnp.zeros((dh, tq), F32))
                     for _ in heads)
        carry = lax.fori_loop(0, n_full, functools.partial(step, diag=None), init)
        for d in range(n_diag):
            carry = step(n_full + d, carry, d)
        o_ref[:, group] = jnp.concatenate([(acc / l).T for _, l, acc in carry], axis=1).astype(o_ref.dtype)


def _fox(fq, fk, fv, c, B, S):
    T = fq.shape[0]
    tq, tk = TQ_FOX, TK_FOX
    nq = S // tq
    once = pl.Buffered(1)
    maps = _fox_bias_maps()
    return pl.pallas_call(
        functools.partial(_fox_kernel, tq=tq, tk=tk), grid=(B, nq),
        in_specs=[pl.BlockSpec((tq, FOX_WIDTH), lambda b, i: (b * nq + i, 0)),
                  pl.BlockSpec((S, FOX_WIDTH), lambda b, i: (b, 0), pipeline_mode=once),
                  pl.BlockSpec((S, FOX_WIDTH), lambda b, i: (b, 0), pipeline_mode=once),
                  pl.BlockSpec((tq, LANES), lambda b, i: (b * nq + i, 0)),
                  pl.BlockSpec((S, LANES), lambda b, i: (b, 0), pipeline_mode=once),
                  pl.BlockSpec(maps.shape, lambda b, i: (0, 0, 0, 0), pipeline_mode=once)],
        out_specs=pl.BlockSpec((tq, FOX_WIDTH), lambda b, i: (b * nq + i, 0)),
        out_shape=jax.ShapeDtypeStruct((T, FOX_WIDTH), BF16),
        scratch_shapes=[pltpu.VMEM((S, FOX_HEADS * LANES), BF16), pltpu.VMEM((S // tk, FOX_WIDTH, tk), BF16)],
        compiler_params=_cparams("parallel", "arbitrary", vmem=BIG_VMEM_LIMIT), name="fox")(
            fq, fk, fv, c, c, maps)


def _gla_kernel(q_ref, k_ref, v_ref, r_ref, sm_ref, wg_ref, bg_ref, gn_ref, tri_ref, y_ref,
                st_ref, b_ref, o_ref, *, ts):
    dk, dv, C = GLA_KEY_DIM, GLA_VALUE_DIM, CHUNK

    @pl.when(pl.program_id(1) == 0)
    def _():
        st_ref[...] = jnp.zeros_like(st_ref)

    z = _dot(sm_ref[...].astype(BF16), wg_ref[...]) + bg_ref[...]
    la = _log_sigmoid(z) * (1.0 / GLA_GATE_TEMP)
    hi = la.astype(BF16)
    r1 = la - hi.astype(F32)
    mid = r1.astype(BF16)
    lo = (r1 - mid.astype(F32)).astype(BF16)
    tri = tri_ref[...]
    b_ref[...] = _dot(tri, hi) + _dot(tri, mid) + _dot(tri, lo)

    row = lax.broadcasted_iota(jnp.int32, (C, C), 0)
    col = lax.broadcasted_iota(jnp.int32, (C, C), 1)
    trow = lax.broadcasted_iota(jnp.int32, (C, LANES), 0)
    lane = lax.broadcasted_iota(jnp.int32, (C, LANES), 1)
    head_lanes = [lane < dk, lane >= dk]

    def level_reference(b, m):
        if m >= 4:
            n = 2 * m
            return jnp.concatenate(
                [jnp.broadcast_to(b[s + m - 1:s + m, :], (n, LANES)) for s in range(0, C, n)], axis=0)
        up1 = pltpu.roll(b, 1, axis=0)
        if m == 1:
            return jnp.where(trow % 2 == 0, b, up1)
        up2 = pltpu.roll(b, 2, axis=0)
        down1 = pltpu.roll(b, C - 1, axis=0)
        tm = trow % 4
        return jnp.where(tm == 0, down1, jnp.where(tm == 1, b, jnp.where(tm == 2, up1, up2)))

    def chunk(r0):
        for p in range(GLA_HEADS // 2):
            kl = slice(p * LANES, (p + 1) * LANES)
            q = q_ref[pl.ds(r0, C), kl].astype(F32)
            k = k_ref[pl.ds(r0, C), kl].astype(F32)
            b = b_ref[pl.ds(r0, C), kl]
            st = st_ref[p]
            stb = st.astype(BF16)
            b_last = b[C - 1:C, :]
            qb, kb = q.astype(BF16), k.astype(BF16)
            q_in = (q * jnp.exp(b)).astype(BF16)
            zero = jnp.zeros_like(qb)

            scores = [jnp.where(row == col, _dot_nt(jnp.where(hl, qb, zero), kb), 0.0) for hl in head_lanes]
            m = C // 2
            while m >= 1:
                ref = level_reference(b, m)
                upper = (trow % (2 * m)) >= m
                qt = jnp.where(upper, q * jnp.exp(jnp.minimum(b - ref, 0.0)), 0.0).astype(BF16)
                kt = jnp.where(upper, 0.0, k * jnp.exp(jnp.minimum(ref - b, 0.0))).astype(BF16)
                same_block = (row // (2 * m)) == (col // (2 * m))
                scores = [sc + jnp.where(same_block, _dot_nt(jnp.where(hl, qt, zero), kt), 0.0)
                          for sc, hl in zip(scores, head_lanes)]
                m //= 2

            kh = (k * jnp.exp(b_last - b)).astype(BF16)
            upd = []
            for n, hl in enumerate(head_lanes):
                h = 2 * p + n
                vl = slice(h * dv, (h + 1) * dv)
                v = v_ref[pl.ds(r0, C), vl]
                o_ref[pl.ds(r0, C), vl] = (_dot_nt(jnp.where(hl, q_in, zero), stb)
                                           + _dot(scores[n].astype(BF16), v))
                upd.append(_dot_tn(v, kh))
            st_ref[p] = st * jnp.exp(b_last) + jnp.where(lax.broadcasted_iota(jnp.int32, st.shape, 1) < dk,
                                                         upd[0], upd[1])

    def chunk_group(g, carry):
        for sub in range(GLA_CHUNKS_PER_STEP):
            chunk(pl.multiple_of((g * GLA_CHUNKS_PER_STEP + sub) * C, C))
        return carry

    lax.fori_loop(0, ts // (C * GLA_CHUNKS_PER_STEP), chunk_group, 0)

    for h in range(GLA_HEADS):
        vl = slice(h * dv, (h + 1) * dv)
        o = o_ref[:, vl]
        ms = jnp.mean(o * o, axis=-1, keepdims=True)
        g = r_ref[:, vl].astype(F32)
        y = o * lax.rsqrt(ms + LN_EPS) * gn_ref[...] * (g * _sigmoid(g))
        y_ref[:, vl] = y.astype(y_ref.dtype)


def _gla(gq, gk, gv, gr, small, wg_pad, bg, gnorm, tri, B, S):
    T = gq.shape[0]
    ts = TS_GLA
    ns = S // ts
    row = lambda n: pl.BlockSpec((ts, n), lambda b, i: (b * ns + i, 0))
    full = lambda a: pl.BlockSpec(a.shape, lambda b, i: (0,) * a.ndim)
    return pl.pallas_call(
        functools.partial(_gla_kernel, ts=ts), grid=(B, ns),
        in_specs=[row(GLA_KEY_WIDTH), row(GLA_KEY_WIDTH), row(GLA_WIDTH), row(GLA_WIDTH), row(LANES),
                  full(wg_pad), full(bg), full(gnorm), full(tri)],
        out_specs=row(GLA_WIDTH),
        out_shape=jax.ShapeDtypeStruct((T, GLA_WIDTH), BF16),
        scratch_shapes=[pltpu.VMEM((GLA_HEADS // 2, GLA_VALUE_DIM, LANES), F32),
                        pltpu.VMEM((ts, GLA_KEY_WIDTH), F32),
                        pltpu.VMEM((ts, GLA_WIDTH), F32)],
        compiler_params=_cparams("parallel", "arbitrary"), name="gla")(
            gq, gk, gv, gr, small, wg_pad, bg, gnorm, tri)


def _oproj_kernel(yf_ref, yg_ref, h_ref, wo_ref, g_ref, b_ref, wr_ref, h1_ref, st_ref, *part_refs):
    mix = _dot(yf_ref[...], wo_ref[0:FOX_WIDTH, :]) + _dot(yg_ref[...], wo_ref[FOX_WIDTH:, :])
    h1 = _layer_norm(DEEPNORM_ALPHA * h_ref[...] + mix, g_ref[...], b_ref[...])
    h1_ref[...] = h1
    _store_row_parts(h1, part_refs)
    st_ref[...] = _sigmoid(_dot_nt(wr_ref[...], h1.astype(BF16)))


def _oproj(y_fox, y_gla, h, w_out, g, b, w_router_t):
    T, D = h.shape
    tm = TM_OPROJ
    row = lambda n: pl.BlockSpec((tm, n), lambda i: (i, 0))
    full = lambda a: pl.BlockSpec(a.shape, lambda i: (0,) * a.ndim)
    return pl.pallas_call(
        _oproj_kernel, grid=(T // tm,),
        in_specs=[row(FOX_WIDTH), row(GLA_WIDTH), row(D), full(w_out), full(g), full(b), full(w_router_t)],
        out_specs=[row(D), pl.BlockSpec((N_EXPERTS, tm), lambda i: (0, i))] + [row(PART_WORDS)] * ROW_PARTS,
        out_shape=[jax.ShapeDtypeStruct((T, D), F32), jax.ShapeDtypeStruct((N_EXPERTS, T), F32)]
        + [jax.ShapeDtypeStruct((T, PART_WORDS), jnp.uint32)] * ROW_PARTS,
        compiler_params=_cparams("parallel"), name="oproj")(y_fox, y_gla, h, w_out, g, b, w_router_t)


def _route_kernel(s_ref, bias_ref, upper_ref, ones_ref, idx_ref, w_ref, rank_ref, cnt_ref, carry_ref):
    E, tm = s_ref.shape

    @pl.when(pl.program_id(0) == 0)
    def _():
        carry_ref[...] = jnp.zeros_like(carry_ref)

    s = s_ref[...]
    biased = s + bias_ref[...]
    neg = -jnp.inf
    erow = lax.broadcasted_iota(jnp.int32, (E, tm), 0).astype(F32)
    grow = lax.broadcasted_iota(jnp.int32, (GROUP_SIZE, tm), 0).astype(F32)

    gs = []
    for g in range(N_GROUPS):
        blk = biased[g * GROUP_SIZE:(g + 1) * GROUP_SIZE, :]
        m1 = jnp.max(blk, axis=0, keepdims=True)
        i1 = jnp.min(jnp.where(blk == m1, grow, float(GROUP_SIZE)), axis=0, keepdims=True)
        m2 = jnp.max(jnp.where(grow == i1, neg, blk), axis=0, keepdims=True)
        gs.append(m1 + m2)
    keep = []
    for g in range(N_GROUPS):
        beaten = jnp.zeros((1, tm), F32)
        for o in range(N_GROUPS):
            if o == g:
                continue
            wins = (gs[o] > gs[g]) | ((gs[o] == gs[g]) & (o < g))
            beaten = beaten + jnp.where(wins, 1.0, 0.0)
        keep.append(jnp.broadcast_to(beaten < float(TOPK_GROUPS), (GROUP_SIZE, tm)))
    cur = jnp.where(jnp.concatenate(keep, axis=0), biased, neg)

    ids, ws, hots = [], [], []
    chosen = jnp.zeros((E, tm), F32)
    for _ in range(TOP_K):
        m = jnp.max(cur, axis=0, keepdims=True)
        ik = jnp.min(jnp.where(cur == m, erow, float(E)), axis=0, keepdims=True)
        hot = erow == ik
        ws.append(jnp.sum(jnp.where(hot, s, 0.0), axis=0, keepdims=True))
        cur = jnp.where(hot, neg, cur)
        chosen = jnp.where(hot, 1.0, chosen)
        ids.append(ik)
        hots.append(hot)
    wsum = ws[0]
    for w in ws[1:]:
        wsum = wsum + w

    chosen_b = chosen.astype(BF16)
    before = carry_ref[...] + _dot(chosen_b, upper_ref[...])
    carry_ref[...] = carry_ref[...] + _dot(chosen_b, ones_ref[...])
    for k in range(TOP_K):
        idx_ref[k:k + 1, :] = ids[k].astype(jnp.int32)
        w_ref[k:k + 1, :] = ws[k] / wsum * ROUTED_SCALE
        rank_ref[k:k + 1, :] = jnp.sum(jnp.where(hots[k], before, 0.0), axis=0, keepdims=True).astype(jnp.int32)
    cnt_ref[...] = carry_ref[:, 0:LANES]


def _route(scores_t, bias_col, upper, ones):
    E, T = scores_t.shape
    tm = TM_ROUTE
    full = lambda a: pl.BlockSpec(a.shape, lambda i: (0,) * a.ndim)
    kt = pl.BlockSpec((TOP_K, tm), lambda i: (0, i))
    return pl.pallas_call(
        _route_kernel, grid=(T // tm,),
        in_specs=[pl.BlockSpec((E, tm), lambda i: (0, i)), full(bias_col), full(upper), full(ones)],
        out_specs=[kt, kt, kt, pl.BlockSpec((E, LANES), lambda i: (0, 0))],
        out_shape=[jax.ShapeDtypeStruct((TOP_K, T), jnp.int32), jax.ShapeDtypeStruct((TOP_K, T), F32),
                   jax.ShapeDtypeStruct((TOP_K, T), jnp.int32), jax.ShapeDtypeStruct((E, LANES), F32)],
        scratch_shapes=[pltpu.VMEM((E, tm), F32)],
        compiler_params=_cparams("arbitrary"), name="route")(scores_t, bias_col, upper, ones)


def _expert_kernel(ufirst_ref, ucount_ref, ustart_ref, usize_ref, total_ref, *refs):
    x_hbm = refs[:ROW_PARTS]
    wg_ref, wu_ref, wd_ref = refs[ROW_PARTS:ROW_PARTS + 3]
    y_hbm = refs[ROW_PARTS + 3:2 * ROW_PARTS + 3]
    xbuf, ybuf, wgb, wub, wdb, in_sem, out_sem = refs[2 * ROW_PARTS + 3:]
    e = pl.program_id(0)
    blk = EXPERT_ROW_BLOCK
    total = total_ref[0]

    def hbm_rows(u, k):
        return pl.ds(pl.multiple_of(ustart_ref[u] * blk, blk), k * blk)

    def in_copy(u, k, p):
        return pltpu.make_async_copy(x_hbm[p].at[hbm_rows(u, k)], xbuf.at[u % 2, p, pl.ds(0, k * blk)],
                                     in_sem.at[u % 2, p])

    def out_copy(u, k, p):
        return pltpu.make_async_copy(ybuf.at[u % 2, p, pl.ds(0, k * blk)], y_hbm[p].at[hbm_rows(u, k)],
                                     out_sem.at[u % 2, p])

    def for_size(u, fn):
        for k in range(1, EXPERT_UNIT_BLOCKS + 1):
            pl.when(usize_ref[u] == k)(functools.partial(fn, k))

    def start_read(u):
        def go(k):
            for p in range(ROW_PARTS):
                in_copy(u, k, p).start()
        for_size(u, go)

    def wait_read(u):
        def go(k):
            for p in range(ROW_PARTS):
                in_copy(u, k, p).wait()
        for_size(u, go)

    def wait_write(u):
        def go(k):
            for p in range(ROW_PARTS):
                out_copy(u, k, p).wait()
        for_size(u, go)

    def compute(u, k):
        slot = u % 2
        rows = pl.ds(0, k * blk)
        gate = up = None
        for c0, xc in _load_row_parts([xbuf[slot, p, rows] for p in range(ROW_PARTS)]):
            xb = xc.astype(BF16)
            wrows = slice(c0, c0 + PART_WORDS)
            gc = _dot(xb, wgb[wrows, :])
            uc = _dot(xb, wub[wrows, :])
            gate = gc if gate is None else gate + gc
            up = uc if up is None else up + uc
        y = _dot((gate * _sigmoid(gate) * up).astype(BF16), wdb[...])
        _store_row_parts(y, [ybuf.at[slot, p, rows] for p in range(ROW_PARTS)])
        for p in range(ROW_PARTS):
            out_copy(u, k, p).start()

    @pl.when((e == 0) & (total > 0))
    def _():
        start_read(0)

    @pl.when(ucount_ref[e] > 0)
    def _():
        wgb[...] = wg_ref[...].astype(BF16)
        wub[...] = wu_ref[...].astype(BF16)
        wdb[...] = wd_ref[...].astype(BF16)

        def unit(c, carry):
            u = ufirst_ref[e] + c
            wait_read(u)
            pl.when(u + 1 < total)(lambda: start_read(u + 1))
            pl.when(u >= 2)(lambda: wait_write(u - 2))
            for_size(u, lambda k: compute(u, k))
            return carry

        lax.fori_loop(0, ucount_ref[e], unit, 0)

    @pl.when(e == pl.num_programs(0) - 1)
    def _():
        for back in (2, 1):
            pl.when(total >= back)(lambda back=back: wait_write(total - back))


def _experts(ufirst, ucount, ustart, usize, total, xs_parts, w_gate, w_up, w_down):
    P = xs_parts[0].shape[0]
    D = D_MODEL
    unit_rows = EXPERT_UNIT_BLOCKS * EXPERT_ROW_BLOCK
    hbm = pl.BlockSpec(memory_space=pl.ANY)
    grid_spec = pltpu.PrefetchScalarGridSpec(
        num_scalar_prefetch=5, grid=(N_EXPERTS,),
        in_specs=[hbm] * ROW_PARTS + [
            pl.BlockSpec((None, D, EXPERT_DIM), lambda e, *_: (e, 0, 0)),
            pl.BlockSpec((None, D, EXPERT_DIM), lambda e, *_: (e, 0, 0)),
            pl.BlockSpec((None, EXPERT_DIM, D), lambda e, *_: (e, 0, 0))],
        out_specs=[hbm] * ROW_PARTS,
        scratch_shapes=[pltpu.VMEM((2, ROW_PARTS, unit_rows, PART_WORDS), jnp.uint32),
                        pltpu.VMEM((2, ROW_PARTS, unit_rows, PART_WORDS), jnp.uint32),
                        pltpu.VMEM((D, EXPERT_DIM), BF16), pltpu.VMEM((D, EXPERT_DIM), BF16),
                        pltpu.VMEM((EXPERT_DIM, D), BF16),
                        pltpu.SemaphoreType.DMA((2, ROW_PARTS)), pltpu.SemaphoreType.DMA((2, ROW_PARTS))])
    return pl.pallas_call(
        _expert_kernel, grid_spec=grid_spec,
        out_shape=[jax.ShapeDtypeStruct((P, PART_WORDS), jnp.uint32)] * ROW_PARTS,
        compiler_params=_cparams("arbitrary", vmem=BIG_VMEM_LIMIT), name="experts")(
            ufirst, ucount, ustart, usize, total, *xs_parts, w_gate, w_up, w_down)


def _pos_kernel(idx_ref, rank_ref, start_ref, pos_ref):
    K, tm = idx_ref.shape
    E = start_ref.shape[0]
    erow = lax.broadcasted_iota(jnp.int32, (E, tm), 0)
    start = start_ref[...]
    for k in range(K):
        first = jnp.sum(jnp.where(erow == idx_ref[k:k + 1, :], start, 0.0), axis=0, keepdims=True)
        pos_ref[k:k + 1, :] = first.astype(jnp.int32) + rank_ref[k:k + 1, :]


def _positions(idx_t, rank_t, start_col):
    K, T = idx_t.shape
    tm = TM_ROUTE
    kt = pl.BlockSpec((K, tm), lambda i: (0, i))
    return pl.pallas_call(
        _pos_kernel, grid=(T // tm,),
        in_specs=[kt, kt, pl.BlockSpec(start_col.shape, lambda i: (0, 0))],
        out_specs=kt, out_shape=jax.ShapeDtypeStruct((K, T), jnp.int32),
        compiler_params=_cparams("parallel"), name="positions")(idx_t, rank_t, start_col)


def _sc_mesh():
    return plsc.VectorSubcoreMesh(core_axis_name="core", subcore_axis_name="subcore",
                                  num_cores=SC_CORES, num_subcores=SC_SUBCORES)


def _sc_scatter_rows(parts, pos, n_out):
    T, W = parts[0].shape
    K = pos.shape[0]
    n = len(parts)
    win = SC_ROW_WINDOW

    @functools.partial(pl.kernel, out_type=[jax.ShapeDtypeStruct((n_out, W), parts[0].dtype)] * n, mesh=_sc_mesh(),
                       name="sc_dispatch")
    def k(*refs):
        x_hbms, p_hbm, o_hbms = refs[:n], refs[n], refs[n + 1:]
        for x_hbm, o_hbm in zip(x_hbms, o_hbms):
            def body(x_vmem, p_vmem, o_hbm=o_hbm):
                for j in range(K):
                    pltpu.sync_copy(x_vmem, o_hbm.at[p_vmem.at[j]])

            pltpu.emit_pipeline(
                body, grid=(T // win,),
                in_specs=[pl.BlockSpec((win, W), lambda i: (i, 0)), pl.BlockSpec((K, win), lambda i: (0, i))],
                out_specs=[], core_axis_name=("core", "subcore"),
                dimension_semantics=(pltpu.PARALLEL,))(x_hbm, p_hbm)

    return k(*parts, pos)


def _sc_gather_rows(tables, idx):
    M = idx.shape[0]
    W = tables[0].shape[1]
    n = len(tables)
    win = SC_ROW_WINDOW

    @functools.partial(pl.kernel, out_type=[jax.ShapeDtypeStruct((M, W), tables[0].dtype)] * n, mesh=_sc_mesh(),
                       name="sc_combine")
    def k(*refs):
        t_hbms, i_hbm, o_hbms = refs[:n], refs[n], refs[n + 1:]
        for t_hbm, o_hbm in zip(t_hbms, o_hbms):
            def body(i_vmem, o_vmem, t_hbm=t_hbm):
                pltpu.sync_copy(t_hbm.at[i_vmem.at[0]], o_vmem)

            pltpu.emit_pipeline(
                body, grid=(M // win,),
                in_specs=[pl.BlockSpec((1, win), lambda i: (0, i))],
                out_specs=[pl.BlockSpec((win, W), lambda i: (i, 0))],
                core_axis_name=("core", "subcore"), dimension_semantics=(pltpu.PARALLEL,))(i_hbm, o_hbm)

    return k(*tables, idx.reshape(1, M))


def _final_kernel(h1_ref, wk_ref, wg_ref, wu_ref, wd_ref, g_ref, b_ref, *refs):
    yk_refs, o_ref = refs[:ROW_PARTS], refs[ROW_PARTS]
    h1 = h1_ref[...]
    hb = h1.astype(BF16)
    g = _dot(hb, wg_ref[...])
    u = _dot(hb, wu_ref[...])
    ffn = _dot((g * _sigmoid(g) * u).astype(BF16), wd_ref[...])
    wk = wk_ref[...]
    sums = None
    for k in range(TOP_K):
        cols = _load_row_parts([r[k] for r in yk_refs])
        terms = [c * wk[:, k:k + 1] for _, c in cols]
        sums = terms if sums is None else [s + t for s, t in zip(sums, terms)]
    ffn = ffn + jnp.concatenate(sums, axis=1)
    o_ref[...] = _layer_norm(DEEPNORM_ALPHA * h1 + ffn, g_ref[...], b_ref[...])


def _final(h1, yk_parts, wk, w_sg, w_su, w_sd, g, b):
    T, D = h1.shape
    tm = TM_FINAL
    row = lambda n: pl.BlockSpec((tm, n), lambda i: (i, 0))
    full = lambda a: pl.BlockSpec(a.shape, lambda i: (0,) * a.ndim)
    return pl.pallas_call(
        _final_kernel, grid=(T // tm,),
        in_specs=[row(D), row(TOP_K), full(w_sg), full(w_su), full(w_sd), full(g), full(b)]
        + [pl.BlockSpec((TOP_K, tm, PART_WORDS), lambda i: (0, i, 0))] * ROW_PARTS,
        out_specs=row(D), out_shape=jax.ShapeDtypeStruct((T, D), F32),
        compiler_params=_cparams("parallel"), name="final")(h1, wk, w_sg, w_su, w_sd, g, b, *yk_parts)


def _block_diag_tri(n, c):
    r = np.arange(n)
    return jnp.asarray(((r[:, None] >= r[None, :]) & (r[:, None] // c == r[None, :] // c)).astype(np.float32), BF16)


def kernel(x, ln_in_g, ln_in_b, w_in, b_fgate, w_gate_up, b_gate, g_gla_norm, w_out, ln1_g, ln1_b, w_router,
           router_bias, w_exp_gate, w_exp_up, w_exp_down, w_sh_gate, w_sh_up, w_sh_down, ln2_g, ln2_b):
    B, S, D = x.shape
    T = B * S
    x2 = x.reshape(T, D)
    l = 0
    row = lambda a: a.reshape(1, -1)

    off = np.cumsum((0,) + IN_SPLITS)
    seg = lambda i: w_in[l][:, off[i]:off[i + 1]]
    w_main = jnp.concatenate([seg(0) * (FOX_HEAD_DIM ** -0.5 * LOG2E), seg(1), seg(2), seg(4) * GLA_KEY_DIM ** -0.5, seg(5),
                              seg(6), seg(8)], axis=1).astype(BF16)
    n_small = FOX_HEADS + GLA_GATE_RANK
    w_small = jnp.concatenate([seg(3), seg(7), jnp.zeros((D, LANES - n_small), F32)], axis=1).astype(BF16)
    bf_pad = jnp.concatenate([b_fgate[l], jnp.zeros((LANES - FOX_HEADS,), F32)]).reshape(1, LANES)
    wg_pad = jnp.zeros((LANES, GLA_KEY_WIDTH), F32).at[FOX_HEADS:n_small].set(w_gate_up[l]).astype(BF16)

    h, fq, fk, fv, gq, gk, gv, gr, small = _proj(x2, row(ln_in_g), row(ln_in_b), w_main, w_small)

    y_fox = _fox(fq, fk, fv, _fgate(small, bf_pad, B, S), B, S)

    y_gla = _gla(gq, gk, gv, gr, small, wg_pad, row(b_gate[l]), row(g_gla_norm[l]),
                 _block_diag_tri(TS_GLA, CHUNK), B, S)

    h1, scores_t, *h1_parts = _oproj(y_fox, y_gla, h, w_out[l].astype(BF16), row(ln1_g[l]), row(ln1_b[l]),
                                     w_router[l].T.astype(BF16))

    tm = TM_ROUTE
    r = np.arange(tm)
    upper = jnp.asarray((r[:, None] < r[None, :]).astype(np.float32), BF16)
    ones = jnp.ones((tm, tm), BF16)
    idx_t, w_t, rank_t, cnt = _route(scores_t, router_bias[l].reshape(N_EXPERTS, 1), upper, ones)

    blk, ub, E = EXPERT_ROW_BLOCK, EXPERT_UNIT_BLOCKS, N_EXPERTS
    A = T * TOP_K
    P = A + E * blk
    counts = cnt[:, 0].astype(jnp.int32)
    nblk = (counts + blk - 1) // blk
    blk_end = jnp.cumsum(nblk)
    blk_start = blk_end - nblk
    pos = _positions(idx_t, rank_t, (blk_start * blk).astype(F32).reshape(E, 1))
    nunit = (nblk + ub - 1) // ub
    unit_end = jnp.cumsum(nunit)
    ufirst = unit_end - nunit
    u = jnp.arange(E + (A // blk + E) // ub + 1, dtype=jnp.int32)
    owner = jnp.minimum(jnp.sum(unit_end[None, :] <= u[:, None], axis=1), E - 1)
    onehot = owner[:, None] == jnp.arange(E)[None, :]
    pick = lambda v: jnp.sum(jnp.where(onehot, v[None, :], 0), axis=1)
    j = u - pick(ufirst)
    n_units = jnp.maximum(pick(nunit), 1)
    base, rem = pick(nblk) // n_units, pick(nblk) % n_units
    live = u < unit_end[-1]
    usize = jnp.where(live, base + (j < rem), 1).astype(jnp.int32)
    ustart = jnp.where(live, pick(blk_start) + j * base + jnp.minimum(j, rem), 0).astype(jnp.int32)

    xs_parts = _sc_scatter_rows(h1_parts, pos, P)
    ys_parts = _experts(ufirst, nunit, ustart, usize, unit_end[-1:], xs_parts,
                        w_exp_gate[l], w_exp_up[l], w_exp_down[l])
    yk_parts = [y.reshape(TOP_K, T, PART_WORDS) for y in _sc_gather_rows(ys_parts, pos.reshape(-1))]

    out = _final(h1, yk_parts, w_t.T, w_sh_gate[l].astype(BF16), w_sh_up[l].astype(BF16),
                 w_sh_down[l].astype(BF16), row(ln2_g[l]), row(ln2_b[l]))
    return out.reshape(B, S, D)
```

```python
import functools

import jax
import jax.numpy as jnp
import numpy as np
from jax import lax
from jax.experimental import pallas as pl
from jax.experimental.pallas import tpu as pltpu
from jax.experimental.pallas import tpu_sc as plsc

F32 = jnp.float32
BF16 = jnp.bfloat16

D_MODEL = 1024
CHUNK = 64
LN_EPS = 1e-5
LOG2E = 1.4426950408889634
FOX_HEADS = 8
FOX_HEAD_DIM = 64
FOX_WIDTH = FOX_HEADS * FOX_HEAD_DIM
GLA_HEADS = 4
GLA_KEY_DIM = 64
GLA_VALUE_DIM = 128
GLA_KEY_WIDTH = GLA_HEADS * GLA_KEY_DIM
GLA_WIDTH = GLA_HEADS * GLA_VALUE_DIM
GLA_GATE_RANK = 16
GLA_GATE_TEMP = 16.0
N_EXPERTS = 256
N_GROUPS = 8
GROUP_SIZE = N_EXPERTS // N_GROUPS
TOPK_GROUPS = 4
TOP_K = 8
EXPERT_DIM = 256
SHARED_DIM = 256
ROUTED_SCALE = 2.5
DEPTH = 1
DEEPNORM_ALPHA = (2.0 * DEPTH) ** 0.25
IN_SPLITS = (FOX_WIDTH, FOX_WIDTH, FOX_WIDTH, FOX_HEADS, GLA_KEY_WIDTH, GLA_KEY_WIDTH, GLA_WIDTH,
             GLA_GATE_RANK, GLA_WIDTH)

LANES = 128
VMEM_LIMIT = 48 * 1024 * 1024
BIG_VMEM_LIMIT = 56 * 1024 * 1024

TM_PROJ = 1024
TQ_FOX = 512
TK_FOX = 512
FOX_GROUP = 8
TS_GLA = 512
GLA_CHUNKS_PER_STEP = 4
TM_OPROJ = 1024
TM_ROUTE = 512
EXPERT_ROW_BLOCK = 256
EXPERT_UNIT_BLOCKS = 6
TM_FINAL = 512

SC_CORES = 2
SC_SUBCORES = 16
SC_ROW_WINDOW = 128
ROW_PARTS = 2
PART_WORDS = D_MODEL // 2 // ROW_PARTS


def _cparams(*sem, vmem=VMEM_LIMIT):
    return pltpu.CompilerParams(dimension_semantics=sem, vmem_limit_bytes=vmem)


def _layer_norm(x, g, b):
    mu = jnp.mean(x, axis=-1, keepdims=True)
    xc = x - mu
    var = jnp.mean(xc * xc, axis=-1, keepdims=True)
    return xc * lax.rsqrt(var + LN_EPS) * g + b


def _log_sigmoid(z):
    return jnp.minimum(z, 0.0) - jnp.log1p(jnp.exp(-jnp.abs(z)))


def _sigmoid(z):
    return 1.0 / (1.0 + jnp.exp(-z))


def _dot(a, b):
    return jnp.dot(a, b, preferred_element_type=F32)


def _dot_nt(a, b):
    return lax.dot_general(a, b, (((1,), (1,)), ((), ())), preferred_element_type=F32)


def _dot_tn(a, b):
    return lax.dot_general(a, b, (((0,), (0,)), ((), ())), preferred_element_type=F32)


def _pack_bf16_pairs(x):
    n = x.shape[1] // 2
    u = lax.bitcast_convert_type(x.astype(BF16).astype(F32), jnp.uint32)
    return (u[:, :n] >> 16) | u[:, n:]


def _unpack_bf16_pairs(w):
    lo = lax.bitcast_convert_type(w << 16, F32)
    hi = lax.bitcast_convert_type(w & jnp.uint32(0xFFFF0000), F32)
    return lo, hi


def _store_row_parts(x, part_refs):
    packed = _pack_bf16_pairs(x)
    for p, ref in enumerate(part_refs):
        ref[...] = packed[:, p * PART_WORDS:(p + 1) * PART_WORDS]


def _load_row_parts(parts):
    out = []
    for p, w in enumerate(parts):
        lo, hi = _unpack_bf16_pairs(w)
        out.append((p * PART_WORDS, lo))
        out.append((D_MODEL // 2 + p * PART_WORDS, hi))
    return sorted(out, key=lambda t: t[0])


def _proj_kernel(x_ref, g_ref, b_ref, wm_ref, ws_ref,
                 h_ref, fq_ref, fk_ref, fv_ref, gq_ref, gk_ref, gv_ref, gr_ref, sm_ref):
    h = _layer_norm(x_ref[...], g_ref[...], b_ref[...])
    h_ref[...] = h
    hb = h.astype(BF16)
    off = 0
    for ref in (fq_ref, fk_ref, fv_ref, gq_ref, gk_ref, gv_ref, gr_ref):
        n = ref.shape[1]
        ref[...] = _dot(hb, wm_ref[:, off:off + n]).astype(ref.dtype)
        off += n
    sm_ref[...] = _dot(hb, ws_ref[...])


def _proj(x2, g, b, w_main, w_small):
    T, D = x2.shape
    tm = TM_PROJ
    widths = (FOX_WIDTH, FOX_WIDTH, FOX_WIDTH, GLA_KEY_WIDTH, GLA_KEY_WIDTH, GLA_WIDTH, GLA_WIDTH)
    row = lambda n: pl.BlockSpec((tm, n), lambda i: (i, 0))
    full = lambda a: pl.BlockSpec(a.shape, lambda i: (0,) * a.ndim, pipeline_mode=pl.Buffered(1))
    out_shape = [jax.ShapeDtypeStruct((T, D), F32)]
    out_shape += [jax.ShapeDtypeStruct((T, n), BF16) for n in widths]
    out_shape += [jax.ShapeDtypeStruct((T, LANES), F32)]
    out_specs = [row(D)] + [row(n) for n in widths] + [row(LANES)]
    return pl.pallas_call(
        _proj_kernel, grid=(T // tm,),
        in_specs=[row(D), full(g), full(b), full(w_main), full(w_small)],
        out_specs=out_specs, out_shape=out_shape,
        compiler_params=_cparams("parallel", vmem=BIG_VMEM_LIMIT), name="proj")(x2, g, b, w_main, w_small)


def _fgate_kernel(sm_ref, bf_ref, c_ref):
    S = sm_ref.shape[0]
    lf = _log_sigmoid(sm_ref[...] + bf_ref[...])
    rows = lax.broadcasted_iota(jnp.int32, lf.shape, 0)
    s = 1
    while s < S:
        lf = lf + jnp.where(rows >= s, pltpu.roll(lf, s, axis=0), 0.0)
        s *= 2
    c_ref[...] = lf * LOG2E


def _fgate(small, bf_pad, B, S):
    T = small.shape[0]
    return pl.pallas_call(
        _fgate_kernel, grid=(B,),
        in_specs=[pl.BlockSpec((S, LANES), lambda b: (b, 0)),
                  pl.BlockSpec((1, LANES), lambda b: (0, 0))],
        out_specs=pl.BlockSpec((S, LANES), lambda b: (b, 0)),
        out_shape=jax.ShapeDtypeStruct((T, LANES), F32),
        compiler_params=_cparams("parallel"), name="fgate")(small, bf_pad)


def _split3(x):
    hi = x.astype(BF16).astype(F32)
    r = x - hi
    mid = r.astype(BF16).astype(F32)
    return hi, mid, (r - mid).astype(BF16).astype(F32)


def _fox_bias_maps():
    dh = FOX_HEAD_DIM
    maps = np.zeros((2, FOX_HEADS, 3 * LANES, LANES), np.float32)
    for is_key in (0, 1):
        for h in range(FOX_HEADS):
            for piece in range(3):
                maps[is_key, h, piece * LANES + h, dh + 3 * is_key + piece] = 1.0 - 2.0 * is_key
    return jnp.asarray(maps, BF16)


def _gate_pieces(c_tile):
    return jnp.concatenate([t.astype(BF16) for t in _split3(c_tile)], axis=1)


def _fox_operand(pair_ref_tile, odd, pieces, lane_map, is_key):
    dh = FOX_HEAD_DIM
    x = pair_ref_tile.astype(F32)
    if odd:
        x = pltpu.roll(x, dh, axis=1)
    lane = lax.broadcasted_iota(jnp.int32, x.shape, 1)
    ones_lo = dh if is_key else dh + 3
    bias = jnp.where((lane >= ones_lo) & (lane < ones_lo + 3), 1.0, _dot(pieces, lane_map))
    return jnp.where(lane < dh, x, bias).astype(BF16)


def _fox_kernel(q_ref, k_ref, v_ref, cq_ref, ck_ref, map_ref, o_ref, ka_ref, vt_ref, *, tq, tk):
    i = pl.program_id(1)
    S = k_ref.shape[0]
    dh = FOX_HEAD_DIM

    @pl.when(i == 0)
    def _():
        def fill(r, carry):
            r0 = pl.multiple_of(r * tk, tk)
            pieces = _gate_pieces(ck_ref[pl.ds(r0, tk), :])
            for h in range(FOX_HEADS):
                pair = slice(LANES * (h // 2), LANES * (h // 2 + 1))
                ka_ref[pl.ds(r0, tk), LANES * h:LANES * (h + 1)] = _fox_operand(
                    k_ref[pl.ds(r0, tk), pair], h % 2, pieces, map_ref[1, h], True)
            vt_ref[r] = v_ref[pl.ds(r0, tk), :].astype(F32).T.astype(BF16)
            return carry
        lax.fori_loop(0, S // tk, fill, 0)

    q_pieces = _gate_pieces(cq_ref[...])

    key = lax.broadcasted_iota(jnp.int32, (tk, tq), 0)
    qry = lax.broadcasted_iota(jnp.int32, (tk, tq), 1)
    n_diag = tq // tk
    n_full = i * n_diag

    for p in range(FOX_HEADS // FOX_GROUP):
        group = slice(dh * FOX_GROUP * p, dh * FOX_GROUP * (p + 1))
        heads = tuple(range(FOX_GROUP * p, FOX_GROUP * (p + 1)))
        qa = [_fox_operand(q_ref[:, LANES * (h // 2):LANES * (h // 2 + 1)], h % 2, q_pieces, map_ref[0, h], False)
              for h in heads]

        def logits(j):
            r0 = pl.multiple_of(j * tk, tk)
            return [_dot_nt(ka_ref[pl.ds(r0, tk), LANES * h:LANES * (h + 1)], qa[n]) for n, h in enumerate(heads)]

        def step(j, carry, diag):
            ss = logits(j)
            if diag is not None:
                ss = [jnp.where(qry >= key + diag * tk, s, -jnp.inf) for s in ss]
            ms = [jnp.maximum(c[0], jnp.max(s, axis=0, keepdims=True)) for c, s in zip(carry, ss)]
            prs = [jnp.exp2(s - m) for s, m in zip(ss, ms)]
            alphas = [jnp.exp2(c[0] - m) for c, m in zip(carry, ms)]
            ls = [a * c[1] + jnp.sum(pr, axis=0, keepdims=True) for a, c, pr in zip(alphas, carry, prs)]
            pvs = [_dot(vt_ref[j, dh * h:dh * (h + 1), :], pr.astype(BF16)) for h, pr in zip(heads, prs)]
            accs = [a * c[2] + pv for a, c, pv in zip(alphas, carry, pvs)]
            return tuple(zip(ms, ls, accs))

        init = tuple((jnp.full((1, tq), -jnp.inf, F32), jnp.zeros((1, tq), F32), jnp.zeros((dh, tq), F32))
                     for _ in heads)
        carry = lax.fori_loop(0, n_full, functools.partial(step, diag=None), init)
        for d in range(n_diag):
            carry = step(n_full + d, carry, d)
        o_ref[:, group] = jnp.concatenate([(acc / l).T for _, l, acc in carry], axis=1).astype(o_ref.dtype)


def _fox(fq, fk, fv, c, B, S):
    T = fq.shape[0]
    tq, tk = TQ_FOX, TK_FOX
    nq = S // tq
    once = pl.Buffered(1)
    maps = _fox_bias_maps()
    return pl.pallas_call(
        functools.partial(_fox_kernel, tq=tq, tk=tk), grid=(B, nq),
        in_specs=[pl.BlockSpec((tq, FOX_WIDTH), lambda b, i: (b * nq + i, 0)),
                  pl.BlockSpec((S, FOX_WIDTH), lambda b, i: (b, 0), pipeline_mode=once),
                  pl.BlockSpec((S, FOX_WIDTH), lambda b, i: (b, 0), pipeline_mode=once),
                  pl.BlockSpec((tq, LANES), lambda b, i: (b * nq + i, 0)),
                  pl.BlockSpec((S, LANES), lambda b, i: (b, 0), pipeline_mode=once),
                  pl.BlockSpec(maps.shape, lambda b, i: (0, 0, 0, 0), pipeline_mode=once)],
        out_specs=pl.BlockSpec((tq, FOX_WIDTH), lambda b, i: (b * nq + i, 0)),
        out_shape=jax.ShapeDtypeStruct((T, FOX_WIDTH), BF16),
        scratch_shapes=[pltpu.VMEM((S, FOX_HEADS * LANES), BF16), pltpu.VMEM((S // tk, FOX_WIDTH, tk), BF16)],
        compiler_params=_cparams("parallel", "arbitrary", vmem=BIG_VMEM_LIMIT), name="fox")(
            fq, fk, fv, c, c, maps)


def _gla_kernel(q_ref, k_ref, v_ref, r_ref, sm_ref, wg_ref, bg_ref, gn_ref, tri_ref, y_ref,
                st_ref, b_ref, o_ref, *, ts):
    dk, dv, C = GLA_KEY_DIM, GLA_VALUE_DIM, CHUNK

    @pl.when(pl.program_id(1) == 0)
    def _():
        st_ref[...] = jnp.zeros_like(st_ref)

    z = _dot(sm_ref[...].astype(BF16), wg_ref[...]) + bg_ref[...]
    la = _log_sigmoid(z) * (1.0 / GLA_GATE_TEMP)
    hi = la.astype(BF16)
    r1 = la - hi.astype(F32)
    mid = r1.astype(BF16)
    lo = (r1 - mid.astype(F32)).astype(BF16)
    tri = tri_ref[...]
    b_ref[...] = _dot(tri, hi) + _dot(tri, mid) + _dot(tri, lo)

    row = lax.broadcasted_iota(jnp.int32, (C, C), 0)
    col = lax.broadcasted_iota(jnp.int32, (C, C), 1)
    trow = lax.broadcasted_iota(jnp.int32, (C, LANES), 0)
    lane = lax.broadcasted_iota(jnp.int32, (C, LANES), 1)
    head_lanes = [lane < dk, lane >= dk]

    def level_reference(b, m):
        if m >= 4:
            n = 2 * m
            return jnp.concatenate(
                [jnp.broadcast_to(b[s + m - 1:s + m, :], (n, LANES)) for s in range(0, C, n)], axis=0)
        up1 = pltpu.roll(b, 1, axis=0)
        if m == 1:
            return jnp.where(trow % 2 == 0, b, up1)
        up2 = pltpu.roll(b, 2, axis=0)
        down1 = pltpu.roll(b, C - 1, axis=0)
        tm = trow % 4
        return jnp.where(tm == 0, down1, jnp.where(tm == 1, b, jnp.where(tm == 2, up1, up2)))

    def chunk(r0):
        for p in range(GLA_HEADS // 2):
            kl = slice(p * LANES, (p + 1) * LANES)
            q = q_ref[pl.ds(r0, C), kl].astype(F32)
            k = k_ref[pl.ds(r0, C), kl].astype(F32)
            b = b_ref[pl.ds(r0, C), kl]
            st = st_ref[p]
            stb = st.astype(BF16)
            b_last = b[C - 1:C, :]
            qb, kb = q.astype(BF16), k.astype(BF16)
            q_in = (q * jnp.exp(b)).astype(BF16)
            zero = jnp.zeros_like(qb)

            scores = [jnp.where(row == col, _dot_nt(jnp.where(hl, qb, zero), kb), 0.0) for hl in head_lanes]
            m = C // 2
            while m >= 1:
                ref = level_reference(b, m)
                upper = (trow % (2 * m)) >= m
                qt = jnp.where(upper, q * jnp.exp(jnp.minimum(b - ref, 0.0)), 0.0).astype(BF16)
                kt = jnp.where(upper, 0.0, k * jnp.exp(jnp.minimum(ref - b, 0.0))).astype(BF16)
                same_block = (row // (2 * m)) == (col // (2 * m))
                scores = [sc + jnp.where(same_block, _dot_nt(jnp.where(hl, qt, zero), kt), 0.0)
                          for sc, hl in zip(scores, head_lanes)]
                m //= 2

            kh = (k * jnp.exp(b_last - b)).astype(BF16)
            upd = []
            for n, hl in enumerate(head_lanes):
                h = 2 * p + n
                vl = slice(h * dv, (h + 1) * dv)
                v = v_ref[pl.ds(r0, C), vl]
                o_ref[pl.ds(r0, C), vl] = (_dot_nt(jnp.where(hl, q_in, zero), stb)
                                           + _dot(scores[n].astype(BF16), v))
                upd.append(_dot_tn(v, kh))
            st_ref[p] = st * jnp.exp(b_last) + jnp.where(lax.broadcasted_iota(jnp.int32, st.shape, 1) < dk,
                                                         upd[0], upd[1])

    def chunk_group(g, carry):
        for sub in range(GLA_CHUNKS_PER_STEP):
            chunk(pl.multiple_of((g * GLA_CHUNKS_PER_STEP + sub) * C, C))
        return carry

    lax.fori_loop(0, ts // (C * GLA_CHUNKS_PER_STEP), chunk_group, 0)

    for h in range(GLA_HEADS):
        vl = slice(h * dv, (h + 1) * dv)
        o = o_ref[:, vl]
        ms = jnp.mean(o * o, axis=-1, keepdims=True)
        g = r_ref[:, vl].astype(F32)
        y = o * lax.rsqrt(ms + LN_EPS) * gn_ref[...] * (g * _sigmoid(g))
        y_ref[:, vl] = y.astype(y_ref.dtype)


def _gla(gq, gk, gv, gr, small, wg_pad, bg, gnorm, tri, B, S):
    T = gq.shape[0]
    ts = TS_GLA
    ns = S // ts
    row = lambda n: pl.BlockSpec((ts, n), lambda b, i: (b * ns + i, 0))
    full = lambda a: pl.BlockSpec(a.shape, lambda b, i: (0,) * a.ndim)
    return pl.pallas_call(
        functools.partial(_gla_kernel, ts=ts), grid=(B, ns),
        in_specs=[row(GLA_KEY_WIDTH), row(GLA_KEY_WIDTH), row(GLA_WIDTH), row(GLA_WIDTH), row(LANES),
                  full(wg_pad), full(bg), full(gnorm), full(tri)],
        out_specs=row(GLA_WIDTH),
        out_shape=jax.ShapeDtypeStruct((T, GLA_WIDTH), BF16),
        scratch_shapes=[pltpu.VMEM((GLA_HEADS // 2, GLA_VALUE_DIM, LANES), F32),
                        pltpu.VMEM((ts, GLA_KEY_WIDTH), F32),
                        pltpu.VMEM((ts, GLA_WIDTH), F32)],
        compiler_params=_cparams("parallel", "arbitrary"), name="gla")(
            gq, gk, gv, gr, small, wg_pad, bg, gnorm, tri)


def _oproj_kernel(yf_ref, yg_ref, h_ref, wo_ref, g_ref, b_ref, wr_ref, h1_ref, st_ref, *part_refs):
    mix = _dot(yf_ref[...], wo_ref[0:FOX_WIDTH, :]) + _dot(yg_ref[...], wo_ref[FOX_WIDTH:, :])
    h1 = _layer_norm(DEEPNORM_ALPHA * h_ref[...] + mix, g_ref[...], b_ref[...])
    h1_ref[...] = h1
    _store_row_parts(h1, part_refs)
    st_ref[...] = _sigmoid(_dot_nt(wr_ref[...], h1.astype(BF16)))


def _oproj(y_fox, y_gla, h, w_out, g, b, w_router_t):
    T, D = h.shape
    tm = TM_OPROJ
    row = lambda n: pl.BlockSpec((tm, n), lambda i: (i, 0))
    full = lambda a: pl.BlockSpec(a.shape, lambda i: (0,) * a.ndim, pipeline_mode=pl.Buffered(1))
    return pl.pallas_call(
        _oproj_kernel, grid=(T // tm,),
        in_specs=[row(FOX_WIDTH), row(GLA_WIDTH), row(D), full(w_out), full(g), full(b), full(w_router_t)],
        out_specs=[row(D), pl.BlockSpec((N_EXPERTS, tm), lambda i: (0, i))] + [row(PART_WORDS)] * ROW_PARTS,
        out_shape=[jax.ShapeDtypeStruct((T, D), F32), jax.ShapeDtypeStruct((N_EXPERTS, T), F32)]
        + [jax.ShapeDtypeStruct((T, PART_WORDS), jnp.uint32)] * ROW_PARTS,
        compiler_params=_cparams("parallel"), name="oproj")(y_fox, y_gla, h, w_out, g, b, w_router_t)


def _route_kernel(s_ref, bias_ref, upper_ref, ones_ref, idx_ref, w_ref, rank_ref, cnt_ref, carry_ref):
    E, tm = s_ref.shape

    @pl.when(pl.program_id(0) == 0)
    def _():
        carry_ref[...] = jnp.zeros_like(carry_ref)

    s = s_ref[...]
    biased = s + bias_ref[...]
    neg = -jnp.inf
    erow = lax.broadcasted_iota(jnp.int32, (E, tm), 0).astype(F32)
    grow = lax.broadcasted_iota(jnp.int32, (GROUP_SIZE, tm), 0).astype(F32)

    gs = []
    for g in range(N_GROUPS):
        blk = biased[g * GROUP_SIZE:(g + 1) * GROUP_SIZE, :]
        m1 = jnp.max(blk, axis=0, keepdims=True)
        i1 = jnp.min(jnp.where(blk == m1, grow, float(GROUP_SIZE)), axis=0, keepdims=True)
        m2 = jnp.max(jnp.where(grow == i1, neg, blk), axis=0, keepdims=True)
        gs.append(m1 + m2)
    keep = []
    for g in range(N_GROUPS):
        beaten = jnp.zeros((1, tm), F32)
        for o in range(N_GROUPS):
            if o == g:
                continue
            wins = (gs[o] > gs[g]) | ((gs[o] == gs[g]) & (o < g))
            beaten = beaten + jnp.where(wins, 1.0, 0.0)
        keep.append(jnp.broadcast_to(beaten < float(TOPK_GROUPS), (GROUP_SIZE, tm)))
    cur = jnp.where(jnp.concatenate(keep, axis=0), biased, neg)

    ids, ws, hots = [], [], []
    chosen = jnp.zeros((E, tm), F32)
    for _ in range(TOP_K):
        m = jnp.max(cur, axis=0, keepdims=True)
        ik = jnp.min(jnp.where(cur == m, erow, float(E)), axis=0, keepdims=True)
        hot = erow == ik
        ws.append(jnp.sum(jnp.where(hot, s, 0.0), axis=0, keepdims=True))
        cur = jnp.where(hot, neg, cur)
        chosen = jnp.where(hot, 1.0, chosen)
        ids.append(ik)
        hots.append(hot)
    wsum = ws[0]
    for w in ws[1:]:
        wsum = wsum + w

    chosen_b = chosen.astype(BF16)
    before = carry_ref[...] + _dot(chosen_b, upper_ref[...])
    carry_ref[...] = carry_ref[...] + _dot(chosen_b, ones_ref[...])
    for k in range(TOP_K):
        idx_ref[k:k + 1, :] = ids[k].astype(jnp.int32)
        w_ref[k:k + 1, :] = ws[k] / wsum * ROUTED_SCALE
        rank_ref[k:k + 1, :] = jnp.sum(jnp.where(hots[k], before, 0.0), axis=0, keepdims=True).astype(jnp.int32)
    cnt_ref[...] = carry_ref[:, 0:LANES]


def _route(scores_t, bias_col, upper, ones):
    E, T = scores_t.shape
    tm = TM_ROUTE
    full = lambda a: pl.BlockSpec(a.shape, lambda i: (0,) * a.ndim)
    kt = pl.BlockSpec((TOP_K, tm), lambda i: (0, i))
    return pl.pallas_call(
        _route_kernel, grid=(T // tm,),
        in_specs=[pl.BlockSpec((E, tm), lambda i: (0, i)), full(bias_col), full(upper), full(ones)],
        out_specs=[kt, kt, kt, pl.BlockSpec((E, LANES), lambda i: (0, 0))],
        out_shape=[jax.ShapeDtypeStruct((TOP_K, T), jnp.int32), jax.ShapeDtypeStruct((TOP_K, T), F32),
                   jax.ShapeDtypeStruct((TOP_K, T), jnp.int32), jax.ShapeDtypeStruct((E, LANES), F32)],
        scratch_shapes=[pltpu.VMEM((E, tm), F32)],
        compiler_params=_cparams("arbitrary"), name="route")(scores_t, bias_col, upper, ones)


def _expert_kernel(ufirst_ref, ucount_ref, ustart_ref, usize_ref, total_ref, *refs):
    x_hbm = refs[:ROW_PARTS]
    wg_ref, wu_ref, wd_ref = refs[ROW_PARTS:ROW_PARTS + 3]
    y_hbm = refs[ROW_PARTS + 3:2 * ROW_PARTS + 3]
    xbuf, ybuf, wgb, wub, wdb, in_sem, out_sem = refs[2 * ROW_PARTS + 3:]
    e = pl.program_id(0)
    blk = EXPERT_ROW_BLOCK
    total = total_ref[0]

    def hbm_rows(u, k):
        return pl.ds(pl.multiple_of(ustart_ref[u] * blk, blk), k * blk)

    def in_copy(u, k, p):
        return pltpu.make_async_copy(x_hbm[p].at[hbm_rows(u, k)], xbuf.at[u % 2, p, pl.ds(0, k * blk)],
                                     in_sem.at[u % 2, p])

    def out_copy(u, k, p):
        return pltpu.make_async_copy(ybuf.at[u % 2, p, pl.ds(0, k * blk)], y_hbm[p].at[hbm_rows(u, k)],
                                     out_sem.at[u % 2, p])

    def for_size(u, fn):
        for k in range(1, EXPERT_UNIT_BLOCKS + 1):
            pl.when(usize_ref[u] == k)(functools.partial(fn, k))

    def start_read(u):
        def go(k):
            for p in range(ROW_PARTS):
                in_copy(u, k, p).start()
        for_size(u, go)

    def wait_read(u):
        def go(k):
            for p in range(ROW_PARTS):
                in_copy(u, k, p).wait()
        for_size(u, go)

    def wait_write(u):
        def go(k):
            for p in range(ROW_PARTS):
                out_copy(u, k, p).wait()
        for_size(u, go)

    def compute(u, k):
        slot = u % 2
        rows = pl.ds(0, k * blk)
        gate = up = None
        for c0, xc in _load_row_parts([xbuf[slot, p, rows] for p in range(ROW_PARTS)]):
            xb = xc.astype(BF16)
            wrows = slice(c0, c0 + PART_WORDS)
            gc = _dot(xb, wgb[wrows, :])
            uc = _dot(xb, wub[wrows, :])
            gate = gc if gate is None else gate + gc
            up = uc if up is None else up + uc
        y = _dot((gate * _sigmoid(gate) * up).astype(BF16), wdb[...])
        _store_row_parts(y, [ybuf.at[slot, p, rows] for p in range(ROW_PARTS)])
        for p in range(ROW_PARTS):
            out_copy(u, k, p).start()

    @pl.when((e == 0) & (total > 0))
    def _():
        start_read(0)

    @pl.when(ucount_ref[e] > 0)
    def _():
        wgb[...] = wg_ref[...].astype(BF16)
        wub[...] = wu_ref[...].astype(BF16)
        wdb[...] = wd_ref[...].astype(BF16)

        def unit(c, carry):
            u = ufirst_ref[e] + c
            wait_read(u)
            pl.when(u + 1 < total)(lambda: start_read(u + 1))
            pl.when(u >= 2)(lambda: wait_write(u - 2))
            for_size(u, lambda k: compute(u, k))
            return carry

        lax.fori_loop(0, ucount_ref[e], unit, 0)

    @pl.when(e == pl.num_programs(0) - 1)
    def _():
        for back in (2, 1):
            pl.when(total >= back)(lambda back=back: wait_write(total - back))


def _experts(ufirst, ucount, ustart, usize, total, xs_parts, w_gate, w_up, w_down):
    P = xs_parts[0].shape[0]
    D = D_MODEL
    unit_rows = EXPERT_UNIT_BLOCKS * EXPERT_ROW_BLOCK
    hbm = pl.BlockSpec(memory_space=pl.ANY)
    grid_spec = pltpu.PrefetchScalarGridSpec(
        num_scalar_prefetch=5, grid=(N_EXPERTS,),
        in_specs=[hbm] * ROW_PARTS + [
            pl.BlockSpec((None, D, EXPERT_DIM), lambda e, *_: (e, 0, 0)),
            pl.BlockSpec((None, D, EXPERT_DIM), lambda e, *_: (e, 0, 0)),
            pl.BlockSpec((None, EXPERT_DIM, D), lambda e, *_: (e, 0, 0))],
        out_specs=[hbm] * ROW_PARTS,
        scratch_shapes=[pltpu.VMEM((2, ROW_PARTS, unit_rows, PART_WORDS), jnp.uint32),
                        pltpu.VMEM((2, ROW_PARTS, unit_rows, PART_WORDS), jnp.uint32),
                        pltpu.VMEM((D, EXPERT_DIM), BF16), pltpu.VMEM((D, EXPERT_DIM), BF16),
                        pltpu.VMEM((EXPERT_DIM, D), BF16),
                        pltpu.SemaphoreType.DMA((2, ROW_PARTS)), pltpu.SemaphoreType.DMA((2, ROW_PARTS))])
    return pl.pallas_call(
        _expert_kernel, grid_spec=grid_spec,
        out_shape=[jax.ShapeDtypeStruct((P, PART_WORDS), jnp.uint32)] * ROW_PARTS,
        compiler_params=_cparams("arbitrary", vmem=BIG_VMEM_LIMIT), name="experts")(
            ufirst, ucount, ustart, usize, total, *xs_parts, w_gate, w_up, w_down)


def _pos_kernel(idx_ref, rank_ref, start_ref, pos_ref):
    K, tm = idx_ref.shape
    E = start_ref.shape[0]
    erow = lax.broadcasted_iota(jnp.int32, (E, tm), 0)
    start = start_ref[...]
    for k in range(K):
        first = jnp.sum(jnp.where(erow == idx_ref[k:k + 1, :], start, 0.0), axis=0, keepdims=True)
        pos_ref[k:k + 1, :] = first.astype(jnp.int32) + rank_ref[k:k + 1, :]


def _positions(idx_t, rank_t, start_col):
    K, T = idx_t.shape
    tm = TM_ROUTE
    kt = pl.BlockSpec((K, tm), lambda i: (0, i))
    return pl.pallas_call(
        _pos_kernel, grid=(T // tm,),
        in_specs=[kt, kt, pl.BlockSpec(start_col.shape, lambda i: (0, 0))],
        out_specs=kt, out_shape=jax.ShapeDtypeStruct((K, T), jnp.int32),
        compiler_params=_cparams("parallel"), name="positions")(idx_t, rank_t, start_col)


def _sc_mesh():
    return plsc.VectorSubcoreMesh(core_axis_name="core", subcore_axis_name="subcore",
                                  num_cores=SC_CORES, num_subcores=SC_SUBCORES)


def _sc_scatter_rows(parts, pos, n_out):
    T, W = parts[0].shape
    K = pos.shape[0]
    n = len(parts)
    win = SC_ROW_WINDOW

    @functools.partial(pl.kernel, out_type=[jax.ShapeDtypeStruct((n_out, W), parts[0].dtype)] * n, mesh=_sc_mesh(),
                       name="sc_dispatch")
    def k(*refs):
        x_hbms, p_hbm, o_hbms = refs[:n], refs[n], refs[n + 1:]
        for x_hbm, o_hbm in zip(x_hbms, o_hbms):
            def body(x_vmem, p_vmem, o_hbm=o_hbm):
                for j in range(K):
                    pltpu.sync_copy(x_vmem, o_hbm.at[p_vmem.at[j]])

            pltpu.emit_pipeline(
                body, grid=(T // win,),
                in_specs=[pl.BlockSpec((win, W), lambda i: (i, 0)), pl.BlockSpec((K, win), lambda i: (0, i))],
                out_specs=[], core_axis_name=("core", "subcore"),
                dimension_semantics=(pltpu.PARALLEL,))(x_hbm, p_hbm)

    return k(*parts, pos)


def _sc_gather_rows(tables, idx):
    M = idx.shape[0]
    W = tables[0].shape[1]
    n = len(tables)
    win = SC_ROW_WINDOW

    @functools.partial(pl.kernel, out_type=[jax.ShapeDtypeStruct((M, W), tables[0].dtype)] * n, mesh=_sc_mesh(),
                       name="sc_combine")
    def k(*refs):
        t_hbms, i_hbm, o_hbms = refs[:n], refs[n], refs[n + 1:]
        for t_hbm, o_hbm in zip(t_hbms, o_hbms):
            def body(i_vmem, o_vmem, t_hbm=t_hbm):
                pltpu.sync_copy(t_hbm.at[i_vmem.at[0]], o_vmem)

            pltpu.emit_pipeline(
                body, grid=(M // win,),
                in_specs=[pl.BlockSpec((1, win), lambda i: (0, i))],
                out_specs=[pl.BlockSpec((win, W), lambda i: (i, 0))],
                core_axis_name=("core", "subcore"), dimension_semantics=(pltpu.PARALLEL,))(i_hbm, o_hbm)

    return k(*tables, idx.reshape(1, M))


def _final_kernel(h1_ref, wk_ref, wg_ref, wu_ref, wd_ref, g_ref, b_ref, *refs):
    yk_refs, o_ref = refs[:ROW_PARTS], refs[ROW_PARTS]
    h1 = h1_ref[...]
    hb = h1.astype(BF16)
    g = _dot(hb, wg_ref[...])
    u = _dot(hb, wu_ref[...])
    ffn = _dot((g * _sigmoid(g) * u).astype(BF16), wd_ref[...])
    wk = wk_ref[...]
    sums = None
    for k in range(TOP_K):
        cols = _load_row_parts([r[k] for r in yk_refs])
        terms = [c * wk[:, k:k + 1] for _, c in cols]
        sums = terms if sums is None else [s + t for s, t in zip(sums, terms)]
    ffn = ffn + jnp.concatenate(sums, axis=1)
    o_ref[...] = _layer_norm(DEEPNORM_ALPHA * h1 + ffn, g_ref[...], b_ref[...])


def _final(h1, yk_parts, wk, w_sg, w_su, w_sd, g, b):
    T, D = h1.shape
    tm = TM_FINAL
    row = lambda n: pl.BlockSpec((tm, n), lambda i: (i, 0))
    full = lambda a: pl.BlockSpec(a.shape, lambda i: (0,) * a.ndim)
    return pl.pallas_call(
        _final_kernel, grid=(T // tm,),
        in_specs=[row(D), row(TOP_K), full(w_sg), full(w_su), full(w_sd), full(g), full(b)]
        + [pl.BlockSpec((TOP_K, tm, PART_WORDS), lambda i: (0, i, 0))] * ROW_PARTS,
        out_specs=row(D), out_shape=jax.ShapeDtypeStruct((T, D), F32),
        compiler_params=_cparams("parallel"), name="final")(h1, wk, w_sg, w_su, w_sd, g, b, *yk_parts)


def _block_diag_tri(n, c):
    r = np.arange(n)
    return jnp.asarray(((r[:, None] >= r[None, :]) & (r[:, None] // c == r[None, :] // c)).astype(np.float32), BF16)


def kernel(x, ln_in_g, ln_in_b, w_in, b_fgate, w_gate_up, b_gate, g_gla_norm, w_out, ln1_g, ln1_b, w_router,
           router_bias, w_exp_gate, w_exp_up, w_exp_down, w_sh_gate, w_sh_up, w_sh_down, ln2_g, ln2_b):
    B, S, D = x.shape
    T = B * S
    x2 = x.reshape(T, D)
    l = 0
    row = lambda a: a.reshape(1, -1)

    off = np.cumsum((0,) + IN_SPLITS)
    seg = lambda i: w_in[l][:, off[i]:off[i + 1]]
    w_main = jnp.concatenate([seg(0) * (FOX_HEAD_DIM ** -0.5 * LOG2E), seg(1), seg(2), seg(4) * GLA_KEY_DIM ** -0.5, seg(5),
                              seg(6), seg(8)], axis=1).astype(BF16)
    n_small = FOX_HEADS + GLA_GATE_RANK
    w_small = jnp.concatenate([seg(3), seg(7), jnp.zeros((D, LANES - n_small), F32)], axis=1).astype(BF16)
    bf_pad = jnp.concatenate([b_fgate[l], jnp.zeros((LANES - FOX_HEADS,), F32)]).reshape(1, LANES)
    wg_pad = jnp.zeros((LANES, GLA_KEY_WIDTH), F32).at[FOX_HEADS:n_small].set(w_gate_up[l]).astype(BF16)

    h, fq, fk, fv, gq, gk, gv, gr, small = _proj(x2, row(ln_in_g), row(ln_in_b), w_main, w_small)

    y_fox = _fox(fq, fk, fv, _fgate(small, bf_pad, B, S), B, S)

    y_gla = _gla(gq, gk, gv, gr, small, wg_pad, row(b_gate[l]), row(g_gla_norm[l]),
                 _block_diag_tri(TS_GLA, CHUNK), B, S)

    h1, scores_t, *h1_parts = _oproj(y_fox, y_gla, h, w_out[l].astype(BF16), row(ln1_g[l]), row(ln1_b[l]),
                                     w_router[l].T.astype(BF16))

    tm = TM_ROUTE
    r = np.arange(tm)
    upper = jnp.asarray((r[:, None] < r[None, :]).astype(np.float32), BF16)
    ones = jnp.ones((tm, tm), BF16)
    idx_t, w_t, rank_t, cnt = _route(scores_t, router_bias[l].reshape(N_EXPERTS, 1), upper, ones)

    blk, ub, E = EXPERT_ROW_BLOCK, EXPERT_UNIT_BLOCKS, N_EXPERTS
    A = T * TOP_K
    P = A + E * blk
    counts = cnt[:, 0].astype(jnp.int32)
    nblk = (counts + blk - 1) // blk
    blk_end = jnp.cumsum(nblk)
    blk_start = blk_end - nblk
    pos = _positions(idx_t, rank_t, (blk_start * blk).astype(F32).reshape(E, 1))
    nunit = (nblk + ub - 1) // ub
    unit_end = jnp.cumsum(nunit)
    ufirst = unit_end - nunit
    u = jnp.arange(E + (A // blk + E) // ub + 1, dtype=jnp.int32)
    owner = jnp.minimum(jnp.sum(unit_end[None, :] <= u[:, None], axis=1), E - 1)
    onehot = owner[:, None] == jnp.arange(E)[None, :]
    pick = lambda v: jnp.sum(jnp.where(onehot, v[None, :], 0), axis=1)
    j = u - pick(ufirst)
    n_units = jnp.maximum(pick(nunit), 1)
    base, rem = pick(nblk) // n_units, pick(nblk) % n_units
    live = u < unit_end[-1]
    usize = jnp.where(live, base + (j < rem), 1).astype(jnp.int32)
    ustart = jnp.where(live, pick(blk_start) + j * base + jnp.minimum(j, rem), 0).astype(jnp.int32)

    xs_parts = _sc_scatter_rows(h1_parts, pos, P)
    ys_parts = _experts(ufirst, nunit, ustart, usize, unit_end[-1:], xs_parts,
                        w_exp_gate[l], w_exp_up[l], w_exp_down[l])
    yk_parts = [y.reshape(TOP_K, T, PART_WORDS) for y in _sc_gather_rows(ys_parts, pos.reshape(-1))]

    out = _final(h1, yk_parts, w_t.T, w_sh_gate[l].astype(BF16), w_sh_up[l].astype(BF16),
                 w_sh_down[l].astype(BF16), row(ln2_g[l]), row(ln2_b[l]))
    return out.reshape(B, S, D)
```

```python
import functools

import jax
import jax.numpy as jnp
import numpy as np
from jax import lax
from jax.experimental import pallas as pl
from jax.experimental.pallas import tpu as pltpu
from jax.experimental.pallas import tpu_sc as plsc

F32 = jnp.float32
BF16 = jnp.bfloat16

D_MODEL = 1024
CHUNK = 64
LN_EPS = 1e-5
LOG2E = 1.4426950408889634
FOX_HEADS = 8
FOX_HEAD_DIM = 64
FOX_WIDTH = FOX_HEADS * FOX_HEAD_DIM
GLA_HEADS = 4
GLA_KEY_DIM = 64
GLA_VALUE_DIM = 128
GLA_KEY_WIDTH = GLA_HEADS * GLA_KEY_DIM
GLA_WIDTH = GLA_HEADS * GLA_VALUE_DIM
GLA_GATE_RANK = 16
GLA_GATE_TEMP = 16.0
N_EXPERTS = 256
N_GROUPS = 8
GROUP_SIZE = N_EXPERTS // N_GROUPS
TOPK_GROUPS = 4
TOP_K = 8
EXPERT_DIM = 256
SHARED_DIM = 256
ROUTED_SCALE = 2.5
DEPTH = 1
DEEPNORM_ALPHA = (2.0 * DEPTH) ** 0.25
IN_SPLITS = (FOX_WIDTH, FOX_WIDTH, FOX_WIDTH, FOX_HEADS, GLA_KEY_WIDTH, GLA_KEY_WIDTH, GLA_WIDTH,
             GLA_GATE_RANK, GLA_WIDTH)

LANES = 128
VMEM_LIMIT = 48 * 1024 * 1024
BIG_VMEM_LIMIT = 56 * 1024 * 1024

TM_PROJ = 1024
TQ_FOX = 512
TK_FOX = 512
FOX_GROUP = 8
TS_GLA = 512
GLA_CHUNKS_PER_STEP = 4
TM_OPROJ = 1024
TM_ROUTE = 512
EXPERT_ROW_BLOCK = 256
EXPERT_UNIT_BLOCKS = 6
TM_FINAL = 512

SC_CORES = 2
SC_SUBCORES = 16
SC_ROW_WINDOW = 128
ROW_PARTS = 2
PART_WORDS = D_MODEL // 2 // ROW_PARTS


def _cparams(*sem, vmem=VMEM_LIMIT):
    return pltpu.CompilerParams(dimension_semantics=sem, vmem_limit_bytes=vmem)


def _layer_norm(x, g, b):
    mu = jnp.mean(x, axis=-1, keepdims=True)
    xc = x - mu
    var = jnp.mean(xc * xc, axis=-1, keepdims=True)
    return xc * lax.rsqrt(var + LN_EPS) * g + b


def _log_sigmoid(z):
    return jnp.minimum(z, 0.0) - jnp.log1p(jnp.exp(-jnp.abs(z)))


def _sigmoid(z):
    return 1.0 / (1.0 + jnp.exp(-z))


def _dot(a, b):
    return jnp.dot(a, b, preferred_element_type=F32)


def _dot_nt(a, b):
    return lax.dot_general(a, b, (((1,), (1,)), ((), ())), preferred_element_type=F32)


def _dot_tn(a, b):
    return lax.dot_general(a, b, (((0,), (0,)), ((), ())), preferred_element_type=F32)


def _pack_bf16_pairs(x):
    n = x.shape[1] // 2
    u = lax.bitcast_convert_type(x.astype(BF16).astype(F32), jnp.uint32)
    return (u[:, :n] >> 16) | u[:, n:]


def _unpack_bf16_pairs(w):
    lo = lax.bitcast_convert_type(w << 16, F32)
    hi = lax.bitcast_convert_type(w & jnp.uint32(0xFFFF0000), F32)
    return lo, hi


def _store_row_parts(x, part_refs):
    packed = _pack_bf16_pairs(x)
    for p, ref in enumerate(part_refs):
        ref[...] = packed[:, p * PART_WORDS:(p + 1) * PART_WORDS]


def _load_row_parts(parts):
    out = []
    for p, w in enumerate(parts):
        lo, hi = _unpack_bf16_pairs(w)
        out.append((p * PART_WORDS, lo))
        out.append((D_MODEL // 2 + p * PART_WORDS, hi))
    return sorted(out, key=lambda t: t[0])


def _proj_kernel(x_ref, g_ref, b_ref, wm_ref, ws_ref,
                 h_ref, fq_ref, fk_ref, fv_ref, gq_ref, gk_ref, gv_ref, gr_ref, sm_ref):
    h = _layer_norm(x_ref[...], g_ref[...], b_ref[...])
    h_ref[...] = h
    hb = h.astype(BF16)
    off = 0
    for ref in (fq_ref, fk_ref, fv_ref, gq_ref, gk_ref, gv_ref, gr_ref):
        n = ref.shape[1]
        ref[...] = _dot(hb, wm_ref[:, off:off + n]).astype(ref.dtype)
        off += n
    sm_ref[...] = _dot(hb, ws_ref[...])


def _proj(x2, g, b, w_main, w_small):
    T, D = x2.shape
    tm = TM_PROJ
    widths = (FOX_WIDTH, FOX_WIDTH, FOX_WIDTH, GLA_KEY_WIDTH, GLA_KEY_WIDTH, GLA_WIDTH, GLA_WIDTH)
    row = lambda n: pl.BlockSpec((tm, n), lambda i: (i, 0))
    full = lambda a: pl.BlockSpec(a.shape, lambda i: (0,) * a.ndim, pipeline_mode=pl.Buffered(1))
    out_shape = [jax.ShapeDtypeStruct((T, D), F32)]
    out_shape += [jax.ShapeDtypeStruct((T, n), BF16) for n in widths]
    out_shape += [jax.ShapeDtypeStruct((T, LANES), F32)]
    out_specs = [row(D)] + [row(n) for n in widths] + [row(LANES)]
    return pl.pallas_call(
        _proj_kernel, grid=(T // tm,),
        in_specs=[row(D), full(g), full(b), full(w_main), full(w_small)],
        out_specs=out_specs, out_shape=out_shape,
        compiler_params=_cparams("parallel", vmem=BIG_VMEM_LIMIT), name="proj")(x2, g, b, w_main, w_small)


def _fgate_kernel(sm_ref, bf_ref, c_ref):
    S = sm_ref.shape[0]
    lf = _log_sigmoid(sm_ref[...] + bf_ref[...])
    rows = lax.broadcasted_iota(jnp.int32, lf.shape, 0)
    s = 1
    while s < S:
        lf = lf + jnp.where(rows >= s, pltpu.roll(lf, s, axis=0), 0.0)
        s *= 2
    c_ref[...] = lf * LOG2E


def _fgate(small, bf_pad, B, S):
    T = small.shape[0]
    return pl.pallas_call(
        _fgate_kernel, grid=(B,),
        in_specs=[pl.BlockSpec((S, LANES), lambda b: (b, 0)),
                  pl.BlockSpec((1, LANES), lambda b: (0, 0))],
        out_specs=pl.BlockSpec((S, LANES), lambda b: (b, 0)),
        out_shape=jax.ShapeDtypeStruct((T, LANES), F32),
        compiler_params=_cparams("parallel"), name="fgate")(small, bf_pad)


def _split3(x):
    hi = x.astype(BF16).astype(F32)
    r = x - hi
    mid = r.astype(BF16).astype(F32)
    return hi, mid, (r - mid).astype(BF16).astype(F32)


def _fox_bias_maps():
    dh = FOX_HEAD_DIM
    maps = np.zeros((2, FOX_HEADS, 3 * LANES, LANES), np.float32)
    for is_key in (0, 1):
        for h in range(FOX_HEADS):
            for piece in range(3):
                maps[is_key, h, piece * LANES + h, dh + 3 * is_key + piece] = 1.0 - 2.0 * is_key
    return jnp.asarray(maps, BF16)


def _gate_pieces(c_tile):
    return jnp.concatenate([t.astype(BF16) for t in _split3(c_tile)], axis=1)


def _fox_operand(pair_ref_tile, odd, pieces, lane_map, is_key):
    dh = FOX_HEAD_DIM
    x = pair_ref_tile.astype(F32)
    if odd:
        x = pltpu.roll(x, dh, axis=1)
    lane = lax.broadcasted_iota(jnp.int32, x.shape, 1)
    ones_lo = dh if is_key else dh + 3
    bias = jnp.where((lane >= ones_lo) & (lane < ones_lo + 3), 1.0, _dot(pieces, lane_map))
    return jnp.where(lane < dh, x, bias).astype(BF16)


def _fox_kernel(q_ref, k_ref, v_ref, cq_ref, ck_ref, map_ref, o_ref, ka_ref, vt_ref, *, tq, tk):
    i = pl.program_id(1)
    S = k_ref.shape[0]
    dh = FOX_HEAD_DIM

    @pl.when(i == 0)
    def _():
        def fill(r, carry):
            r0 = pl.multiple_of(r * tk, tk)
            pieces = _gate_pieces(ck_ref[pl.ds(r0, tk), :])
            for h in range(FOX_HEADS):
                pair = slice(LANES * (h // 2), LANES * (h // 2 + 1))
                ka_ref[pl.ds(r0, tk), LANES * h:LANES * (h + 1)] = _fox_operand(
                    k_ref[pl.ds(r0, tk), pair], h % 2, pieces, map_ref[1, h], True)
            vt_ref[r] = v_ref[pl.ds(r0, tk), :].astype(F32).T.astype(BF16)
            return carry
        lax.fori_loop(0, S // tk, fill, 0)

    q_pieces = _gate_pieces(cq_ref[...])

    key = lax.broadcasted_iota(jnp.int32, (tk, tq), 0)
    qry = lax.broadcasted_iota(jnp.int32, (tk, tq), 1)
    n_diag = tq // tk
    n_full = i * n_diag

    for p in range(FOX_HEADS // FOX_GROUP):
        group = slice(dh * FOX_GROUP * p, dh * FOX_GROUP * (p + 1))
        heads = tuple(range(FOX_GROUP * p, FOX_GROUP * (p + 1)))
        qa = [_fox_operand(q_ref[:, LANES * (h // 2):LANES * (h // 2 + 1)], h % 2, q_pieces, map_ref[0, h], False)
              for h in heads]

        def logits(j):
            r0 = pl.multiple_of(j * tk, tk)
            return [_dot_nt(ka_ref[pl.ds(r0, tk), LANES * h:LANES * (h + 1)], qa[n]) for n, h in enumerate(heads)]

        def step(j, carry, diag):
            ss = logits(j)
            if diag is not None:
                ss = [jnp.where(qry >= key + diag * tk, s, -jnp.inf) for s in ss]
            ms = [jnp.maximum(c[0], jnp.max(s, axis=0, keepdims=True)) for c, s in zip(carry, ss)]
            prs = [jnp.exp2(s - m) for s, m in zip(ss, ms)]
            alphas = [jnp.exp2(c[0] - m) for c, m in zip(carry, ms)]
            ls = [a * c[1] + jnp.sum(pr, axis=0, keepdims=True) for a, c, pr in zip(alphas, carry, prs)]
            pvs = [_dot(vt_ref[j, dh * h:dh * (h + 1), :], pr.astype(BF16)) for h, pr in zip(heads, prs)]
            accs = [a * c[2] + pv for a, c, pv in zip(alphas, carry, pvs)]
            return tuple(zip(ms, ls, accs))

        init = tuple((jnp.full((1, tq), -jnp.inf, F32), jnp.zeros((1, tq), F32), jnp.zeros((dh, tq), F32))
                     for _ in heads)
        carry = lax.fori_loop(0, n_full, functools.partial(step, diag=None), init)
        for d in range(n_diag):
            carry = step(n_full + d, carry, d)
        o_ref[:, group] = jnp.concatenate([(acc / l).T for _, l, acc in carry], axis=1).astype(o_ref.dtype)


def _fox(fq, fk, fv, c, B, S):
    T = fq.shape[0]
    tq, tk = TQ_FOX, TK_FOX
    nq = S // tq
    once = pl.Buffered(1)
    maps = _fox_bias_maps()
    return pl.pallas_call(
        functools.partial(_fox_kernel, tq=tq, tk=tk), grid=(B, nq),
        in_specs=[pl.BlockSpec((tq, FOX_WIDTH), lambda b, i: (b * nq + i, 0)),
                  pl.BlockSpec((S, FOX_WIDTH), lambda b, i: (b, 0), pipeline_mode=once),
                  pl.BlockSpec((S, FOX_WIDTH), lambda b, i: (b, 0), pipeline_mode=once),
                  pl.BlockSpec((tq, LANES), lambda b, i: (b * nq + i, 0)),
                  pl.BlockSpec((S, LANES), lambda b, i: (b, 0), pipeline_mode=once),
                  pl.BlockSpec(maps.shape, lambda b, i: (0, 0, 0, 0), pipeline_mode=once)],
        out_specs=pl.BlockSpec((tq, FOX_WIDTH), lambda b, i: (b * nq + i, 0)),
        out_shape=jax.ShapeDtypeStruct((T, FOX_WIDTH), BF16),
        scratch_shapes=[pltpu.VMEM((S, FOX_HEADS * LANES), BF16), pltpu.VMEM((S // tk, FOX_WIDTH, tk), BF16)],
        compiler_params=_cparams("parallel", "arbitrary", vmem=BIG_VMEM_LIMIT), name="fox")(
            fq, fk, fv, c, c, maps)


def _gla_kernel(q_ref, k_ref, v_ref, r_ref, sm_ref, wg_ref, bg_ref, gn_ref, tri_ref, y_ref,
                st_ref, b_ref, o_ref, *, ts):
    dk, dv, C = GLA_KEY_DIM, GLA_VALUE_DIM, CHUNK

    @pl.when(pl.program_id(1) == 0)
    def _():
        st_ref[...] = jnp.zeros_like(st_ref)

    z = _dot(sm_ref[...].astype(BF16), wg_ref[...]) + bg_ref[...]
    la = _log_sigmoid(z) * (1.0 / GLA_GATE_TEMP)
    hi = la.astype(BF16)
    r1 = la - hi.astype(F32)
    mid = r1.astype(BF16)
    lo = (r1 - mid.astype(F32)).astype(BF16)
    tri = tri_ref[...]
    b_ref[...] = _dot(tri, hi) + _dot(tri, mid) + _dot(tri, lo)

    row = lax.broadcasted_iota(jnp.int32, (C, C), 0)
    col = lax.broadcasted_iota(jnp.int32, (C, C), 1)
    trow = lax.broadcasted_iota(jnp.int32, (C, LANES), 0)
    lane = lax.broadcasted_iota(jnp.int32, (C, LANES), 1)
    head_lanes = [lane < dk, lane >= dk]

    def level_reference(b, m):
        if m >= 4:
            n = 2 * m
            return jnp.concatenate(
                [jnp.broadcast_to(b[s + m - 1:s + m, :], (n, LANES)) for s in range(0, C, n)], axis=0)
        up1 = pltpu.roll(b, 1, axis=0)
        if m == 1:
            return jnp.where(trow % 2 == 0, b, up1)
        up2 = pltpu.roll(b, 2, axis=0)
        down1 = pltpu.roll(b, C - 1, axis=0)
        tm = trow % 4
        return jnp.where(tm == 0, down1, jnp.where(tm == 1, b, jnp.where(tm == 2, up1, up2)))

    def chunk(r0):
        for p in range(GLA_HEADS // 2):
            kl = slice(p * LANES, (p + 1) * LANES)
            q = q_ref[pl.ds(r0, C), kl].astype(F32)
            k = k_ref[pl.ds(r0, C), kl].astype(F32)
            b = b_ref[pl.ds(r0, C), kl]
            st = st_ref[p]
            stb = st.astype(BF16)
            b_last = b[C - 1:C, :]
            qb, kb = q.astype(BF16), k.astype(BF16)
            q_in = (q * jnp.exp(b)).astype(BF16)
            zero = jnp.zeros_like(qb)

            scores = [jnp.where(row == col, _dot_nt(jnp.where(hl, qb, zero), kb), 0.0) for hl in head_lanes]
            m = C // 2
            while m >= 1:
                ref = level_reference(b, m)
                upper = (trow % (2 * m)) >= m
                qt = jnp.where(upper, q * jnp.exp(jnp.minimum(b - ref, 0.0)), 0.0).astype(BF16)
                kt = jnp.where(upper, 0.0, k * jnp.exp(jnp.minimum(ref - b, 0.0))).astype(BF16)
                same_block = (row // (2 * m)) == (col // (2 * m))
                scores = [sc + jnp.where(same_block, _dot_nt(jnp.where(hl, qt, zero), kt), 0.0)
                          for sc, hl in zip(scores, head_lanes)]
                m //= 2

            kh = (k * jnp.exp(b_last - b)).astype(BF16)
            upd = []
            for n, hl in enumerate(head_lanes):
                h = 2 * p + n
                vl = slice(h * dv, (h + 1) * dv)
                v = v_ref[pl.ds(r0, C), vl]
                o_ref[pl.ds(r0, C), vl] = (_dot_nt(jnp.where(hl, q_in, zero), stb)
                                           + _dot(scores[n].astype(BF16), v))
                upd.append(_dot_tn(v, kh))
            st_ref[p] = st * jnp.exp(b_last) + jnp.where(lax.broadcasted_iota(jnp.int32, st.shape, 1) < dk,
                                                         upd[0], upd[1])

    def chunk_group(g, carry):
        for sub in range(GLA_CHUNKS_PER_STEP):
            chunk(pl.multiple_of((g * GLA_CHUNKS_PER_STEP + sub) * C, C))
        return carry

    lax.fori_loop(0, ts // (C * GLA_CHUNKS_PER_STEP), chunk_group, 0)

    for h in range(GLA_HEADS):
        vl = slice(h * dv, (h + 1) * dv)
        o = o_ref[:, vl]
        ms = jnp.mean(o * o, axis=-1, keepdims=True)
        g = r_ref[:, vl].astype(F32)
        y = o * lax.rsqrt(ms + LN_EPS) * gn_ref[...] * (g * _sigmoid(g))
        y_ref[:, vl] = y.astype(y_ref.dtype)


def _gla(gq, gk, gv, gr, small, wg_pad, bg, gnorm, tri, B, S):
    T = gq.shape[0]
    ts = TS_GLA
    ns = S // ts
    row = lambda n: pl.BlockSpec((ts, n), lambda b, i: (b * ns + i, 0))
    full = lambda a: pl.BlockSpec(a.shape, lambda b, i: (0,) * a.ndim)
    return pl.pallas_call(
        functools.partial(_gla_kernel, ts=ts), grid=(B, ns),
        in_specs=[row(GLA_KEY_WIDTH), row(GLA_KEY_WIDTH), row(GLA_WIDTH), row(GLA_WIDTH), row(LANES),
                  full(wg_pad), full(bg), full(gnorm), full(tri)],
        out_specs=row(GLA_WIDTH),
        out_shape=jax.ShapeDtypeStruct((T, GLA_WIDTH), BF16),
        scratch_shapes=[pltpu.VMEM((GLA_HEADS // 2, GLA_VALUE_DIM, LANES), F32),
                        pltpu.VMEM((ts, GLA_KEY_WIDTH), F32),
                        pltpu.VMEM((ts, GLA_WIDTH), F32)],
        compiler_params=_cparams("parallel", "arbitrary"), name="gla")(
            gq, gk, gv, gr, small, wg_pad, bg, gnorm, tri)


def _oproj_kernel(yf_ref, yg_ref, h_ref, wo_ref, g_ref, b_ref, wr_ref, h1_ref, st_ref, *part_refs):
    mix = _dot(yf_ref[...], wo_ref[0:FOX_WIDTH, :]) + _dot(yg_ref[...], wo_ref[FOX_WIDTH:, :])
    h1 = _layer_norm(DEEPNORM_ALPHA * h_ref[...] + mix, g_ref[...], b_ref[...])
    h1_ref[...] = h1
    _store_row_parts(h1, part_refs)
    st_ref[...] = _sigmoid(_dot_nt(wr_ref[...], h1.astype(BF16)))


def _oproj(y_fox, y_gla, h, w_out, g, b, w_router_t):
    T, D = h.shape
    tm = TM_OPROJ
    row = lambda n: pl.BlockSpec((tm, n), lambda i: (i, 0))
    full = lambda a: pl.BlockSpec(a.shape, lambda i: (0,) * a.ndim, pipeline_mode=pl.Buffered(1))
    return pl.pallas_call(
        _oproj_kernel, grid=(T // tm,),
        in_specs=[row(FOX_WIDTH), row(GLA_WIDTH), row(D), full(w_out), full(g), full(b), full(w_router_t)],
        out_specs=[row(D), pl.BlockSpec((N_EXPERTS, tm), lambda i: (0, i))] + [row(PART_WORDS)] * ROW_PARTS,
        out_shape=[jax.ShapeDtypeStruct((T, D), F32), jax.ShapeDtypeStruct((N_EXPERTS, T), F32)]
        + [jax.ShapeDtypeStruct((T, PART_WORDS), jnp.uint32)] * ROW_PARTS,
        compiler_params=_cparams("parallel"), name="oproj")(y_fox, y_gla, h, w_out, g, b, w_router_t)


def _route_kernel(s_ref, bias_ref, upper_ref, ones_ref, idx_ref, w_ref, rank_ref, cnt_ref, carry_ref):
    E, tm = s_ref.shape

    @pl.when(pl.program_id(0) == 0)
    def _():
        carry_ref[...] = jnp.zeros_like(carry_ref)

    s = s_ref[...]
    biased = s + bias_ref[...]
    neg = -jnp.inf
    erow = lax.broadcasted_iota(jnp.int32, (E, tm), 0).astype(F32)
    grow = lax.broadcasted_iota(jnp.int32, (GROUP_SIZE, tm), 0).astype(F32)

    gs = []
    for g in range(N_GROUPS):
        blk = biased[g * GROUP_SIZE:(g + 1) * GROUP_SIZE, :]
        m1 = jnp.max(blk, axis=0, keepdims=True)
        i1 = jnp.min(jnp.where(blk == m1, grow, float(GROUP_SIZE)), axis=0, keepdims=True)
        m2 = jnp.max(jnp.where(grow == i1, neg, blk), axis=0, keepdims=True)
        gs.append(m1 + m2)
    keep = []
    for g in range(N_GROUPS):
        beaten = jnp.zeros((1, tm), F32)
        for o in range(N_GROUPS):
            if o == g:
                continue
            wins = (gs[o] > gs[g]) | ((gs[o] == gs[g]) & (o < g))
            beaten = beaten + jnp.where(wins, 1.0, 0.0)
        keep.append(jnp.broadcast_to(beaten < float(TOPK_GROUPS), (GROUP_SIZE, tm)))
    cur = jnp.where(jnp.concatenate(keep, axis=0), biased, neg)

    ids, ws, hots = [], [], []
    chosen = jnp.zeros((E, tm), F32)
    for _ in range(TOP_K):
        m = jnp.max(cur, axis=0, keepdims=True)
        ik = jnp.min(jnp.where(cur == m, erow, float(E)), axis=0, keepdims=True)
        hot = erow == ik
        ws.append(jnp.sum(jnp.where(hot, s, 0.0), axis=0, keepdims=True))
        cur = jnp.where(hot, neg, cur)
        chosen = jnp.where(hot, 1.0, chosen)
        ids.append(ik)
        hots.append(hot)
    wsum = ws[0]
    for w in ws[1:]:
        wsum = wsum + w

    chosen_b = chosen.astype(BF16)
    before = carry_ref[...] + _dot(chosen_b, upper_ref[...])
    carry_ref[...] = carry_ref[...] + _dot(chosen_b, ones_ref[...])
    for k in range(TOP_K):
        idx_ref[k:k + 1, :] = ids[k].astype(jnp.int32)
        w_ref[k:k + 1, :] = ws[k] / wsum * ROUTED_SCALE
        rank_ref[k:k + 1, :] = jnp.sum(jnp.where(hots[k], before, 0.0), axis=0, keepdims=True).astype(jnp.int32)
    cnt_ref[...] = carry_ref[:, 0:LANES]


def _route(scores_t, bias_col, upper, ones):
    E, T = scores_t.shape
    tm = TM_ROUTE
    full = lambda a: pl.BlockSpec(a.shape, lambda i: (0,) * a.ndim)
    kt = pl.BlockSpec((TOP_K, tm), lambda i: (0, i))
    return pl.pallas_call(
        _route_kernel, grid=(T // tm,),
        in_specs=[pl.BlockSpec((E, tm), lambda i: (0, i)), full(bias_col), full(upper), full(ones)],
        out_specs=[kt, kt, kt, pl.BlockSpec((E, LANES), lambda i: (0, 0))],
        out_shape=[jax.ShapeDtypeStruct((TOP_K, T), jnp.int32), jax.ShapeDtypeStruct((TOP_K, T), F32),
                   jax.ShapeDtypeStruct((TOP_K, T), jnp.int32), jax.ShapeDtypeStruct((E, LANES), F32)],
        scratch_shapes=[pltpu.VMEM((E, tm), F32)],
        compiler_params=_cparams("arbitrary"), name="route")(scores_t, bias_col, upper, ones)


def _expert_kernel(ufirst_ref, ucount_ref, ustart_ref, usize_ref, total_ref, *refs):
    x_hbm = refs[:ROW_PARTS]
    wg_ref, wu_ref, wd_ref = refs[ROW_PARTS:ROW_PARTS + 3]
    y_hbm = refs[ROW_PARTS + 3:2 * ROW_PARTS + 3]
    xbuf, ybuf, in_sem, out_sem = refs[2 * ROW_PARTS + 3:]
    e = pl.program_id(0)
    blk = EXPERT_ROW_BLOCK
    total = total_ref[0]

    def hbm_rows(u, k):
        return pl.ds(pl.multiple_of(ustart_ref[u] * blk, blk), k * blk)

    def in_copy(u, k, p):
        return pltpu.make_async_copy(x_hbm[p].at[hbm_rows(u, k)], xbuf.at[u % 2, p, pl.ds(0, k * blk)],
                                     in_sem.at[u % 2, p])

    def out_copy(u, k, p):
        return pltpu.make_async_copy(ybuf.at[u % 2, p, pl.ds(0, k * blk)], y_hbm[p].at[hbm_rows(u, k)],
                                     out_sem.at[u % 2, p])

    def for_size(u, fn):
        for k in range(1, EXPERT_UNIT_BLOCKS + 1):
            pl.when(usize_ref[u] == k)(functools.partial(fn, k))

    def start_read(u):
        def go(k):
            for p in range(ROW_PARTS):
                in_copy(u, k, p).start()
        for_size(u, go)

    def wait_read(u):
        def go(k):
            for p in range(ROW_PARTS):
                in_copy(u, k, p).wait()
        for_size(u, go)

    def wait_write(u):
        def go(k):
            for p in range(ROW_PARTS):
                out_copy(u, k, p).wait()
        for_size(u, go)

    def compute(u, k):
        slot = u % 2
        rows = pl.ds(0, k * blk)
        gate = up = None
        for c0, xc in _load_row_parts([xbuf[slot, p, rows] for p in range(ROW_PARTS)]):
            xb = xc.astype(BF16)
            wrows = slice(c0, c0 + PART_WORDS)
            gc = _dot(xb, wg_ref[wrows, :].astype(BF16))
            uc = _dot(xb, wu_ref[wrows, :].astype(BF16))
            gate = gc if gate is None else gate + gc
            up = uc if up is None else up + uc
        y = _dot((gate * _sigmoid(gate) * up).astype(BF16), wd_ref[...].astype(BF16))
        _store_row_parts(y, [ybuf.at[slot, p, rows] for p in range(ROW_PARTS)])
        for p in range(ROW_PARTS):
            out_copy(u, k, p).start()

    @pl.when((e == 0) & (total > 0))
    def _():
        start_read(0)

    @pl.when(ucount_ref[e] > 0)
    def _():
        def unit(c, carry):
            u = ufirst_ref[e] + c
            wait_read(u)
            pl.when(u + 1 < total)(lambda: start_read(u + 1))
            pl.when(u >= 2)(lambda: wait_write(u - 2))
            for_size(u, lambda k: compute(u, k))
            return carry

        lax.fori_loop(0, ucount_ref[e], unit, 0)

    @pl.when(e == pl.num_programs(0) - 1)
    def _():
        for back in (2, 1):
            pl.when(total >= back)(lambda back=back: wait_write(total - back))


def _experts(ufirst, ucount, ustart, usize, total, xs_parts, w_gate, w_up, w_down):
    P = xs_parts[0].shape[0]
    D = D_MODEL
    unit_rows = EXPERT_UNIT_BLOCKS * EXPERT_ROW_BLOCK
    hbm = pl.BlockSpec(memory_space=pl.ANY)
    grid_spec = pltpu.PrefetchScalarGridSpec(
        num_scalar_prefetch=5, grid=(N_EXPERTS,),
        in_specs=[hbm] * ROW_PARTS + [
            pl.BlockSpec((None, D, EXPERT_DIM), lambda e, *_: (e, 0, 0)),
            pl.BlockSpec((None, D, EXPERT_DIM), lambda e, *_: (e, 0, 0)),
            pl.BlockSpec((None, EXPERT_DIM, D), lambda e, *_: (e, 0, 0))],
        out_specs=[hbm] * ROW_PARTS,
        scratch_shapes=[pltpu.VMEM((2, ROW_PARTS, unit_rows, PART_WORDS), jnp.uint32),
                        pltpu.VMEM((2, ROW_PARTS, unit_rows, PART_WORDS), jnp.uint32),
                        pltpu.SemaphoreType.DMA((2, ROW_PARTS)), pltpu.SemaphoreType.DMA((2, ROW_PARTS))])
    return pl.pallas_call(
        _expert_kernel, grid_spec=grid_spec,
        out_shape=[jax.ShapeDtypeStruct((P, PART_WORDS), jnp.uint32)] * ROW_PARTS,
        compiler_params=_cparams("arbitrary", vmem=BIG_VMEM_LIMIT), name="experts")(
            ufirst, ucount, ustart, usize, total, *xs_parts, w_gate, w_up, w_down)


def _pos_kernel(idx_ref, rank_ref, start_ref, pos_ref):
    K, tm = idx_ref.shape
    E = start_ref.shape[0]
    erow = lax.broadcasted_iota(jnp.int32, (E, tm), 0)
    start = start_ref[...]
    for k in range(K):
        first = jnp.sum(jnp.where(erow == idx_ref[k:k + 1, :], start, 0.0), axis=0, keepdims=True)
        pos_ref[k:k + 1, :] = first.astype(jnp.int32) + rank_ref[k:k + 1, :]


def _positions(idx_t, rank_t, start_col):
    K, T = idx_t.shape
    tm = TM_ROUTE
    kt = pl.BlockSpec((K, tm), lambda i: (0, i))
    return pl.pallas_call(
        _pos_kernel, grid=(T // tm,),
        in_specs=[kt, kt, pl.BlockSpec(start_col.shape, lambda i: (0, 0))],
        out_specs=kt, out_shape=jax.ShapeDtypeStruct((K, T), jnp.int32),
        compiler_params=_cparams("parallel"), name="positions")(idx_t, rank_t, start_col)


def _sc_mesh():
    return plsc.VectorSubcoreMesh(core_axis_name="core", subcore_axis_name="subcore",
                                  num_cores=SC_CORES, num_subcores=SC_SUBCORES)


def _sc_scatter_rows(parts, pos, n_out):
    T, W = parts[0].shape
    K = pos.shape[0]
    n = len(parts)
    win = SC_ROW_WINDOW

    @functools.partial(pl.kernel, out_type=[jax.ShapeDtypeStruct((n_out, W), parts[0].dtype)] * n, mesh=_sc_mesh(),
                       name="sc_dispatch")
    def k(*refs):
        x_hbms, p_hbm, o_hbms = refs[:n], refs[n], refs[n + 1:]
        for x_hbm, o_hbm in zip(x_hbms, o_hbms):
            def body(x_vmem, p_vmem, o_hbm=o_hbm):
                for j in range(K):
                    pltpu.sync_copy(x_vmem, o_hbm.at[p_vmem.at[j]])

            pltpu.emit_pipeline(
                body, grid=(T // win,),
                in_specs=[pl.BlockSpec((win, W), lambda i: (i, 0)), pl.BlockSpec((K, win), lambda i: (0, i))],
                out_specs=[], core_axis_name=("core", "subcore"),
                dimension_semantics=(pltpu.PARALLEL,))(x_hbm, p_hbm)

    return k(*parts, pos)


def _sc_gather_rows(tables, idx):
    M = idx.shape[0]
    W = tables[0].shape[1]
    n = len(tables)
    win = SC_ROW_WINDOW

    @functools.partial(pl.kernel, out_type=[jax.ShapeDtypeStruct((M, W), tables[0].dtype)] * n, mesh=_sc_mesh(),
                       name="sc_combine")
    def k(*refs):
        t_hbms, i_hbm, o_hbms = refs[:n], refs[n], refs[n + 1:]
        for t_hbm, o_hbm in zip(t_hbms, o_hbms):
            def body(i_vmem, o_vmem, t_hbm=t_hbm):
                pltpu.sync_copy(t_hbm.at[i_vmem.at[0]], o_vmem)

            pltpu.emit_pipeline(
                body, grid=(M // win,),
                in_specs=[pl.BlockSpec((1, win), lambda i: (0, i))],
                out_specs=[pl.BlockSpec((win, W), lambda i: (i, 0))],
                core_axis_name=("core", "subcore"), dimension_semantics=(pltpu.PARALLEL,))(i_hbm, o_hbm)

    return k(*tables, idx.reshape(1, M))


def _final_kernel(h1_ref, wk_ref, wg_ref, wu_ref, wd_ref, g_ref, b_ref, *refs):
    yk_refs, o_ref = refs[:ROW_PARTS], refs[ROW_PARTS]
    h1 = h1_ref[...]
    hb = h1.astype(BF16)
    g = _dot(hb, wg_ref[...])
    u = _dot(hb, wu_ref[...])
    ffn = _dot((g * _sigmoid(g) * u).astype(BF16), wd_ref[...])
    wk = wk_ref[...]
    sums = None
    for k in range(TOP_K):
        cols = _load_row_parts([r[k] for r in yk_refs])
        terms = [c * wk[:, k:k + 1] for _, c in cols]
        sums = terms if sums is None else [s + t for s, t in zip(sums, terms)]
    ffn = ffn + jnp.concatenate(sums, axis=1)
    o_ref[...] = _layer_norm(DEEPNORM_ALPHA * h1 + ffn, g_ref[...], b_ref[...])


def _final(h1, yk_parts, wk, w_sg, w_su, w_sd, g, b):
    T, D = h1.shape
    tm = TM_FINAL
    row = lambda n: pl.BlockSpec((tm, n), lambda i: (i, 0))
    full = lambda a: pl.BlockSpec(a.shape, lambda i: (0,) * a.ndim)
    return pl.pallas_call(
        _final_kernel, grid=(T // tm,),
        in_specs=[row(D), row(TOP_K), full(w_sg), full(w_su), full(w_sd), full(g), full(b)]
        + [pl.BlockSpec((TOP_K, tm, PART_WORDS), lambda i: (0, i, 0))] * ROW_PARTS,
        out_specs=row(D), out_shape=jax.ShapeDtypeStruct((T, D), F32),
        compiler_params=_cparams("parallel"), name="final")(h1, wk, w_sg, w_su, w_sd, g, b, *yk_parts)


def _block_diag_tri(n, c):
    r = np.arange(n)
    return jnp.asarray(((r[:, None] >= r[None, :]) & (r[:, None] // c == r[None, :] // c)).astype(np.float32), BF16)


def kernel(x, ln_in_g, ln_in_b, w_in, b_fgate, w_gate_up, b_gate, g_gla_norm, w_out, ln1_g, ln1_b, w_router,
           router_bias, w_exp_gate, w_exp_up, w_exp_down, w_sh_gate, w_sh_up, w_sh_down, ln2_g, ln2_b):
    B, S, D = x.shape
    T = B * S
    x2 = x.reshape(T, D)
    l = 0
    row = lambda a: a.reshape(1, -1)

    off = np.cumsum((0,) + IN_SPLITS)
    seg = lambda i: w_in[l][:, off[i]:off[i + 1]]
    w_main = jnp.concatenate([seg(0) * (FOX_HEAD_DIM ** -0.5 * LOG2E), seg(1), seg(2), seg(4) * GLA_KEY_DIM ** -0.5, seg(5),
                              seg(6), seg(8)], axis=1).astype(BF16)
    n_small = FOX_HEADS + GLA_GATE_RANK
    w_small = jnp.concatenate([seg(3), seg(7), jnp.zeros((D, LANES - n_small), F32)], axis=1).astype(BF16)
    bf_pad = jnp.concatenate([b_fgate[l], jnp.zeros((LANES - FOX_HEADS,), F32)]).reshape(1, LANES)
    wg_pad = jnp.zeros((LANES, GLA_KEY_WIDTH), F32).at[FOX_HEADS:n_small].set(w_gate_up[l]).astype(BF16)

    h, fq, fk, fv, gq, gk, gv, gr, small = _proj(x2, row(ln_in_g), row(ln_in_b), w_main, w_small)

    y_fox = _fox(fq, fk, fv, _fgate(small, bf_pad, B, S), B, S)

    y_gla = _gla(gq, gk, gv, gr, small, wg_pad, row(b_gate[l]), row(g_gla_norm[l]),
                 _block_diag_tri(TS_GLA, CHUNK), B, S)

    h1, scores_t, *h1_parts = _oproj(y_fox, y_gla, h, w_out[l].astype(BF16), row(ln1_g[l]), row(ln1_b[l]),
                                     w_router[l].T.astype(BF16))

    tm = TM_ROUTE
    r = np.arange(tm)
    upper = jnp.asarray((r[:, None] < r[None, :]).astype(np.float32), BF16)
    ones = jnp.ones((tm, tm), BF16)
    idx_t, w_t, rank_t, cnt = _route(scores_t, router_bias[l].reshape(N_EXPERTS, 1), upper, ones)

    blk, ub, E = EXPERT_ROW_BLOCK, EXPERT_UNIT_BLOCKS, N_EXPERTS
    A = T * TOP_K
    P = A + E * blk
    counts = cnt[:, 0].astype(jnp.int32)
    nblk = (counts + blk - 1) // blk
    blk_end = jnp.cumsum(nblk)
    blk_start = blk_end - nblk
    pos = _positions(idx_t, rank_t, (blk_start * blk).astype(F32).reshape(E, 1))
    nunit = (nblk + ub - 1) // ub
    unit_end = jnp.cumsum(nunit)
    ufirst = unit_end - nunit
    u = jnp.arange(E + (A // blk + E) // ub + 1, dtype=jnp.int32)
    owner = jnp.minimum(jnp.sum(unit_end[None, :] <= u[:, None], axis=1), E - 1)
    onehot = owner[:, None] == jnp.arange(E)[None, :]
    pick = lambda v: jnp.sum(jnp.where(onehot, v[None, :], 0), axis=1)
    j = u - pick(ufirst)
    n_units = jnp.maximum(pick(nunit), 1)
    base, rem = pick(nblk) // n_units, pick(nblk) % n_units
    live = u < unit_end[-1]
    usize = jnp.where(live, base + (j < rem), 1).astype(jnp.int32)
    ustart = jnp.where(live, pick(blk_start) + j * base + jnp.minimum(j, rem), 0).astype(jnp.int32)

    xs_parts = _sc_scatter_rows(h1_parts, pos, P)
    ys_parts = _experts(ufirst, nunit, ustart, usize, unit_end[-1:], xs_parts,
                        w_exp_gate[l], w_exp_up[l], w_exp_down[l])
    yk_parts = [y.reshape(TOP_K, T, PART_WORDS) for y in _sc_gather_rows(ys_parts, pos.reshape(-1))]

    out = _final(h1, yk_parts, w_t.T, w_sh_gate[l].astype(BF16), w_sh_up[l].astype(BF16),
                 w_sh_down[l].astype(BF16), row(ln2_g[l]), row(ln2_b[l]))
    return out.reshape(B, S, D)
```

```python
import functools

import jax
import jax.numpy as jnp
import numpy as np
from jax import lax
from jax.experimental import pallas as pl
from jax.experimental.pallas import tpu as pltpu
from jax.experimental.pallas import tpu_sc as plsc

F32 = jnp.float32
BF16 = jnp.bfloat16

D_MODEL = 1024
CHUNK = 64
LN_EPS = 1e-5
LOG2E = 1.4426950408889634
FOX_HEADS = 8
FOX_HEAD_DIM = 64
FOX_WIDTH = FOX_HEADS * FOX_HEAD_DIM
GLA_HEADS = 4
GLA_KEY_DIM = 64
GLA_VALUE_DIM = 128
GLA_KEY_WIDTH = GLA_HEADS * GLA_KEY_DIM
GLA_WIDTH = GLA_HEADS * GLA_VALUE_DIM
GLA_GATE_RANK = 16
GLA_GATE_TEMP = 16.0
N_EXPERTS = 256
N_GROUPS = 8
GROUP_SIZE = N_EXPERTS // N_GROUPS
TOPK_GROUPS = 4
TOP_K = 8
EXPERT_DIM = 256
SHARED_DIM = 256
ROUTED_SCALE = 2.5
DEPTH = 1
DEEPNORM_ALPHA = (2.0 * DEPTH) ** 0.25
IN_SPLITS = (FOX_WIDTH, FOX_WIDTH, FOX_WIDTH, FOX_HEADS, GLA_KEY_WIDTH, GLA_KEY_WIDTH, GLA_WIDTH,
             GLA_GATE_RANK, GLA_WIDTH)

LANES = 128
VMEM_LIMIT = 48 * 1024 * 1024
BIG_VMEM_LIMIT = 56 * 1024 * 1024

TM_PROJ = 1024
TQ_FOX = 512
TK_FOX = 512
FOX_GROUP = 8
TS_GLA = 512
GLA_CHUNKS_PER_STEP = 4
TM_OPROJ = 1024
TM_ROUTE = 512
EXPERT_ROW_BLOCK = 128
EXPERT_UNIT_BLOCKS = 12
TM_FINAL = 512

SC_CORES = 2
SC_SUBCORES = 16
SC_ROW_WINDOW = 128
ROW_PARTS = 2
PART_WORDS = D_MODEL // 2 // ROW_PARTS


def _cparams(*sem, vmem=VMEM_LIMIT):
    return pltpu.CompilerParams(dimension_semantics=sem, vmem_limit_bytes=vmem)


def _layer_norm(x, g, b):
    mu = jnp.mean(x, axis=-1, keepdims=True)
    xc = x - mu
    var = jnp.mean(xc * xc, axis=-1, keepdims=True)
    return xc * lax.rsqrt(var + LN_EPS) * g + b


def _log_sigmoid(z):
    return jnp.minimum(z, 0.0) - jnp.log1p(jnp.exp(-jnp.abs(z)))


def _sigmoid(z):
    return 1.0 / (1.0 + jnp.exp(-z))


def _dot(a, b):
    return jnp.dot(a, b, preferred_element_type=F32)


def _dot_nt(a, b):
    return lax.dot_general(a, b, (((1,), (1,)), ((), ())), preferred_element_type=F32)


def _dot_tn(a, b):
    return lax.dot_general(a, b, (((0,), (0,)), ((), ())), preferred_element_type=F32)


def _pack_bf16_pairs(x):
    n = x.shape[1] // 2
    u = lax.bitcast_convert_type(x.astype(BF16).astype(F32), jnp.uint32)
    return (u[:, :n] >> 16) | u[:, n:]


def _unpack_bf16_pairs(w):
    lo = lax.bitcast_convert_type(w << 16, F32)
    hi = lax.bitcast_convert_type(w & jnp.uint32(0xFFFF0000), F32)
    return lo, hi


def _store_row_parts(x, part_refs):
    packed = _pack_bf16_pairs(x)
    for p, ref in enumerate(part_refs):
        ref[...] = packed[:, p * PART_WORDS:(p + 1) * PART_WORDS]


def _load_row_parts(parts):
    out = []
    for p, w in enumerate(parts):
        lo, hi = _unpack_bf16_pairs(w)
        out.append((p * PART_WORDS, lo))
        out.append((D_MODEL // 2 + p * PART_WORDS, hi))
    return sorted(out, key=lambda t: t[0])


def _proj_kernel(x_ref, g_ref, b_ref, wm_ref, ws_ref,
                 h_ref, fq_ref, fk_ref, fv_ref, gq_ref, gk_ref, gv_ref, gr_ref, sm_ref):
    h = _layer_norm(x_ref[...], g_ref[...], b_ref[...])
    h_ref[...] = h
    hb = h.astype(BF16)
    off = 0
    for ref in (fq_ref, fk_ref, fv_ref, gq_ref, gk_ref, gv_ref, gr_ref):
        n = ref.shape[1]
        ref[...] = _dot(hb, wm_ref[:, off:off + n]).astype(ref.dtype)
        off += n
    sm_ref[...] = _dot(hb, ws_ref[...])


def _proj(x2, g, b, w_main, w_small):
    T, D = x2.shape
    tm = TM_PROJ
    widths = (FOX_WIDTH, FOX_WIDTH, FOX_WIDTH, GLA_KEY_WIDTH, GLA_KEY_WIDTH, GLA_WIDTH, GLA_WIDTH)
    row = lambda n: pl.BlockSpec((tm, n), lambda i: (i, 0))
    full = lambda a: pl.BlockSpec(a.shape, lambda i: (0,) * a.ndim, pipeline_mode=pl.Buffered(1))
    out_shape = [jax.ShapeDtypeStruct((T, D), F32)]
    out_shape += [jax.ShapeDtypeStruct((T, n), BF16) for n in widths]
    out_shape += [jax.ShapeDtypeStruct((T, LANES), F32)]
    out_specs = [row(D)] + [row(n) for n in widths] + [row(LANES)]
    return pl.pallas_call(
        _proj_kernel, grid=(T // tm,),
        in_specs=[row(D), full(g), full(b), full(w_main), full(w_small)],
        out_specs=out_specs, out_shape=out_shape,
        compiler_params=_cparams("parallel", vmem=BIG_VMEM_LIMIT), name="proj")(x2, g, b, w_main, w_small)


def _fgate_kernel(sm_ref, bf_ref, c_ref):
    S = sm_ref.shape[0]
    lf = _log_sigmoid(sm_ref[...] + bf_ref[...])
    rows = lax.broadcasted_iota(jnp.int32, lf.shape, 0)
    s = 1
    while s < S:
        lf = lf + jnp.where(rows >= s, pltpu.roll(lf, s, axis=0), 0.0)
        s *= 2
    c_ref[...] = lf * LOG2E


def _fgate(small, bf_pad, B, S):
    T = small.shape[0]
    return pl.pallas_call(
        _fgate_kernel, grid=(B,),
        in_specs=[pl.BlockSpec((S, LANES), lambda b: (b, 0)),
                  pl.BlockSpec((1, LANES), lambda b: (0, 0))],
        out_specs=pl.BlockSpec((S, LANES), lambda b: (b, 0)),
        out_shape=jax.ShapeDtypeStruct((T, LANES), F32),
        compiler_params=_cparams("parallel"), name="fgate")(small, bf_pad)


def _split3(x):
    hi = x.astype(BF16).astype(F32)
    r = x - hi
    mid = r.astype(BF16).astype(F32)
    return hi, mid, (r - mid).astype(BF16).astype(F32)


def _fox_bias_maps():
    dh = FOX_HEAD_DIM
    maps = np.zeros((2, FOX_HEADS, 3 * LANES, LANES), np.float32)
    for is_key in (0, 1):
        for h in range(FOX_HEADS):
            for piece in range(3):
                maps[is_key, h, piece * LANES + h, dh + 3 * is_key + piece] = 1.0 - 2.0 * is_key
    return jnp.asarray(maps, BF16)


def _gate_pieces(c_tile):
    return jnp.concatenate([t.astype(BF16) for t in _split3(c_tile)], axis=1)


def _fox_operand(pair_ref_tile, odd, pieces, lane_map, is_key):
    dh = FOX_HEAD_DIM
    x = pair_ref_tile.astype(F32)
    if odd:
        x = pltpu.roll(x, dh, axis=1)
    lane = lax.broadcasted_iota(jnp.int32, x.shape, 1)
    ones_lo = dh if is_key else dh + 3
    bias = jnp.where((lane >= ones_lo) & (lane < ones_lo + 3), 1.0, _dot(pieces, lane_map))
    return jnp.where(lane < dh, x, bias).astype(BF16)


def _fox_kernel(q_ref, k_ref, v_ref, cq_ref, ck_ref, map_ref, o_ref, ka_ref, vt_ref, *, tq, tk):
    i = pl.program_id(1)
    S = k_ref.shape[0]
    dh = FOX_HEAD_DIM

    @pl.when(i == 0)
    def _():
        def fill(r, carry):
            r0 = pl.multiple_of(r * tk, tk)
            pieces = _gate_pieces(ck_ref[pl.ds(r0, tk), :])
            for h in range(FOX_HEADS):
                pair = slice(LANES * (h // 2), LANES * (h // 2 + 1))
                ka_ref[pl.ds(r0, tk), LANES * h:LANES * (h + 1)] = _fox_operand(
                    k_ref[pl.ds(r0, tk), pair], h % 2, pieces, map_ref[1, h], True)
            vt_ref[r] = v_ref[pl.ds(r0, tk), :].astype(F32).T.astype(BF16)
            return carry
        lax.fori_loop(0, S // tk, fill, 0)

    q_pieces = _gate_pieces(cq_ref[...])

    key = lax.broadcasted_iota(jnp.int32, (tk, tq), 0)
    qry = lax.broadcasted_iota(jnp.int32, (tk, tq), 1)
    n_diag = tq // tk
    n_full = i * n_diag

    for p in range(FOX_HEADS // FOX_GROUP):
        group = slice(dh * FOX_GROUP * p, dh * FOX_GROUP * (p + 1))
        heads = tuple(range(FOX_GROUP * p, FOX_GROUP * (p + 1)))
        qa = [_fox_operand(q_ref[:, LANES * (h // 2):LANES * (h // 2 + 1)], h % 2, q_pieces, map_ref[0, h], False)
              for h in heads]

        def logits(j):
            r0 = pl.multiple_of(j * tk, tk)
            return [_dot_nt(ka_ref[pl.ds(r0, tk), LANES * h:LANES * (h + 1)], qa[n]) for n, h in enumerate(heads)]

        def step(j, carry, diag):
            ss = logits(j)
            if diag is not None:
                ss = [jnp.where(qry >= key + diag * tk, s, -jnp.inf) for s in ss]
            ms = [jnp.maximum(c[0], jnp.max(s, axis=0, keepdims=True)) for c, s in zip(carry, ss)]
            prs = [jnp.exp2(s - m) for s, m in zip(ss, ms)]
            alphas = [jnp.exp2(c[0] - m) for c, m in zip(carry, ms)]
            ls = [a * c[1] + jnp.sum(pr, axis=0, keepdims=True) for a, c, pr in zip(alphas, carry, prs)]
            pvs = [_dot(vt_ref[j, dh * h:dh * (h + 1), :], pr.astype(BF16)) for h, pr in zip(heads, prs)]
            accs = [a * c[2] + pv for a, c, pv in zip(alphas, carry, pvs)]
            return tuple(zip(ms, ls, accs))

        init = tuple((jnp.full((1, tq), -jnp.inf, F32), jnp.zeros((1, tq), F32), jnp.zeros((dh, tq), F32))
                     for _ in heads)
        carry = lax.fori_loop(0, n_full, functools.partial(step, diag=None), init)
        for d in range(n_diag):
            carry = step(n_full + d, carry, d)
        o_ref[:, group] = jnp.concatenate([(acc / l).T for _, l, acc in carry], axis=1).astype(o_ref.dtype)


def _fox(fq, fk, fv, c, B, S):
    T = fq.shape[0]
    tq, tk = TQ_FOX, TK_FOX
    nq = S // tq
    once = pl.Buffered(1)
    maps = _fox_bias_maps()
    return pl.pallas_call(
        functools.partial(_fox_kernel, tq=tq, tk=tk), grid=(B, nq),
        in_specs=[pl.BlockSpec((tq, FOX_WIDTH), lambda b, i: (b * nq + i, 0)),
                  pl.BlockSpec((S, FOX_WIDTH), lambda b, i: (b, 0), pipeline_mode=once),
                  pl.BlockSpec((S, FOX_WIDTH), lambda b, i: (b, 0), pipeline_mode=once),
                  pl.BlockSpec((tq, LANES), lambda b, i: (b * nq + i, 0)),
                  pl.BlockSpec((S, LANES), lambda b, i: (b, 0), pipeline_mode=once),
                  pl.BlockSpec(maps.shape, lambda b, i: (0, 0, 0, 0), pipeline_mode=once)],
        out_specs=pl.BlockSpec((tq, FOX_WIDTH), lambda b, i: (b * nq + i, 0)),
        out_shape=jax.ShapeDtypeStruct((T, FOX_WIDTH), BF16),
        scratch_shapes=[pltpu.VMEM((S, FOX_HEADS * LANES), BF16), pltpu.VMEM((S // tk, FOX_WIDTH, tk), BF16)],
        compiler_params=_cparams("parallel", "arbitrary", vmem=BIG_VMEM_LIMIT), name="fox")(
            fq, fk, fv, c, c, maps)


def _gla_kernel(q_ref, k_ref, v_ref, r_ref, sm_ref, wg_ref, bg_ref, gn_ref, tri_ref, y_ref,
                st_ref, b_ref, o_ref, *, ts):
    dk, dv, C = GLA_KEY_DIM, GLA_VALUE_DIM, CHUNK

    @pl.when(pl.program_id(1) == 0)
    def _():
        st_ref[...] = jnp.zeros_like(st_ref)

    z = _dot(sm_ref[...].astype(BF16), wg_ref[...]) + bg_ref[...]
    la = _log_sigmoid(z) * (1.0 / GLA_GATE_TEMP)
    hi = la.astype(BF16)
    r1 = la - hi.astype(F32)
    mid = r1.astype(BF16)
    lo = (r1 - mid.astype(F32)).astype(BF16)
    tri = tri_ref[...]
    b_ref[...] = _dot(tri, hi) + _dot(tri, mid) + _dot(tri, lo)

    row = lax.broadcasted_iota(jnp.int32, (C, C), 0)
    col = lax.broadcasted_iota(jnp.int32, (C, C), 1)
    trow = lax.broadcasted_iota(jnp.int32, (C, LANES), 0)
    lane = lax.broadcasted_iota(jnp.int32, (C, LANES), 1)
    head_lanes = [lane < dk, lane >= dk]

    def level_reference(b, m):
        if m >= 4:
            n = 2 * m
            return jnp.concatenate(
                [jnp.broadcast_to(b[s + m - 1:s + m, :], (n, LANES)) for s in range(0, C, n)], axis=0)
        up1 = pltpu.roll(b, 1, axis=0)
        if m == 1:
            return jnp.where(trow % 2 == 0, b, up1)
        up2 = pltpu.roll(b, 2, axis=0)
        down1 = pltpu.roll(b, C - 1, axis=0)
        tm = trow % 4
        return jnp.where(tm == 0, down1, jnp.where(tm == 1, b, jnp.where(tm == 2, up1, up2)))

    def chunk(r0):
        for p in range(GLA_HEADS // 2):
            kl = slice(p * LANES, (p + 1) * LANES)
            q = q_ref[pl.ds(r0, C), kl].astype(F32)
            k = k_ref[pl.ds(r0, C), kl].astype(F32)
            b = b_ref[pl.ds(r0, C), kl]
            st = st_ref[p]
            stb = st.astype(BF16)
            b_last = b[C - 1:C, :]
            qb, kb = q.astype(BF16), k.astype(BF16)
            q_in = (q * jnp.exp(b)).astype(BF16)
            zero = jnp.zeros_like(qb)

            scores = [jnp.where(row == col, _dot_nt(jnp.where(hl, qb, zero), kb), 0.0) for hl in head_lanes]
            m = C // 2
            while m >= 1:
                ref = level_reference(b, m)
                upper = (trow % (2 * m)) >= m
                qt = jnp.where(upper, q * jnp.exp(jnp.minimum(b - ref, 0.0)), 0.0).astype(BF16)
                kt = jnp.where(upper, 0.0, k * jnp.exp(jnp.minimum(ref - b, 0.0))).astype(BF16)
                same_block = (row // (2 * m)) == (col // (2 * m))
                scores = [sc + jnp.where(same_block, _dot_nt(jnp.where(hl, qt, zero), kt), 0.0)
                          for sc, hl in zip(scores, head_lanes)]
                m //= 2

            kh = (k * jnp.exp(b_last - b)).astype(BF16)
            upd = []
            for n, hl in enumerate(head_lanes):
                h = 2 * p + n
                vl = slice(h * dv, (h + 1) * dv)
                v = v_ref[pl.ds(r0, C), vl]
                o_ref[pl.ds(r0, C), vl] = (_dot_nt(jnp.where(hl, q_in, zero), stb)
                                           + _dot(scores[n].astype(BF16), v))
                upd.append(_dot_tn(v, kh))
            st_ref[p] = st * jnp.exp(b_last) + jnp.where(lax.broadcasted_iota(jnp.int32, st.shape, 1) < dk,
                                                         upd[0], upd[1])

    def chunk_group(g, carry):
        for sub in range(GLA_CHUNKS_PER_STEP):
            chunk(pl.multiple_of((g * GLA_CHUNKS_PER_STEP + sub) * C, C))
        return carry

    lax.fori_loop(0, ts // (C * GLA_CHUNKS_PER_STEP), chunk_group, 0)

    for h in range(GLA_HEADS):
        vl = slice(h * dv, (h + 1) * dv)
        o = o_ref[:, vl]
        ms = jnp.mean(o * o, axis=-1, keepdims=True)
        g = r_ref[:, vl].astype(F32)
        y = o * lax.rsqrt(ms + LN_EPS) * gn_ref[...] * (g * _sigmoid(g))
        y_ref[:, vl] = y.astype(y_ref.dtype)


def _gla(gq, gk, gv, gr, small, wg_pad, bg, gnorm, tri, B, S):
    T = gq.shape[0]
    ts = TS_GLA
    ns = S // ts
    row = lambda n: pl.BlockSpec((ts, n), lambda b, i: (b * ns + i, 0))
    full = lambda a: pl.BlockSpec(a.shape, lambda b, i: (0,) * a.ndim)
    return pl.pallas_call(
        functools.partial(_gla_kernel, ts=ts), grid=(B, ns),
        in_specs=[row(GLA_KEY_WIDTH), row(GLA_KEY_WIDTH), row(GLA_WIDTH), row(GLA_WIDTH), row(LANES),
                  full(wg_pad), full(bg), full(gnorm), full(tri)],
        out_specs=row(GLA_WIDTH),
        out_shape=jax.ShapeDtypeStruct((T, GLA_WIDTH), BF16),
        scratch_shapes=[pltpu.VMEM((GLA_HEADS // 2, GLA_VALUE_DIM, LANES), F32),
                        pltpu.VMEM((ts, GLA_KEY_WIDTH), F32),
                        pltpu.VMEM((ts, GLA_WIDTH), F32)],
        compiler_params=_cparams("parallel", "arbitrary"), name="gla")(
            gq, gk, gv, gr, small, wg_pad, bg, gnorm, tri)


def _oproj_kernel(yf_ref, yg_ref, h_ref, wo_ref, g_ref, b_ref, wr_ref, h1_ref, st_ref, *part_refs):
    mix = _dot(yf_ref[...], wo_ref[0:FOX_WIDTH, :]) + _dot(yg_ref[...], wo_ref[FOX_WIDTH:, :])
    h1 = _layer_norm(DEEPNORM_ALPHA * h_ref[...] + mix, g_ref[...], b_ref[...])
    h1_ref[...] = h1
    _store_row_parts(h1, part_refs)
    st_ref[...] = _sigmoid(_dot_nt(wr_ref[...], h1.astype(BF16)))


def _oproj(y_fox, y_gla, h, w_out, g, b, w_router_t):
    T, D = h.shape
    tm = TM_OPROJ
    row = lambda n: pl.BlockSpec((tm, n), lambda i: (i, 0))
    full = lambda a: pl.BlockSpec(a.shape, lambda i: (0,) * a.ndim, pipeline_mode=pl.Buffered(1))
    return pl.pallas_call(
        _oproj_kernel, grid=(T // tm,),
        in_specs=[row(FOX_WIDTH), row(GLA_WIDTH), row(D), full(w_out), full(g), full(b), full(w_router_t)],
        out_specs=[row(D), pl.BlockSpec((N_EXPERTS, tm), lambda i: (0, i))] + [row(PART_WORDS)] * ROW_PARTS,
        out_shape=[jax.ShapeDtypeStruct((T, D), F32), jax.ShapeDtypeStruct((N_EXPERTS, T), F32)]
        + [jax.ShapeDtypeStruct((T, PART_WORDS), jnp.uint32)] * ROW_PARTS,
        compiler_params=_cparams("parallel"), name="oproj")(y_fox, y_gla, h, w_out, g, b, w_router_t)


def _route_kernel(s_ref, bias_ref, upper_ref, ones_ref, idx_ref, w_ref, rank_ref, cnt_ref, carry_ref):
    E, tm = s_ref.shape

    @pl.when(pl.program_id(0) == 0)
    def _():
        carry_ref[...] = jnp.zeros_like(carry_ref)

    s = s_ref[...]
    biased = s + bias_ref[...]
    neg = -jnp.inf
    erow = lax.broadcasted_iota(jnp.int32, (E, tm), 0).astype(F32)
    grow = lax.broadcasted_iota(jnp.int32, (GROUP_SIZE, tm), 0).astype(F32)

    gs = []
    for g in range(N_GROUPS):
        blk = biased[g * GROUP_SIZE:(g + 1) * GROUP_SIZE, :]
        m1 = jnp.max(blk, axis=0, keepdims=True)
        i1 = jnp.min(jnp.where(blk == m1, grow, float(GROUP_SIZE)), axis=0, keepdims=True)
        m2 = jnp.max(jnp.where(grow == i1, neg, blk), axis=0, keepdims=True)
        gs.append(m1 + m2)
    keep = []
    for g in range(N_GROUPS):
        beaten = jnp.zeros((1, tm), F32)
        for o in range(N_GROUPS):
            if o == g:
                continue
            wins = (gs[o] > gs[g]) | ((gs[o] == gs[g]) & (o < g))
            beaten = beaten + jnp.where(wins, 1.0, 0.0)
        keep.append(jnp.broadcast_to(beaten < float(TOPK_GROUPS), (GROUP_SIZE, tm)))
    cur = jnp.where(jnp.concatenate(keep, axis=0), biased, neg)

    ids, ws, hots = [], [], []
    chosen = jnp.zeros((E, tm), F32)
    for _ in range(TOP_K):
        m = jnp.max(cur, axis=0, keepdims=True)
        ik = jnp.min(jnp.where(cur == m, erow, float(E)), axis=0, keepdims=True)
        hot = erow == ik
        ws.append(jnp.sum(jnp.where(hot, s, 0.0), axis=0, keepdims=True))
        cur = jnp.where(hot, neg, cur)
        chosen = jnp.where(hot, 1.0, chosen)
        ids.append(ik)
        hots.append(hot)
    wsum = ws[0]
    for w in ws[1:]:
        wsum = wsum + w

    chosen_b = chosen.astype(BF16)
    before = carry_ref[...] + _dot(chosen_b, upper_ref[...])
    carry_ref[...] = carry_ref[...] + _dot(chosen_b, ones_ref[...])
    for k in range(TOP_K):
        idx_ref[k:k + 1, :] = ids[k].astype(jnp.int32)
        w_ref[k:k + 1, :] = ws[k] / wsum * ROUTED_SCALE
        rank_ref[k:k + 1, :] = jnp.sum(jnp.where(hots[k], before, 0.0), axis=0, keepdims=True).astype(jnp.int32)
    cnt_ref[...] = carry_ref[:, 0:LANES]


def _route(scores_t, bias_col, upper, ones):
    E, T = scores_t.shape
    tm = TM_ROUTE
    full = lambda a: pl.BlockSpec(a.shape, lambda i: (0,) * a.ndim)
    kt = pl.BlockSpec((TOP_K, tm), lambda i: (0, i))
    return pl.pallas_call(
        _route_kernel, grid=(T // tm,),
        in_specs=[pl.BlockSpec((E, tm), lambda i: (0, i)), full(bias_col), full(upper), full(ones)],
        out_specs=[kt, kt, kt, pl.BlockSpec((E, LANES), lambda i: (0, 0))],
        out_shape=[jax.ShapeDtypeStruct((TOP_K, T), jnp.int32), jax.ShapeDtypeStruct((TOP_K, T), F32),
                   jax.ShapeDtypeStruct((TOP_K, T), jnp.int32), jax.ShapeDtypeStruct((E, LANES), F32)],
        scratch_shapes=[pltpu.VMEM((E, tm), F32)],
        compiler_params=_cparams("arbitrary"), name="route")(scores_t, bias_col, upper, ones)


def _expert_kernel(ufirst_ref, ucount_ref, ustart_ref, usize_ref, total_ref, *refs):
    x_hbm = refs[:ROW_PARTS]
    wg_ref, wu_ref, wd_ref = refs[ROW_PARTS:ROW_PARTS + 3]
    y_hbm = refs[ROW_PARTS + 3:2 * ROW_PARTS + 3]
    xbuf, ybuf, wgb, wub, wdb, in_sem, out_sem = refs[2 * ROW_PARTS + 3:]
    e = pl.program_id(0)
    blk = EXPERT_ROW_BLOCK
    total = total_ref[0]

    def hbm_rows(u, k):
        return pl.ds(pl.multiple_of(ustart_ref[u] * blk, blk), k * blk)

    def in_copy(u, k, p):
        return pltpu.make_async_copy(x_hbm[p].at[hbm_rows(u, k)], xbuf.at[u % 2, p, pl.ds(0, k * blk)],
                                     in_sem.at[u % 2, p])

    def out_copy(u, k, p):
        return pltpu.make_async_copy(ybuf.at[u % 2, p, pl.ds(0, k * blk)], y_hbm[p].at[hbm_rows(u, k)],
                                     out_sem.at[u % 2, p])

    def for_size(u, fn):
        for k in range(1, EXPERT_UNIT_BLOCKS + 1):
            pl.when(usize_ref[u] == k)(functools.partial(fn, k))

    def start_read(u):
        def go(k):
            for p in range(ROW_PARTS):
                in_copy(u, k, p).start()
        for_size(u, go)

    def wait_read(u):
        def go(k):
            for p in range(ROW_PARTS):
                in_copy(u, k, p).wait()
        for_size(u, go)

    def wait_write(u):
        def go(k):
            for p in range(ROW_PARTS):
                out_copy(u, k, p).wait()
        for_size(u, go)

    def compute(u, k):
        slot = u % 2
        rows = pl.ds(0, k * blk)
        gate = up = None
        for c0, xc in _load_row_parts([xbuf[slot, p, rows] for p in range(ROW_PARTS)]):
            xb = xc.astype(BF16)
            wrows = slice(c0, c0 + PART_WORDS)
            gc = _dot(xb, wgb[wrows, :])
            uc = _dot(xb, wub[wrows, :])
            gate = gc if gate is None else gate + gc
            up = uc if up is None else up + uc
        y = _dot((gate * _sigmoid(gate) * up).astype(BF16), wdb[...])
        _store_row_parts(y, [ybuf.at[slot, p, rows] for p in range(ROW_PARTS)])
        for p in range(ROW_PARTS):
            out_copy(u, k, p).start()

    @pl.when((e == 0) & (total > 0))
    def _():
        start_read(0)

    @pl.when(ucount_ref[e] > 0)
    def _():
        wgb[...] = wg_ref[...].astype(BF16)
        wub[...] = wu_ref[...].astype(BF16)
        wdb[...] = wd_ref[...].astype(BF16)

        def unit(c, carry):
            u = ufirst_ref[e] + c
            wait_read(u)
            pl.when(u + 1 < total)(lambda: start_read(u + 1))
            pl.when(u >= 2)(lambda: wait_write(u - 2))
            for_size(u, lambda k: compute(u, k))
            return carry

        lax.fori_loop(0, ucount_ref[e], unit, 0)

    @pl.when(e == pl.num_programs(0) - 1)
    def _():
        for back in (2, 1):
            pl.when(total >= back)(lambda back=back: wait_write(total - back))


def _experts(ufirst, ucount, ustart, usize, total, xs_parts, w_gate, w_up, w_down):
    P = xs_parts[0].shape[0]
    D = D_MODEL
    unit_rows = EXPERT_UNIT_BLOCKS * EXPERT_ROW_BLOCK
    hbm = pl.BlockSpec(memory_space=pl.ANY)
    grid_spec = pltpu.PrefetchScalarGridSpec(
        num_scalar_prefetch=5, grid=(N_EXPERTS,),
        in_specs=[hbm] * ROW_PARTS + [
            pl.BlockSpec((None, D, EXPERT_DIM), lambda e, *_: (e, 0, 0)),
            pl.BlockSpec((None, D, EXPERT_DIM), lambda e, *_: (e, 0, 0)),
            pl.BlockSpec((None, EXPERT_DIM, D), lambda e, *_: (e, 0, 0))],
        out_specs=[hbm] * ROW_PARTS,
        scratch_shapes=[pltpu.VMEM((2, ROW_PARTS, unit_rows, PART_WORDS), jnp.uint32),
                        pltpu.VMEM((2, ROW_PARTS, unit_rows, PART_WORDS), jnp.uint32),
                        pltpu.VMEM((D, EXPERT_DIM), BF16), pltpu.VMEM((D, EXPERT_DIM), BF16),
                        pltpu.VMEM((EXPERT_DIM, D), BF16),
                        pltpu.SemaphoreType.DMA((2, ROW_PARTS)), pltpu.SemaphoreType.DMA((2, ROW_PARTS))])
    return pl.pallas_call(
        _expert_kernel, grid_spec=grid_spec,
        out_shape=[jax.ShapeDtypeStruct((P, PART_WORDS), jnp.uint32)] * ROW_PARTS,
        compiler_params=_cparams("arbitrary", vmem=BIG_VMEM_LIMIT), name="experts")(
            ufirst, ucount, ustart, usize, total, *xs_parts, w_gate, w_up, w_down)


def _pos_kernel(idx_ref, rank_ref, start_ref, pos_ref):
    K, tm = idx_ref.shape
    E = start_ref.shape[0]
    erow = lax.broadcasted_iota(jnp.int32, (E, tm), 0)
    start = start_ref[...]
    for k in range(K):
        first = jnp.sum(jnp.where(erow == idx_ref[k:k + 1, :], start, 0.0), axis=0, keepdims=True)
        pos_ref[k:k + 1, :] = first.astype(jnp.int32) + rank_ref[k:k + 1, :]


def _positions(idx_t, rank_t, start_col):
    K, T = idx_t.shape
    tm = TM_ROUTE
    kt = pl.BlockSpec((K, tm), lambda i: (0, i))
    return pl.pallas_call(
        _pos_kernel, grid=(T // tm,),
        in_specs=[kt, kt, pl.BlockSpec(start_col.shape, lambda i: (0, 0))],
        out_specs=kt, out_shape=jax.ShapeDtypeStruct((K, T), jnp.int32),
        compiler_params=_cparams("parallel"), name="positions")(idx_t, rank_t, start_col)


def _sc_mesh():
    return plsc.VectorSubcoreMesh(core_axis_name="core", subcore_axis_name="subcore",
                                  num_cores=SC_CORES, num_subcores=SC_SUBCORES)


def _sc_scatter_rows(parts, pos, n_out):
    T, W = parts[0].shape
    K = pos.shape[0]
    n = len(parts)
    win = SC_ROW_WINDOW

    @functools.partial(pl.kernel, out_type=[jax.ShapeDtypeStruct((n_out, W), parts[0].dtype)] * n, mesh=_sc_mesh(),
                       name="sc_dispatch")
    def k(*refs):
        x_hbms, p_hbm, o_hbms = refs[:n], refs[n], refs[n + 1:]
        for x_hbm, o_hbm in zip(x_hbms, o_hbms):
            def body(x_vmem, p_vmem, o_hbm=o_hbm):
                for j in range(K):
                    pltpu.sync_copy(x_vmem, o_hbm.at[p_vmem.at[j]])

            pltpu.emit_pipeline(
                body, grid=(T // win,),
                in_specs=[pl.BlockSpec((win, W), lambda i: (i, 0)), pl.BlockSpec((K, win), lambda i: (0, i))],
                out_specs=[], core_axis_name=("core", "subcore"),
                dimension_semantics=(pltpu.PARALLEL,))(x_hbm, p_hbm)

    return k(*parts, pos)


def _sc_gather_rows(tables, idx):
    M = idx.shape[0]
    W = tables[0].shape[1]
    n = len(tables)
    win = SC_ROW_WINDOW

    @functools.partial(pl.kernel, out_type=[jax.ShapeDtypeStruct((M, W), tables[0].dtype)] * n, mesh=_sc_mesh(),
                       name="sc_combine")
    def k(*refs):
        t_hbms, i_hbm, o_hbms = refs[:n], refs[n], refs[n + 1:]
        for t_hbm, o_hbm in zip(t_hbms, o_hbms):
            def body(i_vmem, o_vmem, t_hbm=t_hbm):
                pltpu.sync_copy(t_hbm.at[i_vmem.at[0]], o_vmem)

            pltpu.emit_pipeline(
                body, grid=(M // win,),
                in_specs=[pl.BlockSpec((1, win), lambda i: (0, i))],
                out_specs=[pl.BlockSpec((win, W), lambda i: (i, 0))],
                core_axis_name=("core", "subcore"), dimension_semantics=(pltpu.PARALLEL,))(i_hbm, o_hbm)

    return k(*tables, idx.reshape(1, M))


def _final_kernel(h1_ref, wk_ref, wg_ref, wu_ref, wd_ref, g_ref, b_ref, *refs):
    yk_refs, o_ref = refs[:ROW_PARTS], refs[ROW_PARTS]
    h1 = h1_ref[...]
    hb = h1.astype(BF16)
    g = _dot(hb, wg_ref[...])
    u = _dot(hb, wu_ref[...])
    ffn = _dot((g * _sigmoid(g) * u).astype(BF16), wd_ref[...])
    wk = wk_ref[...]
    sums = None
    for k in range(TOP_K):
        cols = _load_row_parts([r[k] for r in yk_refs])
        terms = [c * wk[:, k:k + 1] for _, c in cols]
        sums = terms if sums is None else [s + t for s, t in zip(sums, terms)]
    ffn = ffn + jnp.concatenate(sums, axis=1)
    o_ref[...] = _layer_norm(DEEPNORM_ALPHA * h1 + ffn, g_ref[...], b_ref[...])


def _final(h1, yk_parts, wk, w_sg, w_su, w_sd, g, b):
    T, D = h1.shape
    tm = TM_FINAL
    row = lambda n: pl.BlockSpec((tm, n), lambda i: (i, 0))
    full = lambda a: pl.BlockSpec(a.shape, lambda i: (0,) * a.ndim)
    return pl.pallas_call(
        _final_kernel, grid=(T // tm,),
        in_specs=[row(D), row(TOP_K), full(w_sg), full(w_su), full(w_sd), full(g), full(b)]
        + [pl.BlockSpec((TOP_K, tm, PART_WORDS), lambda i: (0, i, 0))] * ROW_PARTS,
        out_specs=row(D), out_shape=jax.ShapeDtypeStruct((T, D), F32),
        compiler_params=_cparams("parallel"), name="final")(h1, wk, w_sg, w_su, w_sd, g, b, *yk_parts)


def _block_diag_tri(n, c):
    r = np.arange(n)
    return jnp.asarray(((r[:, None] >= r[None, :]) & (r[:, None] // c == r[None, :] // c)).astype(np.float32), BF16)


def kernel(x, ln_in_g, ln_in_b, w_in, b_fgate, w_gate_up, b_gate, g_gla_norm, w_out, ln1_g, ln1_b, w_router,
           router_bias, w_exp_gate, w_exp_up, w_exp_down, w_sh_gate, w_sh_up, w_sh_down, ln2_g, ln2_b):
    B, S, D = x.shape
    T = B * S
    x2 = x.reshape(T, D)
    l = 0
    row = lambda a: a.reshape(1, -1)

    off = np.cumsum((0,) + IN_SPLITS)
    seg = lambda i: w_in[l][:, off[i]:off[i + 1]]
    w_main = jnp.concatenate([seg(0) * (FOX_HEAD_DIM ** -0.5 * LOG2E), seg(1), seg(2), seg(4) * GLA_KEY_DIM ** -0.5, seg(5),
                              seg(6), seg(8)], axis=1).astype(BF16)
    n_small = FOX_HEADS + GLA_GATE_RANK
    w_small = jnp.concatenate([seg(3), seg(7), jnp.zeros((D, LANES - n_small), F32)], axis=1).astype(BF16)
    bf_pad = jnp.concatenate([b_fgate[l], jnp.zeros((LANES - FOX_HEADS,), F32)]).reshape(1, LANES)
    wg_pad = jnp.zeros((LANES, GLA_KEY_WIDTH), F32).at[FOX_HEADS:n_small].set(w_gate_up[l]).astype(BF16)

    h, fq, fk, fv, gq, gk, gv, gr, small = _proj(x2, row(ln_in_g), row(ln_in_b), w_main, w_small)

    y_fox = _fox(fq, fk, fv, _fgate(small, bf_pad, B, S), B, S)

    y_gla = _gla(gq, gk, gv, gr, small, wg_pad, row(b_gate[l]), row(g_gla_norm[l]),
                 _block_diag_tri(TS_GLA, CHUNK), B, S)

    h1, scores_t, *h1_parts = _oproj(y_fox, y_gla, h, w_out[l].astype(BF16), row(ln1_g[l]), row(ln1_b[l]),
                                     w_router[l].T.astype(BF16))

    tm = TM_ROUTE
    r = np.arange(tm)
    upper = jnp.asarray((r[:, None] < r[None, :]).astype(np.float32), BF16)
    ones = jnp.ones((tm, tm), BF16)
    idx_t, w_t, rank_t, cnt = _route(scores_t, router_bias[l].reshape(N_EXPERTS, 1), upper, ones)

    blk, ub, E = EXPERT_ROW_BLOCK, EXPERT_UNIT_BLOCKS, N_EXPERTS
    A = T * TOP_K
    P = A + E * blk
    counts = cnt[:, 0].astype(jnp.int32)
    nblk = (counts + blk - 1) // blk
    blk_end = jnp.cumsum(nblk)
    blk_start = blk_end - nblk
    pos = _positions(idx_t, rank_t, (blk_start * blk).astype(F32).reshape(E, 1))
    nunit = (nblk + ub - 1) // ub
    unit_end = jnp.cumsum(nunit)
    ufirst = unit_end - nunit
    u = jnp.arange(E + (A // blk + E) // ub + 1, dtype=jnp.int32)
    owner = jnp.minimum(jnp.sum(unit_end[None, :] <= u[:, None], axis=1), E - 1)
    onehot = owner[:, None] == jnp.arange(E)[None, :]
    pick = lambda v: jnp.sum(jnp.where(onehot, v[None, :], 0), axis=1)
    j = u - pick(ufirst)
    n_units = jnp.maximum(pick(nunit), 1)
    base, rem = pick(nblk) // n_units, pick(nblk) % n_units
    live = u < unit_end[-1]
    usize = jnp.where(live, base + (j < rem), 1).astype(jnp.int32)
    ustart = jnp.where(live, pick(blk_start) + j * base + jnp.minimum(j, rem), 0).astype(jnp.int32)

    xs_parts = _sc_scatter_rows(h1_parts, pos, P)
    ys_parts = _experts(ufirst, nunit, ustart, usize, unit_end[-1:], xs_parts,
                        w_exp_gate[l], w_exp_up[l], w_exp_down[l])
    yk_parts = [y.reshape(TOP_K, T, PART_WORDS) for y in _sc_gather_rows(ys_parts, pos.reshape(-1))]

    out = _final(h1, yk_parts, w_t.T, w_sh_gate[l].astype(BF16), w_sh_up[l].astype(BF16),
                 w_sh_down[l].astype(BF16), row(ln2_g[l]), row(ln2_b[l]))
    return out.reshape(B, S, D)
```
